```python
import math
import jax, jax.numpy as jnp
from jax import lax
import numpy as np

D_MODEL = 2048
BATCH = 8
SEQ = 4096
DEPTH = 1

D_MIX = D_MODEL
GLA_HEADS = 4
GLA_WIDTH = D_MIX // 2
GLA_DV = GLA_WIDTH // GLA_HEADS
GLA_DK = GLA_DV // 2
GLA_RANK = 16
GLA_TAU = 16.0
GLA_CHUNK = 64
NSA_HEADS = 8
NSA_KV = 2
NSA_REP = NSA_HEADS // NSA_KV
NSA_WIDTH = D_MIX - GLA_WIDTH
NSA_DH = NSA_WIDTH // NSA_HEADS
CMP_LEN = 32
CMP_STRIDE = 16
CMP_HIDDEN = NSA_DH
SEL_BLOCK = 64
SEL_TOPN = 16
SEL_Q_CHUNK = 32
WINDOW = 512
WIN_QBLOCK = 128
N_BUCKETS = 32
MAX_DISTANCE = 128
N_EXPERTS = 32
TOP_K = 4
D_FF = D_MODEL
SWIGLU_ALPHA = 1.702
SWIGLU_LIMIT = 7.0
MOE_BLOCK = 128

RMS_EPS = 1e-5
NEG_INF = -1e30
FORCE_SCORE = 1e4

IN_SIZES = (GLA_HEADS * GLA_DK, GLA_HEADS * GLA_DK, GLA_WIDTH, GLA_WIDTH, GLA_RANK,
            NSA_WIDTH,
            NSA_KV * NSA_DH, NSA_KV * NSA_DH,
            NSA_KV * NSA_DH, NSA_KV * NSA_DH,
            NSA_KV * NSA_DH, NSA_KV * NSA_DH,
            3 * NSA_HEADS)
D_IN = sum(IN_SIZES)

kernel_name = 'hybrid_gla_nsa_moe_block'


def rmsnorm(x, g):
    xf = x.astype(jnp.float32)
    y = xf * lax.rsqrt(jnp.mean(xf * xf, axis=-1, keepdims=True) + RMS_EPS)
    return (y * g.astype(jnp.float32)).astype(x.dtype)


def t5_bucket(dist):
    n = jnp.maximum(dist, 0)
    exact = N_BUCKETS // 2
    scaled = jnp.log(jnp.maximum(n, 1).astype(jnp.float32) / exact) / math.log(MAX_DISTANCE / exact)
    large = exact + (scaled * (N_BUCKETS - exact)).astype(jnp.int32)
    return jnp.where(n < exact, n, jnp.minimum(large, N_BUCKETS - 1)).astype(jnp.int32)


def masked_softmax(logits, mask):
    return jax.nn.softmax(jnp.where(mask, logits.astype(jnp.float32), NEG_INF), axis=-1)


def gla_chunked(q, k, v, log_a):
    B, S, H, DK = q.shape
    DV = v.shape[-1]
    C = GLA_CHUNK
    NC = S // C

    def chunks(t):
        return t.reshape(B, NC, C, H, t.shape[-1]).transpose(1, 0, 3, 2, 4).astype(jnp.float32)

    qc, kc, vc = chunks(q), chunks(k), chunks(v)
    bc = jnp.cumsum(chunks(log_a), axis=3)
    causal = jnp.tril(jnp.ones((C, C), dtype=bool))[:, :, None]

    def step(state, inp):
        q_, k_, v_, b_ = inp
        o_inter = jnp.einsum('bhcd,bhdv->bhcv', q_ * jnp.exp(b_), state)
        diff = b_[:, :, :, None, :] - b_[:, :, None, :, :]
        decay = jnp.exp(jnp.where(causal, diff, -jnp.inf))
        scores = jnp.einsum('bhid,bhjd,bhijd->bhij', q_, k_, decay)
        o_intra = jnp.einsum('bhij,bhjv->bhiv', scores, v_)
        b_last = b_[:, :, -1, :]
        k_dec = k_ * jnp.exp(b_last[:, :, None, :] - b_)
        state = state * jnp.exp(b_last)[..., None] + jnp.einsum('bhjd,bhjv->bhdv', k_dec, v_)
        return state, o_inter + o_intra

    state0 = jnp.zeros((B, H, DK, DV), jnp.float32)
    _, o = lax.scan(step, state0, (qc, kc, vc, bc))
    return o.transpose(1, 0, 3, 2, 4).reshape(B, S, H, DV).astype(v.dtype)


def compress_blocks(t, pe, w1, w2):
    B, G, S, DH = t.shape
    nc = (S - CMP_LEN) // CMP_STRIDE + 1
    idx = jnp.arange(nc)[:, None] * CMP_STRIDE + jnp.arange(CMP_LEN)[None, :]
    blocks = t[:, :, idx] + pe
    flat = blocks.reshape(B, G, nc, CMP_LEN * DH)
    return jax.nn.gelu(flat @ w1) @ w2


def nsa_attention(q, kc, vc, ks, vs, kw, vw, gate_logits, pe_k, wk1, wk2, pe_v, wv1, wv2, rel_bias):
    B, S, _ = q.shape
    G, R, DH = NSA_KV, NSA_REP, NSA_DH
    f32 = jnp.float32
    q = q.reshape(B, S, G, R, DH).transpose(0, 2, 3, 1, 4) * (DH ** -0.5)

    def kv_heads(t):
        return t.reshape(B, S, G, DH).transpose(0, 2, 1, 3)

    kc, vc, ks, vs, kw, vw = (kv_heads(t) for t in (kc, vc, ks, vs, kw, vw))
    pos = jnp.arange(S)

    def head_bias(dist):
        b = rel_bias[t5_bucket(dist)]
        return jnp.moveaxis(b, -1, 0).reshape((G, R) + dist.shape).astype(f32)

    k_cmp = compress_blocks(kc, pe_k, wk1, wk2)
    v_cmp = compress_blocks(vc, pe_v, wv1, wv2)
    nc = k_cmp.shape[2]
    cmp_start = jnp.arange(nc) * CMP_STRIDE
    dist_c = pos[:, None] - (cmp_start + CMP_LEN - 1)[None, :]
    valid_c = dist_c >= 0
    logits_c = jnp.einsum('bgrsd,bgcd->bgrsc', q, k_cmp).astype(f32) + head_bias(dist_c)
    p_c = masked_softmax(logits_c, valid_c) * jnp.any(valid_c, axis=-1, keepdims=True)
    o_cmp = jnp.einsum('bgrsc,bgcd->bgrsd', p_c.astype(v_cmp.dtype), v_cmp)

    ns = S // SEL_BLOCK
    n_sel = min(SEL_TOPN, ns)
    sel_start = jnp.arange(ns) * SEL_BLOCK
    overlap = ((cmp_start[:, None] < sel_start[None, :] + SEL_BLOCK)
               & (cmp_start[:, None] + CMP_LEN > sel_start[None, :])).astype(f32)
    importance = jnp.einsum('bgrsc,cn->bgsn', p_c, overlap)
    blk = jnp.arange(ns)[None, :]
    cur = (pos // SEL_BLOCK)[:, None]
    forced = (blk == 0) | (blk == cur) | (blk == cur - 1)
    score = jnp.where(forced, FORCE_SCORE, jnp.where(blk <= cur, importance, -1.0))
    _, sel_idx = lax.top_k(score, n_sel)

    k_blocks = ks.reshape(B, G, ns, SEL_BLOCK, DH)
    v_blocks = vs.reshape(B, G, ns, SEL_BLOCK, DH)
    gather = jax.vmap(jax.vmap(lambda blocks, ix: blocks[ix]))
    bias_tbl = rel_bias.reshape(N_BUCKETS, G, R)

    def sel_chunk(c):
        s0 = c * SEL_Q_CHUNK
        q_c = lax.dynamic_slice_in_dim(q, s0, SEL_Q_CHUNK, axis=3)
        idx_c = lax.dynamic_slice_in_dim(sel_idx, s0, SEL_Q_CHUNK, axis=2)
        k_g = gather(k_blocks, idx_c)
        v_g = gather(v_blocks, idx_c)
        t_c = s0 + jnp.arange(SEL_Q_CHUNK)
        kpos = idx_c[..., None] * SEL_BLOCK + jnp.arange(SEL_BLOCK)
        dist = t_c[None, None, :, None, None] - kpos
        bias = jax.vmap(lambda tbl, bk: tbl[bk], in_axes=(1, 1), out_axes=1)(bias_tbl, t5_bucket(dist))
        bias = jnp.moveaxis(bias, -1, 2).astype(f32)
        logits = jnp.einsum('bgrqd,bgqnkd->bgrqnk', q_c, k_g).astype(f32) + bias
        mask = (dist >= 0)[:, :, None].reshape(B, G, 1, SEL_Q_CHUNK, -1)
        p = masked_softmax(logits.reshape(B, G, R, SEL_Q_CHUNK, -1), mask).reshape(logits.shape)
        return jnp.einsum('bgrqnk,bgqnkd->bgrqd', p.astype(v_g.dtype), v_g)

    o_sel = lax.map(sel_chunk, jnp.arange(S // SEL_Q_CHUNK))
    o_sel = o_sel.transpose(1, 2, 3, 0, 4, 5).reshape(B, G, R, S, DH)

    nqb = S // WIN_QBLOCK
    pre = WINDOW // WIN_QBLOCK

    def band(t):
        padded = jnp.pad(t, ((0, 0), (0, 0), (pre * WIN_QBLOCK, 0), (0, 0)))
        padded = padded.reshape(B, G, nqb + pre, WIN_QBLOCK, DH)
        return jnp.concatenate([padded[:, :, o:o + nqb] for o in range(pre + 1)], axis=3)

    k_band, v_band = band(kw), band(vw)
    qpos = jnp.arange(nqb)[:, None] * WIN_QBLOCK + jnp.arange(WIN_QBLOCK)[None, :]
    kpos_w = (jnp.arange(nqb)[:, None] - pre) * WIN_QBLOCK + jnp.arange((pre + 1) * WIN_QBLOCK)[None, :]
    dist_w = qpos[:, :, None] - kpos_w[:, None, :]
    valid_w = (dist_w >= 0) & (dist_w < WINDOW) & (kpos_w[:, None, :] >= 0)
    q_w = q.reshape(B, G, R, nqb, WIN_QBLOCK, DH)
    logits_w = jnp.einsum('bgrnqd,bgnkd->bgrnqk', q_w, k_band).astype(f32) + head_bias(dist_w)
    p_w = masked_softmax(logits_w, valid_w)
    o_win = jnp.einsum('bgrnqk,bgnkd->bgrnqd', p_w.astype(v_band.dtype), v_band).reshape(B, G, R, S, DH)

    gates = jax.nn.sigmoid(gate_logits.astype(f32)).reshape(B, S, 3, G, R).transpose(2, 0, 3, 4, 1)[..., None]
    o = gates[0] * o_cmp + gates[1] * o_sel + gates[2] * o_win
    return o.transpose(0, 3, 1, 2, 4).reshape(B, S, NSA_WIDTH).astype(q.dtype)


def hybrid_mixer(xn, w_in, gla_wa2, gla_ba, gla_norm_g, cmp_pe_k, cmp_wk1, cmp_wk2,
                 cmp_pe_v, cmp_wv1, cmp_wv2, nsa_norm_g, rel_bias, w_out):
    B, S, _ = xn.shape
    proj = xn @ w_in
    split_points = [int(p) for p in np.cumsum(IN_SIZES)[:-1]]
    (g_q, g_k, g_v, g_g, g_a, n_q, n_kc, n_vc, n_ks, n_vs, n_kw, n_vw, n_gate) = jnp.split(proj, split_points, axis=-1)

    q = g_q.reshape(B, S, GLA_HEADS, GLA_DK) * (GLA_DK ** -0.5)
    k = g_k.reshape(B, S, GLA_HEADS, GLA_DK)
    v = g_v.reshape(B, S, GLA_HEADS, GLA_DV)
    log_a = jax.nn.log_sigmoid((g_a @ gla_wa2 + gla_ba).astype(jnp.float32)) / GLA_TAU
    o_gla = gla_chunked(q, k, v, log_a.reshape(B, S, GLA_HEADS, GLA_DK))
    o_gla = rmsnorm(o_gla, gla_norm_g) * jax.nn.silu(g_g.reshape(B, S, GLA_HEADS, GLA_DV))
    o_gla = o_gla.reshape(B, S, GLA_WIDTH)

    o_nsa = nsa_attention(n_q, n_kc, n_vc, n_ks, n_vs, n_kw, n_vw, n_gate,
                          cmp_pe_k, cmp_wk1, cmp_wk2, cmp_pe_v, cmp_wv1, cmp_wv2, rel_bias)
    o_nsa = rmsnorm(o_nsa, nsa_norm_g)

    return jnp.concatenate([o_gla, o_nsa], axis=-1) @ w_out


def clamped_swiglu(hu):
    glu, lin = jnp.split(hu, 2, axis=-1)
    glu = jnp.minimum(glu, SWIGLU_LIMIT)
    lin = jnp.clip(lin, -SWIGLU_LIMIT, SWIGLU_LIMIT)
    return glu * jax.nn.sigmoid(SWIGLU_ALPHA * glu) * (lin + 1.0)


def moe_ffn(h, router_w, router_b, w1, b1, w2, b2):
    B, S, D = h.shape
    T = B * S
    xt = h.reshape(T, D)
    logits = (xt @ router_w + router_b).astype(jnp.float32)
    top_val, top_idx = lax.top_k(logits, TOP_K)
    gate = jax.nn.softmax(top_val, axis=-1).astype(h.dtype)

    A = T * TOP_K
    e_flat = top_idx.reshape(A)
    tok_flat = jnp.arange(A, dtype=jnp.int32) // TOP_K
    w_flat = gate.reshape(A)
    order = jnp.argsort(e_flat)
    e_sorted, tok_sorted, w_sorted = e_flat[order], tok_flat[order], w_flat[order]

    counts = jnp.zeros((N_EXPERTS,), jnp.int32).at[e_flat].add(1)
    padded = (counts + MOE_BLOCK - 1) // MOE_BLOCK * MOE_BLOCK
    start = jnp.cumsum(counts) - counts
    pend = jnp.cumsum(padded)
    pstart = pend - padded
    pos = pstart[e_sorted] + (jnp.arange(A, dtype=jnp.int32) - start[e_sorted])

    n_blocks = -(-A // MOE_BLOCK) + N_EXPERTS
    P = n_blocks * MOE_BLOCK
    buf_tok = jnp.full((P,), T, jnp.int32).at[pos].set(tok_sorted)
    buf_w = jnp.zeros((P,), h.dtype).at[pos].set(w_sorted)
    blk_exp = jnp.minimum(jnp.searchsorted(pend, jnp.arange(n_blocks) * MOE_BLOCK, side='right'), N_EXPERTS - 1)
    xt_pad = jnp.concatenate([xt, jnp.zeros((1, D), h.dtype)], axis=0)

    def expert_block(args):
        tok, wgt, e = args
        hu = xt_pad[tok] @ w1[e] + b1[e]
        y = clamped_swiglu(hu) @ w2[e] + b2[e]
        return y * wgt[:, None]

    ys = lax.map(expert_block, (buf_tok.reshape(n_blocks, MOE_BLOCK), buf_w.reshape(n_blocks, MOE_BLOCK), blk_exp))
    out = jax.ops.segment_sum(ys.reshape(P, D), buf_tok, num_segments=T + 1)[:T]
    return out.reshape(B, S, D).astype(h.dtype)


def setup_inputs(seed: int = 0) -> dict:
    key = jax.random.key(seed)
    ks = jax.random.split(key, 24)
    L = DEPTH

    def nrm(k, shape, scale):
        return scale * jax.random.normal(k, shape, jnp.float32)

    return {
        'x': nrm(ks[0], (BATCH, SEQ, D_MODEL), 1.0),
        'norm_mix_g': 1.0 + nrm(ks[1], (L, D_MODEL), 0.02),
        'w_in': nrm(ks[2], (L, D_MODEL, D_IN), D_MODEL ** -0.5),
        'gla_wa2': nrm(ks[3], (L, GLA_RANK, GLA_HEADS * GLA_DK), GLA_RANK ** -0.5),
        'gla_ba': nrm(ks[4], (L, GLA_HEADS * GLA_DK), 0.1),
        'gla_norm_g': 1.0 + nrm(ks[5], (L, GLA_DV), 0.02),
        'cmp_pe_k': nrm(ks[6], (L, CMP_LEN, NSA_DH), 0.02),
        'cmp_wk1': nrm(ks[7], (L, CMP_LEN * NSA_DH, CMP_HIDDEN), (CMP_LEN * NSA_DH) ** -0.5),
        'cmp_wk2': nrm(ks[8], (L, CMP_HIDDEN, NSA_DH), CMP_HIDDEN ** -0.5),
        'cmp_pe_v': nrm(ks[9], (L, CMP_LEN, NSA_DH), 0.02),
        'cmp_wv1': nrm(ks[10], (L, CMP_LEN * NSA_DH, CMP_HIDDEN), (CMP_LEN * NSA_DH) ** -0.5),
        'cmp_wv2': nrm(ks[11], (L, CMP_HIDDEN, NSA_DH), CMP_HIDDEN ** -0.5),
        'nsa_norm_g': 1.0 + nrm(ks[12], (L, NSA_WIDTH), 0.02),
        'rel_bias': nrm(ks[13], (N_BUCKETS, NSA_HEADS), 0.1),
        'w_out': nrm(ks[14], (L, D_MIX, D_MODEL), D_MIX ** -0.5),
        'norm_ffn_g': 1.0 + nrm(ks[15], (L, D_MODEL), 0.02),
        'router_w': nrm(ks[16], (L, D_MODEL, N_EXPERTS), D_MODEL ** -0.5),
        'router_b': nrm(ks[17], (L, N_EXPERTS), 0.01),
        'exp_w1': nrm(ks[18], (L, N_EXPERTS, D_MODEL, 2 * D_FF), D_MODEL ** -0.5),
        'exp_b1': nrm(ks[19], (L, N_EXPERTS, 2 * D_FF), 0.01),
        'exp_w2': nrm(ks[20], (L, N_EXPERTS, D_FF, D_MODEL), D_FF ** -0.5),
        'exp_b2': nrm(ks[21], (L, N_EXPERTS, D_MODEL), 0.01),
        'norm_final_g': 1.0 + nrm(ks[22], (D_MODEL,), 0.02),
    }


def reference(x, norm_mix_g, w_in, gla_wa2, gla_ba, gla_norm_g, cmp_pe_k, cmp_wk1, cmp_wk2,
              cmp_pe_v, cmp_wv1, cmp_wv2, nsa_norm_g, rel_bias, w_out, norm_ffn_g,
              router_w, router_b, exp_w1, exp_b1, exp_w2, exp_b2, norm_final_g):
    h = x
    for l in range(DEPTH):
        xn = rmsnorm(h, norm_mix_g[l])
        h = h + hybrid_mixer(xn, w_in[l], gla_wa2[l], gla_ba[l], gla_norm_g[l],
                             cmp_pe_k[l], cmp_wk1[l], cmp_wk2[l], cmp_pe_v[l], cmp_wv1[l], cmp_wv2[l],
                             nsa_norm_g[l], rel_bias, w_out[l])
        hn = rmsnorm(h, norm_ffn_g[l])
        h = h + moe_ffn(hn, router_w[l], router_b[l], exp_w1[l], exp_b1[l], exp_w2[l], exp_b2[l])
    return rmsnorm(h, norm_final_g)
```

```python
import functools
import math

import numpy as np
import jax
import jax.numpy as jnp
from jax import lax
from jax.experimental import pallas as pl
from jax.experimental.pallas import tpu as pltpu

F32 = jnp.float32
BF16 = jnp.bfloat16
I32 = jnp.int32

GLA_HEADS = 4
GLA_DK = 128
GLA_DV = 256
GLA_RANK = 16
GLA_TAU = 16.0
NSA_HEADS = 8
NSA_KV = 2
NSA_REP = NSA_HEADS // NSA_KV
NSA_DH = 128
CMP_LEN = 32
CMP_STRIDE = 16
SEL_BLOCK = 64
SEL_TOPN = 16
WINDOW = 512
N_BUCKETS = 32
MAX_DISTANCE = 128
TOP_K = 4
SWIGLU_ALPHA = 1.702
SWIGLU_LIMIT = 7.0
RMS_EPS = 1e-5
NEG = -1e30
FORCE_SCORE = 1e4

GLA_WIDTH = GLA_HEADS * GLA_DV
NSA_WIDTH = NSA_HEADS * NSA_DH
KV_W = NSA_KV * NSA_DH

COL_GQ = 0
COL_GK = COL_GQ + GLA_HEADS * GLA_DK
COL_GV = COL_GK + GLA_HEADS * GLA_DK
COL_GG = COL_GV + GLA_WIDTH
COL_NQ = COL_GG + GLA_WIDTH
COL_KC = COL_NQ + NSA_WIDTH
COL_VC = COL_KC + KV_W
COL_KS = COL_VC + KV_W
COL_VS = COL_KS + KV_W
COL_KW = COL_VS + KV_W
COL_VW = COL_KW + KV_W
N_MAIN = COL_VW + KV_W
LANE = 128
ATT_T = 128
GLA_C = 64
GLA_SUB = 16
GATE_COL0 = GLA_RANK

VMEM_LIMIT = 56 * 1024 * 1024

NT = (((1,), (1,)), ((), ()))
TN = (((0,), (0,)), ((), ()))


def _dot(a, b):
    return jnp.dot(a, b, preferred_element_type=F32)


def _dg(a, b, dims):
    return lax.dot_general(a, b, dims, preferred_element_type=F32)


def _sigmoid(x):
    return 1.0 / (1.0 + jnp.exp(-x))


def _log_sigmoid(z):
    return -(jnp.maximum(-z, 0.0) + jnp.log(1.0 + jnp.exp(-jnp.abs(z))))


def _rms(x, g):
    return x * lax.rsqrt(jnp.mean(x * x, axis=-1, keepdims=True) + RMS_EPS) * g


def _split3(x):
    hi = x.astype(BF16)
    r1 = x - hi.astype(F32)
    mid = r1.astype(BF16)
    lo = (r1 - mid.astype(F32)).astype(BF16)
    return hi, mid, lo


def _params(sem, vmem=VMEM_LIMIT):
    return pltpu.CompilerParams(dimension_semantics=sem, vmem_limit_bytes=vmem)


def _inproj_kernel(x_ref, g_ref, w_ref, wn_ref, wa2_ref, ba_ref, proj_ref, la_ref, nar_ref, xn_ref):
    @pl.when(pl.program_id(1) == 0)
    def _():
        xn = _rms(x_ref[...], g_ref[...]).astype(BF16)
        xn_ref[...] = xn
        nar = _dot(xn, wn_ref[...])
        nar_ref[...] = nar
        z = _dot(nar.astype(BF16), wa2_ref[...]) + ba_ref[...]
        la_ref[...] = _log_sigmoid(z) * (1.0 / GLA_TAU)

    proj_ref[...] = _dot(xn_ref[...], w_ref[...]).astype(BF16)


def _inproj(x2, g, w_main, w_nar, wa2p, ba, tm, tn):
    T, D = x2.shape
    nj = N_MAIN // tn
    return pl.pallas_call(
        _inproj_kernel,
        grid=(T // tm, nj),
        in_specs=[
            pl.BlockSpec((tm, D), lambda i, j: (i, 0)),
            pl.BlockSpec((1, D), lambda i, j: (0, 0)),
            pl.BlockSpec((D, tn), lambda i, j: (0, j)),
            pl.BlockSpec((D, LANE), lambda i, j: (0, 0)),
            pl.BlockSpec((LANE, GLA_HEADS * GLA_DK), lambda i, j: (0, 0)),
            pl.BlockSpec((1, GLA_HEADS * GLA_DK), lambda i, j: (0, 0)),
        ],
        out_specs=[
            pl.BlockSpec((tm, tn), lambda i, j: (i, j)),
            pl.BlockSpec((tm, GLA_HEADS * GLA_DK), lambda i, j: (i, 0)),
            pl.BlockSpec((tm, LANE), lambda i, j: (i, 0)),
        ],
        out_shape=[
            jax.ShapeDtypeStruct((T, N_MAIN), BF16),
            jax.ShapeDtypeStruct((T, GLA_HEADS * GLA_DK), F32),
            jax.ShapeDtypeStruct((T, LANE), F32),
        ],
        scratch_shapes=[pltpu.VMEM((tm, D), BF16)],
        compiler_params=_params(("parallel", "arbitrary")),
    )(x2, g, w_main, w_nar, wa2p, ba)


def _gla_kernel(q_ref, k_ref, v_ref, g_ref, la_ref, ng_ref, o_ref, st_ref, *, cb):
    @pl.when(pl.program_id(2) == 0)
    def _():
        st_ref[...] = jnp.zeros_like(st_ref)

    C, SUB = GLA_C, GLA_SUB
    ri = lax.broadcasted_iota(I32, (C, C), 0)
    ci = lax.broadcasted_iota(I32, (C, C), 1)
    tri = jnp.where(ri >= ci, 1.0, 0.0).astype(BF16)
    for n in range(cb // C):
        r0 = n * C
        hi, mid, lo = _split3(la_ref[r0:r0 + C, :])
        b = _dot(tri, hi) + _dot(tri, mid) + _dot(tri, lo)
        q = q_ref[r0:r0 + C, :].astype(F32)
        k = k_ref[r0:r0 + C, :].astype(F32)
        v = v_ref[r0:r0 + C, :]
        st = st_ref[...]
        o = _dg((q * jnp.exp(b)).astype(BF16), st.astype(BF16), NT)
        parts = []
        for s in range(C // SUB):
            lo_, hi_ = s * SUB, (s + 1) * SUB
            ref = b[lo_ - 1:lo_, :] if s > 0 else jnp.zeros((1, GLA_DK), F32)
            qs = (q[lo_:hi_] * jnp.exp(b[lo_:hi_] - ref)).astype(BF16)
            ks = (k[:hi_] * jnp.exp(ref - b[:hi_])).astype(BF16)
            sc = _dg(qs, ks, NT)
            row = lax.broadcasted_iota(I32, (SUB, hi_), 0) + lo_
            col = lax.broadcasted_iota(I32, (SUB, hi_), 1)
            sc = jnp.where(col <= row, sc, 0.0)
            parts.append(_dot(sc.astype(BF16), v[:hi_]))
        o = o + jnp.concatenate(parts, axis=0)
        bl = b[C - 1:C, :]
        kdec = (k * jnp.exp(bl - b)).astype(BF16)
        st_ref[...] = st * jnp.exp(bl) + _dg(v, kdec, TN)
        gg = g_ref[r0:r0 + C, :].astype(F32)
        o_ref[r0:r0 + C, :] = (_rms(o, ng_ref[...]) * (gg * _sigmoid(gg))).astype(BF16)


def _gla(proj, la, ng, B, S, cb):
    T = B * S
    ncb = S // cb
    row = lambda b, h, c: b * ncb + c
    return pl.pallas_call(
        functools.partial(_gla_kernel, cb=cb),
        grid=(B, GLA_HEADS, ncb),
        in_specs=[
            pl.BlockSpec((cb, GLA_DK), lambda b, h, c: (row(b, h, c), COL_GQ // GLA_DK + h)),
            pl.BlockSpec((cb, GLA_DK), lambda b, h, c: (row(b, h, c), COL_GK // GLA_DK + h)),
            pl.BlockSpec((cb, GLA_DV), lambda b, h, c: (row(b, h, c), COL_GV // GLA_DV + h)),
            pl.BlockSpec((cb, GLA_DV), lambda b, h, c: (row(b, h, c), COL_GG // GLA_DV + h)),
            pl.BlockSpec((cb, GLA_DK), lambda b, h, c: (row(b, h, c), h)),
            pl.BlockSpec((1, GLA_DV), lambda b, h, c: (0, 0)),
        ],
        out_specs=pl.BlockSpec((cb, GLA_DV), lambda b, h, c: (row(b, h, c), h)),
        out_shape=jax.ShapeDtypeStruct((T, GLA_WIDTH), BF16),
        scratch_shapes=[pltpu.VMEM((GLA_DV, GLA_DK), F32)],
        compiler_params=_params(("parallel", "parallel", "arbitrary")),
    )(proj, proj, proj, proj, la, ng)


def _compress_kernel(hb_ref, pe_ref, w1_ref, w2_ref, o_ref, *, ncp):
    half = CMP_STRIDE * NSA_DH
    hb = hb_ref[0, 0].astype(F32)
    pe = pe_ref[0]
    top = (hb + pe[0:1]).astype(BF16)
    bot = (hb + pe[1:2]).astype(BF16)
    a = _dot(top, w1_ref[0, :half, :])
    bm = _dot(bot, w1_ref[0, half:, :])
    pre = a + pltpu.roll(bm, ncp - 1, 0)
    cdf = 0.5 * (1.0 + jnp.tanh(math.sqrt(2.0 / math.pi) * (pre + 0.044715 * (pre * pre * pre))))
    o_ref[0, 0] = _dot((pre * cdf).astype(BF16), w2_ref[0]).astype(BF16)


def _compress(hb, pe, w1, w2):
    _, BG, ncp, half = hb.shape
    return pl.pallas_call(
        functools.partial(_compress_kernel, ncp=ncp),
        grid=(2, BG),
        in_specs=[
            pl.BlockSpec((1, 1, ncp, half), lambda t, n: (t, n, 0, 0)),
            pl.BlockSpec((1, 2, half), lambda t, n: (t, 0, 0)),
            pl.BlockSpec((1, 2 * half, NSA_DH), lambda t, n: (t, 0, 0)),
            pl.BlockSpec((1, NSA_DH, NSA_DH), lambda t, n: (t, 0, 0)),
        ],
        out_specs=pl.BlockSpec((1, 1, ncp, NSA_DH), lambda t, n: (t, n, 0, 0)),
        out_shape=jax.ShapeDtypeStruct((2, BG, ncp, NSA_DH), BF16),
        compiler_params=_params(("parallel", "parallel")),
    )(hb, pe, w1, w2)


def _cmpattn_kernel(q_ref, kc_ref, vc_ref, bias_ref, ovt_ref, o_ref, sel_ref, *, ns, ncp):
    tq = ATT_T
    i = pl.program_id(1)
    kc = kc_ref[0, 0]
    vc = vc_ref[0, 0]
    pos = i * tq + lax.broadcasted_iota(I32, (tq, 1), 0)
    row_valid = jnp.where(pos >= CMP_LEN - 1, 1.0, 0.0)
    psum = jnp.zeros((tq, ncp), F32)
    outs = []
    for r in range(NSA_REP):
        q = q_ref[:, r * NSA_DH:(r + 1) * NSA_DH]
        s = _dg(q, kc, NT) + bias_ref[r]
        e = jnp.exp(s - jnp.max(s, axis=-1, keepdims=True))
        p = e / jnp.sum(e, axis=-1, keepdims=True) * row_valid
        psum = psum + p
        outs.append(_dot(p.astype(BF16), vc))
    o_ref[...] = jnp.concatenate(outs, axis=1).astype(BF16)

    ph = psum.astype(BF16)
    pl_ = (psum - ph.astype(F32)).astype(BF16)
    imp = _dg(ovt_ref[...], ph, NT) + _dg(ovt_ref[...], pl_, NT)
    n_i = lax.broadcasted_iota(I32, (ns, tq), 0)
    cur = (i * tq + lax.broadcasted_iota(I32, (ns, tq), 1)) // SEL_BLOCK
    forced = (n_i == 0) | (n_i == cur) | (n_i == cur - 1)
    score = jnp.where(forced, FORCE_SCORE, jnp.where(n_i <= cur, imp, -1.0))
    cnt = jnp.zeros((ns, tq), F32)
    for m in range(ns):
        sm = score[m:m + 1, :]
        ge = jnp.where(sm >= score, 1.0, 0.0)
        gt = jnp.where(sm > score, 1.0, 0.0)
        cnt = cnt + jnp.where(n_i > m, ge, gt)
    sel_t = jnp.where(cnt < min(SEL_TOPN, ns), 1.0, 0.0)
    if ns < LANE:
        sel_t = jnp.concatenate([sel_t, jnp.zeros((LANE - ns, tq), F32)], axis=0)
    sel_ref[0] = sel_t.T.astype(BF16)


def _cmpattn(proj, cmp_kv, bias_c, ovt, B, S):
    T = B * S
    tq = ATT_T
    nq = S // tq
    ns = S // SEL_BLOCK
    ncp = S // CMP_STRIDE
    G = NSA_KV
    qcol = COL_NQ // (NSA_REP * NSA_DH)
    return pl.pallas_call(
        functools.partial(_cmpattn_kernel, ns=ns, ncp=ncp),
        grid=(G, nq, B),
        in_specs=[
            pl.BlockSpec((tq, NSA_REP * NSA_DH), lambda g, i, b: (b * nq + i, qcol + g)),
            pl.BlockSpec((1, 1, ncp, NSA_DH), lambda g, i, b: (0, b * G + g, 0, 0)),
            pl.BlockSpec((1, 1, ncp, NSA_DH), lambda g, i, b: (1, b * G + g, 0, 0)),
            pl.BlockSpec((NSA_REP, tq, ncp), lambda g, i, b: (g, i, 0)),
            pl.BlockSpec((ns, ncp), lambda g, i, b: (0, 0)),
        ],
        out_specs=[
            pl.BlockSpec((tq, NSA_REP * NSA_DH), lambda g, i, b: (b * nq + i, g)),
            pl.BlockSpec((1, tq, LANE), lambda g, i, b: (b * G + g, i, 0)),
        ],
        out_shape=[
            jax.ShapeDtypeStruct((T, NSA_WIDTH), BF16),
            jax.ShapeDtypeStruct((B * G, S, LANE), BF16),
        ],
        compiler_params=_params(("parallel", "parallel", "parallel")),
    )(proj, cmp_kv, cmp_kv, bias_c, ovt)


def _flash_update(s, v, m_ref, l_ref, acc_ref):
    m_old = m_ref[...]
    m_new = jnp.maximum(m_old, jnp.max(s, axis=-1, keepdims=True))
    alpha = jnp.exp(m_old - m_new)
    p = jnp.exp(s - m_new)
    l_ref[...] = alpha * l_ref[...] + jnp.sum(p, axis=-1, keepdims=True)
    acc_ref[...] = alpha * acc_ref[...] + _dot(p.astype(BF16), v)
    m_ref[...] = m_new


def _selwin_kernel(q_ref, ks_ref, vs_ref, kw_ref, vw_ref, sel_ref, d_ref, w4_ref,
                   osel_ref, owin_ref, m_ref, l_ref, acc_ref):
    t = ATT_T
    R = NSA_REP
    i = pl.program_id(2)
    q = jnp.concatenate([q_ref[:, r * NSA_DH:(r + 1) * NSA_DH] for r in range(R)], axis=0)
    sel = sel_ref[0]
    n_i = lax.broadcasted_iota(I32, (LANE, t), 0)
    c_blk = lax.broadcasted_iota(I32, (LANE, t), 1) // SEL_BLOCK
    blk_per_tile = t // SEL_BLOCK

    def init():
        m_ref[...] = jnp.full(m_ref.shape, NEG, F32)
        l_ref[...] = jnp.zeros(l_ref.shape, F32)
        acc_ref[...] = jnp.zeros(acc_ref.shape, F32)

    def tile(ref, j):
        return ref[pl.ds(pl.multiple_of(j * t, t), t), :]

    def logits(k_ref, j, add):
        s = _dg(q, tile(k_ref, j), NT).reshape(R, t, t)
        return (s + add).reshape(R * t, t)

    def sel_mask(j):
        expand = jnp.where(n_i == blk_per_tile * j + c_blk, 1.0, 0.0).astype(BF16)
        return (_dot(sel, expand) - 1.0) * (-NEG)

    def finish(o_ref):
        o = acc_ref[...] / l_ref[...]
        o_ref[...] = jnp.concatenate([o[r * t:(r + 1) * t] for r in range(R)], axis=1).astype(BF16)

    init()

    def far(j, c):
        _flash_update(logits(ks_ref, j, sel_mask(j)[None]), tile(vs_ref, j), m_ref, l_ref, acc_ref)
        return c

    lax.fori_loop(0, jnp.maximum(i - 1, 0), far, 0)
    jn = jnp.maximum(i - 1, 0)
    off = jnp.where(i == 0, NEG, 0.0)
    _flash_update(logits(ks_ref, jn, (sel_mask(jn) + off)[None] + d_ref[:, 1]), tile(vs_ref, jn),
                  m_ref, l_ref, acc_ref)
    _flash_update(logits(ks_ref, i, sel_mask(i)[None] + d_ref[:, 0]), tile(vs_ref, i),
                  m_ref, l_ref, acc_ref)
    finish(osel_ref)

    init()
    pre = WINDOW // t
    for o in range(pre, -1, -1):
        jw = i - o
        jc = jnp.maximum(jw, 0)
        off = jnp.where(jw < 0, NEG, 0.0)
        if o == 0:
            add = d_ref[:, 0] + off
        elif o == 1:
            add = d_ref[:, 1] + off
        elif o == pre:
            add = (w4_ref[...] + off)[None]
        else:
            add = off
        _flash_update(logits(kw_ref, jc, add), tile(vw_ref, jc), m_ref, l_ref, acc_ref)
    finish(owin_ref)


def _selwin(proj, sel, dtiles, w4, B, S):
    T = B * S
    t = ATT_T
    nq = S // t
    G = NSA_KV
    qcol = COL_NQ // (NSA_REP * NSA_DH)
    kv = lambda col: pl.BlockSpec((S, NSA_DH), lambda b, g, i, col=col: (b, col // NSA_DH + g))
    return pl.pallas_call(
        _selwin_kernel,
        grid=(B, G, nq),
        in_specs=[
            pl.BlockSpec((t, NSA_REP * NSA_DH), lambda b, g, i: (b * nq + i, qcol + g)),
            kv(COL_KS), kv(COL_VS), kv(COL_KW), kv(COL_VW),
            pl.BlockSpec((1, t, LANE), lambda b, g, i: (b * G + g, i, 0)),
            pl.BlockSpec((NSA_REP, 2, t, t), lambda b, g, i: (g, 0, 0, 0)),
            pl.BlockSpec((t, t), lambda b, g, i: (0, 0)),
        ],
        out_specs=[
            pl.BlockSpec((t, NSA_REP * NSA_DH), lambda b, g, i: (b * nq + i, g)),
            pl.BlockSpec((t, NSA_REP * NSA_DH), lambda b, g, i: (b * nq + i, g)),
        ],
        out_shape=[
            jax.ShapeDtypeStruct((T, NSA_WIDTH), BF16),
            jax.ShapeDtypeStruct((T, NSA_WIDTH), BF16),
        ],
        scratch_shapes=[
            pltpu.VMEM((NSA_REP * t, 1), F32),
            pltpu.VMEM((NSA_REP * t, 1), F32),
            pltpu.VMEM((NSA_REP * t, NSA_DH), F32),
        ],
        compiler_params=_params(("parallel", "parallel", "parallel")),
    )(proj, proj, proj, proj, proj, sel, dtiles, w4)


def _outproj_kernel(x_ref, og_ref, oc_ref, os_ref, ow_ref, nar_ref, ng_ref, wo_ref, fg_ref,
                    rwh_ref, rwl_ref, rb_ref, h1_ref, hn_ref, ti_ref, tw_ref):
    gates = _sigmoid(nar_ref[...])
    parts = []
    for h in range(NSA_HEADS):
        cs = slice(h * NSA_DH, (h + 1) * NSA_DH)
        g0 = gates[:, GATE_COL0 + h:GATE_COL0 + h + 1]
        g1 = gates[:, GATE_COL0 + NSA_HEADS + h:GATE_COL0 + NSA_HEADS + h + 1]
        g2 = gates[:, GATE_COL0 + 2 * NSA_HEADS + h:GATE_COL0 + 2 * NSA_HEADS + h + 1]
        parts.append(g0 * oc_ref[:, cs].astype(F32) + g1 * os_ref[:, cs].astype(F32)
                     + g2 * ow_ref[:, cs].astype(F32))
    on = _rms(jnp.concatenate(parts, axis=1), ng_ref[...]).astype(BF16)
    h1 = x_ref[...] + _dot(og_ref[...], wo_ref[:GLA_WIDTH, :]) + _dot(on, wo_ref[GLA_WIDTH:, :])
    h1_ref[...] = h1
    hn = _rms(h1, fg_ref[...])
    hn_ref[...] = hn

    hh = hn.astype(BF16)
    hl = (hn - hh.astype(F32)).astype(BF16)
    lg = _dot(hh, rwh_ref[...]) + _dot(hl, rwh_ref[...]) + _dot(hh, rwl_ref[...]) + rb_ref[...]
    lane = lax.broadcasted_iota(I32, lg.shape, 1)
    vals, idxs = [], []
    for _ in range(TOP_K):
        mx = jnp.max(lg, axis=-1, keepdims=True)
        ix = jnp.min(jnp.where(lg == mx, lane, LANE), axis=-1, keepdims=True)
        vals.append(mx)
        idxs.append(ix)
        lg = jnp.where(lane == ix, -3e38, lg)
    es = [jnp.exp(v - vals[0]) for v in vals]
    den = es[0] + es[1] + es[2] + es[3]
    ti = jnp.zeros(lg.shape, I32)
    tw = jnp.zeros(lg.shape, F32)
    for kk in range(TOP_K):
        ti = jnp.where(lane == kk, idxs[kk], ti)
        tw = jnp.where(lane == kk, es[kk] / den, tw)
    ti_ref[...] = ti
    tw_ref[...] = tw


def _outproj(x2, og, oc, osel, ow, nar, ng, wo, fg, rwh, rwl, rb, tm):
    T, D = x2.shape
    row = lambda w: pl.BlockSpec((tm, w), lambda i: (i, 0))
    full = lambda a: pl.BlockSpec(a.shape, lambda i: (0,) * a.ndim)
    return pl.pallas_call(
        _outproj_kernel,
        grid=(T // tm,),
        in_specs=[row(D), row(GLA_WIDTH), row(NSA_WIDTH), row(NSA_WIDTH), row(NSA_WIDTH), row(LANE),
                  full(ng), full(wo), full(fg), full(rwh), full(rwl), full(rb)],
        out_specs=[row(D), row(D), row(LANE), row(LANE)],
        out_shape=[
            jax.ShapeDtypeStruct((T, D), F32),
            jax.ShapeDtypeStruct((T, D), F32),
            jax.ShapeDtypeStruct((T, LANE), I32),
            jax.ShapeDtypeStruct((T, LANE), F32),
        ],
        compiler_params=_params(("parallel",)),
    )(x2, og, oc, osel, ow, nar, ng, wo, fg, rwh, rwl, rb)


def _row_copy(src_hbm, dst_vmem, sem, src_row, dst_row):
    return pltpu.make_async_copy(src_hbm.at[pl.ds(src_row, 1), :], dst_vmem.at[pl.ds(dst_row, 1), :], sem)


def _gather_rows(src_hbm, idx_smem, dst_vmem, sem, n):
    def issue(r, c):
        _row_copy(src_hbm, dst_vmem, sem, idx_smem[r], r).start()
        return c

    lax.fori_loop(0, n, issue, 0)

    def drain(r, c):
        _row_copy(src_hbm, dst_vmem, sem, 0, r).wait()
        return c

    lax.fori_loop(0, n, drain, 0)


def _moe_kernel(te_ref, tv_ref, tok_hbm, hn_hbm, wgt_ref, w1g_ref, w1l_ref, b1g_ref, b1l_ref,
                w2_ref, b2_ref, y_ref, tok_smem, xf_ref, xb_ref, sem_idx, sem_row, *, tm, nf):
    i = pl.program_id(0)
    f = pl.program_id(1)
    live = tv_ref[i] > 0

    @pl.when(live & (f == 0))
    def _():
        cp = pltpu.make_async_copy(tok_hbm.at[i], tok_smem, sem_idx)
        cp.start()
        cp.wait()
        _gather_rows(hn_hbm, tok_smem, xf_ref, sem_row, tm)
        xb_ref[...] = xf_ref[...].astype(BF16)

    @pl.when(jnp.logical_not(live) & (f == 0))
    def _():
        y_ref[...] = jnp.zeros_like(y_ref)

    @pl.when(live)
    def _():
        xb = xb_ref[...]
        glu = jnp.minimum(_dot(xb, w1g_ref[0]) + b1g_ref[0], SWIGLU_LIMIT)
        lin = jnp.clip(_dot(xb, w1l_ref[0]) + b1l_ref[0], -SWIGLU_LIMIT, SWIGLU_LIMIT)
        act = (glu * _sigmoid(SWIGLU_ALPHA * glu) * (lin + 1.0)).astype(BF16)
        part = _dot(act, w2_ref[0])

        @pl.when(f == 0)
        def _():
            y_ref[...] = part

        @pl.when(f > 0)
        def _():
            y_ref[...] += part

        @pl.when(f == nf - 1)
        def _():
            y_ref[...] = (y_ref[...] + b2_ref[0]) * wgt_ref[0]


def _moe(tile_exp, tile_valid, tok, hn, wgt, w1, b1, w2, b2, tm, fc):
    n_tiles = tok.shape[0]
    D = hn.shape[1]
    dff = w2.shape[1]
    nf = dff // fc
    grid_spec = pltpu.PrefetchScalarGridSpec(
        num_scalar_prefetch=2,
        grid=(n_tiles, nf),
        in_specs=[
            pl.BlockSpec(memory_space=pl.ANY),
            pl.BlockSpec(memory_space=pl.ANY),
            pl.BlockSpec((1, tm, 1), lambda i, f, te, tv: (i, 0, 0)),
            pl.BlockSpec((1, D, fc), lambda i, f, te, tv: (te[i], 0, f)),
            pl.BlockSpec((1, D, fc), lambda i, f, te, tv: (te[i], 0, nf + f)),
            pl.BlockSpec((1, 1, fc), lambda i, f, te, tv: (te[i], 0, f)),
            pl.BlockSpec((1, 1, fc), lambda i, f, te, tv: (te[i], 0, nf + f)),
            pl.BlockSpec((1, fc, D), lambda i, f, te, tv: (te[i], f, 0)),
            pl.BlockSpec((1, 1, D), lambda i, f, te, tv: (te[i], 0, 0)),
        ],
        out_specs=pl.BlockSpec((tm, D), lambda i, f, te, tv: (i, 0)),
        scratch_shapes=[
            pltpu.SMEM((tm,), I32),
            pltpu.VMEM((tm, D), F32),
            pltpu.VMEM((tm, D), BF16),
            pltpu.SemaphoreType.DMA(()),
            pltpu.SemaphoreType.DMA(()),
        ],
    )
    return pl.pallas_call(
        functools.partial(_moe_kernel, tm=tm, nf=nf),
        grid_spec=grid_spec,
        out_shape=jax.ShapeDtypeStruct((n_tiles * tm, D), F32),
        compiler_params=_params(("arbitrary", "arbitrary")),
    )(tile_exp, tile_valid, tok, hn, wgt, w1, w1, b1, b1, w2, b2)


def _combine_kernel(pos_hbm, ys_hbm, h1_ref, g_ref, o_ref, pos_smem, buf_ref, sem_idx, sem_row, *, tm):
    i = pl.program_id(0)
    cp = pltpu.make_async_copy(pos_hbm.at[i], pos_smem, sem_idx)
    cp.start()
    cp.wait()
    _gather_rows(ys_hbm, pos_smem, buf_ref, sem_row, TOP_K * tm)
    h = h1_ref[...]
    for kk in range(TOP_K):
        h = h + buf_ref[kk * tm:(kk + 1) * tm, :]
    o_ref[...] = _rms(h, g_ref[...])


def _combine(pos, ys, h1, g, tm):
    T, D = h1.shape
    return pl.pallas_call(
        functools.partial(_combine_kernel, tm=tm),
        grid=(T // tm,),
        in_specs=[
            pl.BlockSpec(memory_space=pl.ANY),
            pl.BlockSpec(memory_space=pl.ANY),
            pl.BlockSpec((tm, D), lambda i: (i, 0)),
            pl.BlockSpec((1, D), lambda i: (0, 0)),
        ],
        out_specs=pl.BlockSpec((tm, D), lambda i: (i, 0)),
        out_shape=jax.ShapeDtypeStruct((T, D), F32),
        scratch_shapes=[
            pltpu.SMEM((TOP_K * tm,), I32),
            pltpu.VMEM((TOP_K * tm, D), F32),
            pltpu.SemaphoreType.DMA(()),
            pltpu.SemaphoreType.DMA(()),
        ],
        compiler_params=_params(("arbitrary",)),
    )(pos, ys, h1, g)


def _t5_bucket(dist):
    n = jnp.maximum(dist, 0)
    exact = N_BUCKETS // 2
    scaled = jnp.log(jnp.maximum(n, 1).astype(F32) / exact) / math.log(MAX_DISTANCE / exact)
    large = exact + (scaled * (N_BUCKETS - exact)).astype(I32)
    return jnp.where(n < exact, n, jnp.minimum(large, N_BUCKETS - 1)).astype(I32)


def _bias_tables(rel_bias, S):
    far = rel_bias[N_BUCKETS - 1]
    tab = (rel_bias[_t5_bucket(jnp.arange(S))] - far[None, :]).T.astype(F32)
    t = ATT_T
    r = jnp.arange(t)[:, None]
    c = jnp.arange(t)[None, :]
    d0 = r - c
    d0_tile = jnp.where(d0 >= 0, tab[:, jnp.maximum(d0, 0)], NEG)
    d1_tile = tab[:, t + r - c]
    dtiles = jnp.stack([d0_tile, d1_tile], axis=1)
    w4 = jnp.where(r < c, 0.0, NEG).astype(F32)
    ncp = S // CMP_STRIDE
    dist_c = jnp.arange(S)[:, None] - (jnp.arange(ncp) * CMP_STRIDE + CMP_LEN - 1)[None, :]
    bias_c = jnp.where(dist_c >= 0, tab[:, jnp.maximum(dist_c, 0)], NEG)
    return dtiles.astype(F32), w4, bias_c.astype(F32)


def _overlap_t(S):
    ncp = S // CMP_STRIDE
    ns = S // SEL_BLOCK
    cs = jnp.arange(ncp)[None, :] * CMP_STRIDE
    ss = jnp.arange(ns)[:, None] * SEL_BLOCK
    ov = (cs < ss + SEL_BLOCK) & (cs + CMP_LEN > ss) & (jnp.arange(ncp)[None, :] < ncp - 1)
    return ov.astype(BF16)


def _route(top_idx, top_w, n_exp, tm):
    T = top_idx.shape[0]
    A = T * TOP_K
    e_flat = top_idx.reshape(A)
    w_flat = top_w.reshape(A)
    order = jnp.argsort(e_flat, stable=True).astype(I32)
    e_sorted = e_flat[order]
    counts = jnp.zeros((n_exp,), I32).at[e_flat].add(1)
    padded = (counts + tm - 1) // tm * tm
    start = jnp.cumsum(counts) - counts
    pend = jnp.cumsum(padded)
    pstart = pend - padded
    pos_sorted = pstart[e_sorted] + (jnp.arange(A, dtype=I32) - start[e_sorted])
    n_tiles = A // tm + n_exp
    P = n_tiles * tm
    buf_tok = jnp.zeros((P,), I32).at[pos_sorted].set(order // TOP_K)
    buf_w = jnp.zeros((P,), F32).at[pos_sorted].set(w_flat[order])
    pos = jnp.zeros((A,), I32).at[order].set(pos_sorted).reshape(T, TOP_K)
    tile_row0 = jnp.arange(n_tiles, dtype=I32) * tm
    tile_exp = jnp.minimum(jnp.searchsorted(pend, tile_row0, side='right'), n_exp - 1).astype(I32)
    tile_valid = jnp.clip(pstart[tile_exp] + counts[tile_exp] - tile_row0, 0, tm).astype(I32)
    return buf_tok.reshape(n_tiles, tm), buf_w.reshape(n_tiles, tm, 1), pos, tile_exp, tile_valid


def kernel(x, norm_mix_g, w_in, gla_wa2, gla_ba, gla_norm_g, cmp_pe_k, cmp_wk1, cmp_wk2, cmp_pe_v,
           cmp_wv1, cmp_wv2, nsa_norm_g, rel_bias, w_out, norm_ffn_g, router_w, router_b, exp_w1,
           exp_b1, exp_w2, exp_b2, norm_final_g):
    B, S, D = x.shape
    T = B * S
    G = NSA_KV
    n_exp = router_w.shape[-1]
    assert w_in.shape[0] == 1 and S % 1024 == 0 and S // SEL_BLOCK <= LANE and n_exp <= LANE
    x2 = x.reshape(T, D)

    w = w_in[0]
    a0 = COL_NQ
    n0 = a0 + GLA_RANK
    g0 = n0 + (N_MAIN - COL_NQ)
    w_main = jnp.concatenate([
        w[:, :COL_GK] * (GLA_DK ** -0.5), w[:, COL_GK:a0],
        w[:, n0:n0 + NSA_WIDTH] * (NSA_DH ** -0.5), w[:, n0 + NSA_WIDTH:g0]], axis=1).astype(BF16)
    w_nar = jnp.concatenate([w[:, a0:n0], w[:, g0:], jnp.zeros((D, LANE - GLA_RANK - 3 * NSA_HEADS), F32)],
                            axis=1).astype(BF16)
    wa2p = jnp.concatenate([gla_wa2[0], jnp.zeros((LANE - GLA_RANK, GLA_HEADS * GLA_DK), F32)], axis=0).astype(BF16)

    tm1 = min(1024, T)
    proj, log_a, nar = _inproj(x2, norm_mix_g, w_main, w_nar, wa2p, gla_ba, tm1, 512)

    o_gla = _gla(proj, log_a, gla_norm_g, B, S, min(512, S))

    ncp = S // CMP_STRIDE
    half = CMP_STRIDE * NSA_DH

    def stride_groups(col):
        t4 = proj[:, col:col + KV_W].reshape(B, ncp, CMP_STRIDE, G, NSA_DH)
        return t4.transpose(0, 3, 1, 2, 4).reshape(B * G, ncp, half)

    hb = jnp.stack([stride_groups(COL_KC), stride_groups(COL_VC)], axis=0)
    pe = jnp.stack([cmp_pe_k[0].reshape(2, half), cmp_pe_v[0].reshape(2, half)], axis=0)
    w1c = jnp.stack([cmp_wk1[0], cmp_wv1[0]], axis=0).astype(BF16)
    w2c = jnp.stack([cmp_wk2[0], cmp_wv2[0]], axis=0).astype(BF16)
    cmp_kv = _compress(hb, pe, w1c, w2c)

    dtiles, w4, bias_c = _bias_tables(rel_bias, S)
    o_cmp, sel = _cmpattn(proj, cmp_kv, bias_c, _overlap_t(S), B, S)
    o_sel, o_win = _selwin(proj, sel, dtiles, w4, B, S)

    rw = jnp.concatenate([router_w[0], jnp.zeros((D, LANE - n_exp), F32)], axis=1)
    rwh = rw.astype(BF16)
    rwl = (rw - rwh.astype(F32)).astype(BF16)
    rb = jnp.concatenate([router_b[0], jnp.full((LANE - n_exp,), NEG, F32)])[None, :]
    h1, hn, top_i, top_w = _outproj(x2, o_gla, o_cmp, o_sel, o_win, nar, nsa_norm_g, w_out[0].astype(BF16),
                                    norm_ffn_g, rwh, rwl, rb, min(256, T))

    tm6 = min(1024, T)
    tok, wgt, pos, tile_exp, tile_valid = _route(top_i[:, :TOP_K], top_w[:, :TOP_K], n_exp, tm6)
    ys = _moe(tile_exp, tile_valid, tok, hn, wgt, exp_w1[0].astype(BF16), exp_b1[0][:, None, :],
              exp_w2[0].astype(BF16), exp_b2[0][:, None, :], tm6, 512)
    tm7 = min(256, T)
    pos_t = pos.reshape(T // tm7, tm7, TOP_K).transpose(0, 2, 1).reshape(T // tm7, TOP_K * tm7)
    out = _combine(pos_t, ys, h1, norm_final_g[None, :], tm7)
    return out.reshape(B, S, D)
```

```python
import functools
import math

import numpy as np
import jax
import jax.numpy as jnp
from jax import lax
from jax.experimental import pallas as pl
from jax.experimental.pallas import tpu as pltpu

F32 = jnp.float32
BF16 = jnp.bfloat16
I32 = jnp.int32

GLA_HEADS = 4
GLA_DK = 128
GLA_DV = 256
GLA_RANK = 16
GLA_TAU = 16.0
NSA_HEADS = 8
NSA_KV = 2
NSA_REP = NSA_HEADS // NSA_KV
NSA_DH = 128
CMP_LEN = 32
CMP_STRIDE = 16
SEL_BLOCK = 64
SEL_TOPN = 16
WINDOW = 512
N_BUCKETS = 32
MAX_DISTANCE = 128
TOP_K = 4
SWIGLU_ALPHA = 1.702
SWIGLU_LIMIT = 7.0
RMS_EPS = 1e-5
NEG = -1e30
FORCE_SCORE = 1e4

GLA_WIDTH = GLA_HEADS * GLA_DV
NSA_WIDTH = NSA_HEADS * NSA_DH
KV_W = NSA_KV * NSA_DH

COL_GQ = 0
COL_GK = COL_GQ + GLA_HEADS * GLA_DK
COL_GV = COL_GK + GLA_HEADS * GLA_DK
COL_GG = COL_GV + GLA_WIDTH
COL_NQ = COL_GG + GLA_WIDTH
COL_KC = COL_NQ + NSA_WIDTH
COL_VC = COL_KC + KV_W
COL_KS = COL_VC + KV_W
COL_VS = COL_KS + KV_W
COL_KW = COL_VS + KV_W
COL_VW = COL_KW + KV_W
N_MAIN = COL_VW + KV_W
LANE = 128
ATT_T = 128
SEL_T = 256
GLA_C = 64
GLA_SUB = 16
GATE_COL0 = GLA_RANK

VMEM_LIMIT = 56 * 1024 * 1024

NT = (((1,), (1,)), ((), ()))
TN = (((0,), (0,)), ((), ()))


def _dot(a, b):
    return jnp.dot(a, b, preferred_element_type=F32)


def _dg(a, b, dims):
    return lax.dot_general(a, b, dims, preferred_element_type=F32)


def _sigmoid(x):
    return 1.0 / (1.0 + jnp.exp(-x))


def _log_sigmoid(z):
    return -(jnp.maximum(-z, 0.0) + jnp.log(1.0 + jnp.exp(-jnp.abs(z))))


def _rms(x, g):
    return x * lax.rsqrt(jnp.mean(x * x, axis=-1, keepdims=True) + RMS_EPS) * g


def _split3(x):
    hi = x.astype(BF16)
    r1 = x - hi.astype(F32)
    mid = r1.astype(BF16)
    lo = (r1 - mid.astype(F32)).astype(BF16)
    return hi, mid, lo


def _params(sem, vmem=VMEM_LIMIT):
    return pltpu.CompilerParams(dimension_semantics=sem, vmem_limit_bytes=vmem)


def _inproj_kernel(x_ref, g_ref, w_ref, wn_ref, wa2_ref, ba_ref, proj_ref, la_ref, nar_ref, xn_ref):
    @pl.when(pl.program_id(1) == 0)
    def _():
        xn = _rms(x_ref[...], g_ref[...]).astype(BF16)
        xn_ref[...] = xn
        nar = _dot(xn, wn_ref[...])
        nar_ref[...] = nar
        z = _dot(nar.astype(BF16), wa2_ref[...]) + ba_ref[...]
        la_ref[...] = _log_sigmoid(z) * (1.0 / GLA_TAU)

    proj_ref[...] = _dot(xn_ref[...], w_ref[...]).astype(BF16)


def _inproj(x2, g, w_main, w_nar, wa2p, ba, tm, tn):
    T, D = x2.shape
    nj = N_MAIN // tn
    return pl.pallas_call(
        _inproj_kernel,
        grid=(T // tm, nj),
        in_specs=[
            pl.BlockSpec((tm, D), lambda i, j: (i, 0)),
            pl.BlockSpec((1, D), lambda i, j: (0, 0)),
            pl.BlockSpec((D, tn), lambda i, j: (0, j)),
            pl.BlockSpec((D, LANE), lambda i, j: (0, 0)),
            pl.BlockSpec((LANE, GLA_HEADS * GLA_DK), lambda i, j: (0, 0)),
            pl.BlockSpec((1, GLA_HEADS * GLA_DK), lambda i, j: (0, 0)),
        ],
        out_specs=[
            pl.BlockSpec((tm, tn), lambda i, j: (i, j)),
            pl.BlockSpec((tm, GLA_HEADS * GLA_DK), lambda i, j: (i, 0)),
            pl.BlockSpec((tm, LANE), lambda i, j: (i, 0)),
        ],
        out_shape=[
            jax.ShapeDtypeStruct((T, N_MAIN), BF16),
            jax.ShapeDtypeStruct((T, GLA_HEADS * GLA_DK), F32),
            jax.ShapeDtypeStruct((T, LANE), F32),
        ],
        scratch_shapes=[pltpu.VMEM((tm, D), BF16)],
        compiler_params=_params(("parallel", "arbitrary")),
    )(x2, g, w_main, w_nar, wa2p, ba)


def _gla_kernel(q_ref, k_ref, v_ref, g_ref, la_ref, ng_ref, o_ref, st_ref, *, cb):
    @pl.when(pl.program_id(2) == 0)
    def _():
        st_ref[...] = jnp.zeros_like(st_ref)

    C, SUB = GLA_C, GLA_SUB
    ri = lax.broadcasted_iota(I32, (C, C), 0)
    ci = lax.broadcasted_iota(I32, (C, C), 1)
    tri = jnp.where(ri >= ci, 1.0, 0.0).astype(BF16)
    for n in range(cb // C):
        r0 = n * C
        hi, mid, lo = _split3(la_ref[r0:r0 + C, :])
        b = _dot(tri, hi) + _dot(tri, mid) + _dot(tri, lo)
        q = q_ref[r0:r0 + C, :].astype(F32)
        k = k_ref[r0:r0 + C, :].astype(F32)
        v = v_ref[r0:r0 + C, :]
        st = st_ref[...]
        o = _dg((q * jnp.exp(b)).astype(BF16), st.astype(BF16), NT)
        parts = []
        for s in range(C // SUB):
            lo_, hi_ = s * SUB, (s + 1) * SUB
            ref = b[lo_ - 1:lo_, :] if s > 0 else jnp.zeros((1, GLA_DK), F32)
            qs = (q[lo_:hi_] * jnp.exp(b[lo_:hi_] - ref)).astype(BF16)
            ks = (k[:hi_] * jnp.exp(ref - b[:hi_])).astype(BF16)
            sc = _dg(qs, ks, NT)
            row = lax.broadcasted_iota(I32, (SUB, hi_), 0) + lo_
            col = lax.broadcasted_iota(I32, (SUB, hi_), 1)
            sc = jnp.where(col <= row, sc, 0.0)
            parts.append(_dot(sc.astype(BF16), v[:hi_]))
        o = o + jnp.concatenate(parts, axis=0)
        bl = b[C - 1:C, :]
        kdec = (k * jnp.exp(bl - b)).astype(BF16)
        st_ref[...] = st * jnp.exp(bl) + _dg(v, kdec, TN)
        gg = g_ref[r0:r0 + C, :].astype(F32)
        o_ref[r0:r0 + C, :] = (_rms(o, ng_ref[...]) * (gg * _sigmoid(gg))).astype(BF16)


def _gla(proj, la, ng, B, S, cb):
    T = B * S
    ncb = S // cb
    row = lambda b, h, c: b * ncb + c
    return pl.pallas_call(
        functools.partial(_gla_kernel, cb=cb),
        grid=(B, GLA_HEADS, ncb),
        in_specs=[
            pl.BlockSpec((cb, GLA_DK), lambda b, h, c: (row(b, h, c), COL_GQ // GLA_DK + h)),
            pl.BlockSpec((cb, GLA_DK), lambda b, h, c: (row(b, h, c), COL_GK // GLA_DK + h)),
            pl.BlockSpec((cb, GLA_DV), lambda b, h, c: (row(b, h, c), COL_GV // GLA_DV + h)),
            pl.BlockSpec((cb, GLA_DV), lambda b, h, c: (row(b, h, c), COL_GG // GLA_DV + h)),
            pl.BlockSpec((cb, GLA_DK), lambda b, h, c: (row(b, h, c), h)),
            pl.BlockSpec((1, GLA_DV), lambda b, h, c: (0, 0)),
        ],
        out_specs=pl.BlockSpec((cb, GLA_DV), lambda b, h, c: (row(b, h, c), h)),
        out_shape=jax.ShapeDtypeStruct((T, GLA_WIDTH), BF16),
        scratch_shapes=[pltpu.VMEM((GLA_DV, GLA_DK), F32)],
        compiler_params=_params(("parallel", "parallel", "arbitrary")),
    )(proj, proj, proj, proj, la, ng)


def _compress_kernel(hb_ref, pe_ref, w1_ref, w2_ref, o_ref, *, ncp):
    half = CMP_STRIDE * NSA_DH
    hb = hb_ref[0, 0].astype(F32)
    pe = pe_ref[0]
    top = (hb + pe[0:1]).astype(BF16)
    bot = (hb + pe[1:2]).astype(BF16)
    a = _dot(top, w1_ref[0, :half, :])
    bm = _dot(bot, w1_ref[0, half:, :])
    pre = a + pltpu.roll(bm, ncp - 1, 0)
    cdf = 0.5 * (1.0 + jnp.tanh(math.sqrt(2.0 / math.pi) * (pre + 0.044715 * (pre * pre * pre))))
    o_ref[0, 0] = _dot((pre * cdf).astype(BF16), w2_ref[0]).astype(BF16)


def _compress(hb, pe, w1, w2):
    _, BG, ncp, half = hb.shape
    return pl.pallas_call(
        functools.partial(_compress_kernel, ncp=ncp),
        grid=(2, BG),
        in_specs=[
            pl.BlockSpec((1, 1, ncp, half), lambda t, n: (t, n, 0, 0)),
            pl.BlockSpec((1, 2, half), lambda t, n: (t, 0, 0)),
            pl.BlockSpec((1, 2 * half, NSA_DH), lambda t, n: (t, 0, 0)),
            pl.BlockSpec((1, NSA_DH, NSA_DH), lambda t, n: (t, 0, 0)),
        ],
        out_specs=pl.BlockSpec((1, 1, ncp, NSA_DH), lambda t, n: (t, n, 0, 0)),
        out_shape=jax.ShapeDtypeStruct((2, BG, ncp, NSA_DH), BF16),
        compiler_params=_params(("parallel", "parallel")),
    )(hb, pe, w1, w2)


def _cmpattn_kernel(q_ref, kc_ref, vc_ref, bias_ref, ovt_ref, o_ref, sel_ref, *, ns, ncp):
    tq = ATT_T
    i = pl.program_id(1)
    kc = kc_ref[0, 0]
    vc = vc_ref[0, 0]
    pos = i * tq + lax.broadcasted_iota(I32, (tq, 1), 0)
    row_valid = jnp.where(pos >= CMP_LEN - 1, 1.0, 0.0)
    psum = jnp.zeros((tq, ncp), F32)
    outs = []
    for r in range(NSA_REP):
        q = q_ref[:, r * NSA_DH:(r + 1) * NSA_DH]
        s = _dg(q, kc, NT) + bias_ref[r]
        e = jnp.exp(s - jnp.max(s, axis=-1, keepdims=True))
        p = e / jnp.sum(e, axis=-1, keepdims=True) * row_valid
        psum = psum + p
        outs.append(_dot(p.astype(BF16), vc))
    o_ref[...] = jnp.concatenate(outs, axis=1).astype(BF16)

    ph = psum.astype(BF16)
    pl_ = (psum - ph.astype(F32)).astype(BF16)
    imp = _dg(ovt_ref[...], ph, NT) + _dg(ovt_ref[...], pl_, NT)
    n_i = lax.broadcasted_iota(I32, (ns, tq), 0)
    cur = (i * tq + lax.broadcasted_iota(I32, (ns, tq), 1)) // SEL_BLOCK
    forced = (n_i == 0) | (n_i == cur) | (n_i == cur - 1)
    score = jnp.where(forced, FORCE_SCORE, jnp.where(n_i <= cur, imp, -1.0))
    cnt = jnp.zeros((ns, tq), F32)
    for m in range(ns):
        sm = score[m:m + 1, :]
        ge = jnp.where(sm >= score, 1.0, 0.0)
        gt = jnp.where(sm > score, 1.0, 0.0)
        cnt = cnt + jnp.where(n_i > m, ge, gt)
    sel_t = jnp.where(cnt < min(SEL_TOPN, ns), 1.0, 0.0)
    if ns < LANE:
        sel_t = jnp.concatenate([sel_t, jnp.zeros((LANE - ns, tq), F32)], axis=0)
    sel_ref[0] = sel_t.T.astype(BF16)


def _cmpattn(proj, cmp_kv, bias_c, ovt, B, S):
    T = B * S
    tq = ATT_T
    nq = S // tq
    ns = S // SEL_BLOCK
    ncp = S // CMP_STRIDE
    G = NSA_KV
    qcol = COL_NQ // (NSA_REP * NSA_DH)
    return pl.pallas_call(
        functools.partial(_cmpattn_kernel, ns=ns, ncp=ncp),
        grid=(G, nq, B),
        in_specs=[
            pl.BlockSpec((tq, NSA_REP * NSA_DH), lambda g, i, b: (b * nq + i, qcol + g)),
            pl.BlockSpec((1, 1, ncp, NSA_DH), lambda g, i, b: (0, b * G + g, 0, 0)),
            pl.BlockSpec((1, 1, ncp, NSA_DH), lambda g, i, b: (1, b * G + g, 0, 0)),
            pl.BlockSpec((NSA_REP, tq, ncp), lambda g, i, b: (g, i, 0)),
            pl.BlockSpec((ns, ncp), lambda g, i, b: (0, 0)),
        ],
        out_specs=[
            pl.BlockSpec((tq, NSA_REP * NSA_DH), lambda g, i, b: (b * nq + i, g)),
            pl.BlockSpec((1, tq, LANE), lambda g, i, b: (b * G + g, i, 0)),
        ],
        out_shape=[
            jax.ShapeDtypeStruct((T, NSA_WIDTH), BF16),
            jax.ShapeDtypeStruct((B * G, S, LANE), BF16),
        ],
        compiler_params=_params(("parallel", "parallel", "parallel")),
    )(proj, cmp_kv, cmp_kv, bias_c, ovt)


def _flash_update(s, v, m_ref, al_ref):
    reps = s.shape[1] // LANE
    m_old = m_ref[...]
    m_new = jnp.maximum(m_old, jnp.max(s, axis=-1, keepdims=True))
    alpha = jnp.exp(m_old - m_new)
    p = jnp.exp(s - jnp.concatenate([m_new] * reps, axis=1))
    v_ext = jnp.concatenate([v, jnp.ones(v.shape, BF16)], axis=1)
    al_ref[...] = jnp.concatenate([alpha, alpha], axis=1) * al_ref[...] + _dot(p.astype(BF16), v_ext)
    m_ref[...] = m_new


def _selwin_kernel(q_ref, ks_ref, vs_ref, kw_ref, vw_ref, sel_ref, d_ref, w4_ref,
                   osel_ref, owin_ref, m_ref, al_ref):
    t = SEL_T
    R = NSA_REP
    i = pl.program_id(2)
    q = jnp.concatenate([q_ref[:, r * NSA_DH:(r + 1) * NSA_DH] for r in range(R)], axis=0)
    sel = sel_ref[0]
    n_i = lax.broadcasted_iota(I32, (LANE, t), 0)
    c_blk = lax.broadcasted_iota(I32, (LANE, t), 1) // SEL_BLOCK
    blk_per_tile = t // SEL_BLOCK

    def init():
        m_ref[...] = jnp.full(m_ref.shape, NEG, F32)
        al_ref[...] = jnp.zeros(al_ref.shape, F32)

    def tile(ref, j):
        return ref[pl.ds(pl.multiple_of(j * t, t), t), :]

    def logits(k_ref, j, add):
        s = _dg(q, tile(k_ref, j), NT).reshape(R, t, t)
        return (s + add).reshape(R * t, t)

    def sel_mask(j):
        expand = jnp.where(n_i == blk_per_tile * j + c_blk, 1.0, 0.0).astype(BF16)
        return (_dot(sel, expand) - 1.0) * (-NEG)

    def finish(o_ref):
        al = al_ref[...]
        o = al[:, :NSA_DH] / al[:, NSA_DH:]
        o_ref[...] = jnp.concatenate([o[r * t:(r + 1) * t] for r in range(R)], axis=1).astype(BF16)

    init()

    def far(j, c):
        _flash_update(logits(ks_ref, j, sel_mask(j)[None]), tile(vs_ref, j), m_ref, al_ref)
        return c

    lax.fori_loop(0, jnp.maximum(i - 1, 0), far, 0)
    jn = jnp.maximum(i - 1, 0)
    off = jnp.where(i == 0, NEG, 0.0)
    _flash_update(logits(ks_ref, jn, (sel_mask(jn) + off)[None] + d_ref[:, 1]), tile(vs_ref, jn),
                  m_ref, al_ref)
    _flash_update(logits(ks_ref, i, sel_mask(i)[None] + d_ref[:, 0]), tile(vs_ref, i),
                  m_ref, al_ref)
    finish(osel_ref)

    init()
    pre = WINDOW // t
    for o in range(pre, -1, -1):
        jw = i - o
        jc = jnp.maximum(jw, 0)
        off = jnp.where(jw < 0, NEG, 0.0)
        if o == 0:
            add = d_ref[:, 0] + off
        elif o == 1:
            add = d_ref[:, 1] + off
        elif o == pre:
            add = (w4_ref[...] + off)[None]
        else:
            add = off
        _flash_update(logits(kw_ref, jc, add), tile(vw_ref, jc), m_ref, al_ref)
    finish(owin_ref)


def _selwin(proj, sel, dtiles, w4, B, S):
    T = B * S
    t = SEL_T
    nq = S // t
    G = NSA_KV
    qcol = COL_NQ // (NSA_REP * NSA_DH)
    kv = lambda col: pl.BlockSpec((S, NSA_DH), lambda b, g, i, col=col: (b, col // NSA_DH + g))
    return pl.pallas_call(
        _selwin_kernel,
        grid=(B, G, nq),
        in_specs=[
            pl.BlockSpec((t, NSA_REP * NSA_DH), lambda b, g, i: (b * nq + i, qcol + g)),
            kv(COL_KS), kv(COL_VS), kv(COL_KW), kv(COL_VW),
            pl.BlockSpec((1, t, LANE), lambda b, g, i: (b * G + g, i, 0)),
            pl.BlockSpec((NSA_REP, 2, t, t), lambda b, g, i: (g, 0, 0, 0)),
            pl.BlockSpec((t, t), lambda b, g, i: (0, 0)),
        ],
        out_specs=[
            pl.BlockSpec((t, NSA_REP * NSA_DH), lambda b, g, i: (b * nq + i, g)),
            pl.BlockSpec((t, NSA_REP * NSA_DH), lambda b, g, i: (b * nq + i, g)),
        ],
        out_shape=[
            jax.ShapeDtypeStruct((T, NSA_WIDTH), BF16),
            jax.ShapeDtypeStruct((T, NSA_WIDTH), BF16),
        ],
        scratch_shapes=[
            pltpu.VMEM((NSA_REP * t, LANE), F32),
            pltpu.VMEM((NSA_REP * t, 2 * NSA_DH), F32),
        ],
        compiler_params=_params(("parallel", "parallel", "parallel")),
    )(proj, proj, proj, proj, proj, sel, dtiles, w4)


def _outproj_kernel(x_ref, og_ref, oc_ref, os_ref, ow_ref, nar_ref, ng_ref, wo_ref, fg_ref,
                    rwh_ref, rwl_ref, rb_ref, h1_ref, hn_ref, ti_ref, tw_ref):
    gates = _sigmoid(nar_ref[...])
    parts = []
    for h in range(NSA_HEADS):
        cs = slice(h * NSA_DH, (h + 1) * NSA_DH)
        g0 = gates[:, GATE_COL0 + h:GATE_COL0 + h + 1]
        g1 = gates[:, GATE_COL0 + NSA_HEADS + h:GATE_COL0 + NSA_HEADS + h + 1]
        g2 = gates[:, GATE_COL0 + 2 * NSA_HEADS + h:GATE_COL0 + 2 * NSA_HEADS + h + 1]
        parts.append(g0 * oc_ref[:, cs].astype(F32) + g1 * os_ref[:, cs].astype(F32)
                     + g2 * ow_ref[:, cs].astype(F32))
    on = _rms(jnp.concatenate(parts, axis=1), ng_ref[...]).astype(BF16)
    h1 = x_ref[...] + _dot(og_ref[...], wo_ref[:GLA_WIDTH, :]) + _dot(on, wo_ref[GLA_WIDTH:, :])
    h1_ref[...] = h1
    hn = _rms(h1, fg_ref[...])
    hn_ref[...] = hn

    hh = hn.astype(BF16)
    hl = (hn - hh.astype(F32)).astype(BF16)
    lg = _dot(hh, rwh_ref[...]) + _dot(hl, rwh_ref[...]) + _dot(hh, rwl_ref[...]) + rb_ref[...]
    lane = lax.broadcasted_iota(I32, lg.shape, 1)
    vals, idxs = [], []
    for _ in range(TOP_K):
        mx = jnp.max(lg, axis=-1, keepdims=True)
        ix = jnp.min(jnp.where(lg == mx, lane, LANE), axis=-1, keepdims=True)
        vals.append(mx)
        idxs.append(ix)
        lg = jnp.where(lane == ix, -3e38, lg)
    es = [jnp.exp(v - vals[0]) for v in vals]
    den = es[0] + es[1] + es[2] + es[3]
    ti = jnp.zeros(lg.shape, I32)
    tw = jnp.zeros(lg.shape, F32)
    for kk in range(TOP_K):
        ti = jnp.where(lane == kk, idxs[kk], ti)
        tw = jnp.where(lane == kk, es[kk] / den, tw)
    ti_ref[...] = ti
    tw_ref[...] = tw


def _outproj(x2, og, oc, osel, ow, nar, ng, wo, fg, rwh, rwl, rb, tm):
    T, D = x2.shape
    row = lambda w: pl.BlockSpec((tm, w), lambda i: (i, 0))
    full = lambda a: pl.BlockSpec(a.shape, lambda i: (0,) * a.ndim)
    return pl.pallas_call(
        _outproj_kernel,
        grid=(T // tm,),
        in_specs=[row(D), row(GLA_WIDTH), row(NSA_WIDTH), row(NSA_WIDTH), row(NSA_WIDTH), row(LANE),
                  full(ng), full(wo), full(fg), full(rwh), full(rwl), full(rb)],
        out_specs=[row(D), row(D), row(LANE), row(LANE)],
        out_shape=[
            jax.ShapeDtypeStruct((T, D), F32),
            jax.ShapeDtypeStruct((T, D), F32),
            jax.ShapeDtypeStruct((T, LANE), I32),
            jax.ShapeDtypeStruct((T, LANE), F32),
        ],
        compiler_params=_params(("parallel",)),
    )(x2, og, oc, osel, ow, nar, ng, wo, fg, rwh, rwl, rb)


def _row_copy(src_hbm, dst_vmem, sem, src_row, dst_row):
    return pltpu.make_async_copy(src_hbm.at[pl.ds(src_row, 1), :], dst_vmem.at[pl.ds(dst_row, 1), :], sem)


def _gather_rows(src_hbm, idx_smem, dst_vmem, sem, n):
    def issue(r, c):
        _row_copy(src_hbm, dst_vmem, sem, idx_smem[r], r).start()
        return c

    lax.fori_loop(0, n, issue, 0)

    def drain(r, c):
        _row_copy(src_hbm, dst_vmem, sem, 0, r).wait()
        return c

    lax.fori_loop(0, n, drain, 0)


def _moe_kernel(te_ref, tv_ref, tok_hbm, hn_hbm, wgt_ref, w1g_ref, w1l_ref, b1g_ref, b1l_ref,
                w2_ref, b2_ref, y_ref, tok_smem, xf_ref, xb_ref, sem_idx, sem_row, *, tm, nf):
    i = pl.program_id(0)
    f = pl.program_id(1)
    live = tv_ref[i] > 0

    @pl.when(live & (f == 0))
    def _():
        cp = pltpu.make_async_copy(tok_hbm.at[i], tok_smem, sem_idx)
        cp.start()
        cp.wait()
        _gather_rows(hn_hbm, tok_smem, xf_ref, sem_row, tm)
        xb_ref[...] = xf_ref[...].astype(BF16)

    @pl.when(jnp.logical_not(live) & (f == 0))
    def _():
        y_ref[...] = jnp.zeros_like(y_ref)

    @pl.when(live)
    def _():
        xb = xb_ref[...]
        glu = jnp.minimum(_dot(xb, w1g_ref[0]) + b1g_ref[0], SWIGLU_LIMIT)
        lin = jnp.clip(_dot(xb, w1l_ref[0]) + b1l_ref[0], -SWIGLU_LIMIT, SWIGLU_LIMIT)
        act = (glu * _sigmoid(SWIGLU_ALPHA * glu) * (lin + 1.0)).astype(BF16)
        part = _dot(act, w2_ref[0])

        @pl.when(f == 0)
        def _():
            y_ref[...] = part

        @pl.when(f > 0)
        def _():
            y_ref[...] += part

        @pl.when(f == nf - 1)
        def _():
            y_ref[...] = (y_ref[...] + b2_ref[0]) * wgt_ref[0]


def _moe(tile_exp, tile_valid, tok, hn, wgt, w1, b1, w2, b2, tm, fc):
    n_tiles = tok.shape[0]
    D = hn.shape[1]
    dff = w2.shape[1]
    nf = dff // fc
    grid_spec = pltpu.PrefetchScalarGridSpec(
        num_scalar_prefetch=2,
        grid=(n_tiles, nf),
        in_specs=[
            pl.BlockSpec(memory_space=pl.ANY),
            pl.BlockSpec(memory_space=pl.ANY),
            pl.BlockSpec((1, tm, 1), lambda i, f, te, tv: (i, 0, 0)),
            pl.BlockSpec((1, D, fc), lambda i, f, te, tv: (te[i], 0, f)),
            pl.BlockSpec((1, D, fc), lambda i, f, te, tv: (te[i], 0, nf + f)),
            pl.BlockSpec((1, 1, fc), lambda i, f, te, tv: (te[i], 0, f)),
            pl.BlockSpec((1, 1, fc), lambda i, f, te, tv: (te[i], 0, nf + f)),
            pl.BlockSpec((1, fc, D), lambda i, f, te, tv: (te[i], f, 0)),
            pl.BlockSpec((1, 1, D), lambda i, f, te, tv: (te[i], 0, 0)),
        ],
        out_specs=pl.BlockSpec((tm, D), lambda i, f, te, tv: (i, 0)),
        scratch_shapes=[
            pltpu.SMEM((tm,), I32),
            pltpu.VMEM((tm, D), F32),
            pltpu.VMEM((tm, D), BF16),
            pltpu.SemaphoreType.DMA(()),
            pltpu.SemaphoreType.DMA(()),
        ],
    )
    return pl.pallas_call(
        functools.partial(_moe_kernel, tm=tm, nf=nf),
        grid_spec=grid_spec,
        out_shape=jax.ShapeDtypeStruct((n_tiles * tm, D), F32),
        compiler_params=_params(("arbitrary", "arbitrary")),
    )(tile_exp, tile_valid, tok, hn, wgt, w1, w1, b1, b1, w2, b2)


def _combine_kernel(pos_hbm, ys_hbm, h1_ref, g_ref, o_ref, pos_smem, buf_ref, sem_idx, sem_row, *, tm):
    i = pl.program_id(0)
    cp = pltpu.make_async_copy(pos_hbm.at[i], pos_smem, sem_idx)
    cp.start()
    cp.wait()
    _gather_rows(ys_hbm, pos_smem, buf_ref, sem_row, TOP_K * tm)
    h = h1_ref[...]
    for kk in range(TOP_K):
        h = h + buf_ref[kk * tm:(kk + 1) * tm, :]
    o_ref[...] = _rms(h, g_ref[...])


def _combine(pos, ys, h1, g, tm):
    T, D = h1.shape
    return pl.pallas_call(
        functools.partial(_combine_kernel, tm=tm),
        grid=(T // tm,),
        in_specs=[
            pl.BlockSpec(memory_space=pl.ANY),
            pl.BlockSpec(memory_space=pl.ANY),
            pl.BlockSpec((tm, D), lambda i: (i, 0)),
            pl.BlockSpec((1, D), lambda i: (0, 0)),
        ],
        out_specs=pl.BlockSpec((tm, D), lambda i: (i, 0)),
        out_shape=jax.ShapeDtypeStruct((T, D), F32),
        scratch_shapes=[
            pltpu.SMEM((TOP_K * tm,), I32),
            pltpu.VMEM((TOP_K * tm, D), F32),
            pltpu.SemaphoreType.DMA(()),
            pltpu.SemaphoreType.DMA(()),
        ],
        compiler_params=_params(("arbitrary",)),
    )(pos, ys, h1, g)


def _t5_bucket(dist):
    n = jnp.maximum(dist, 0)
    exact = N_BUCKETS // 2
    scaled = jnp.log(jnp.maximum(n, 1).astype(F32) / exact) / math.log(MAX_DISTANCE / exact)
    large = exact + (scaled * (N_BUCKETS - exact)).astype(I32)
    return jnp.where(n < exact, n, jnp.minimum(large, N_BUCKETS - 1)).astype(I32)


def _bias_tables(rel_bias, S):
    shifted = (rel_bias - rel_bias[N_BUCKETS - 1][None, :]).astype(F32)

    def lookup(dist):
        bucket = _t5_bucket(dist)
        out = jnp.zeros((NSA_HEADS,) + dist.shape, F32)
        for kb in range(N_BUCKETS - 1):
            out = jnp.where((bucket == kb)[None], shifted[kb].reshape((NSA_HEADS,) + (1,) * dist.ndim), out)
        return jnp.where((dist >= 0)[None], out, NEG)

    t = SEL_T
    r = jnp.arange(t)[:, None]
    c = jnp.arange(t)[None, :]
    dtiles = jnp.stack([lookup(r - c), lookup(t + r - c)], axis=1)
    w4 = jnp.where(r < c, 0.0, NEG).astype(F32)
    ncp = S // CMP_STRIDE
    dist_c = jnp.arange(S)[:, None] - (jnp.arange(ncp) * CMP_STRIDE + CMP_LEN - 1)[None, :]
    return dtiles, w4, lookup(dist_c)


def _overlap_t(S):
    ncp = S // CMP_STRIDE
    ns = S // SEL_BLOCK
    cs = jnp.arange(ncp)[None, :] * CMP_STRIDE
    ss = jnp.arange(ns)[:, None] * SEL_BLOCK
    ov = (cs < ss + SEL_BLOCK) & (cs + CMP_LEN > ss) & (jnp.arange(ncp)[None, :] < ncp - 1)
    return ov.astype(BF16)


def _route(top_idx, top_w, n_exp, tm):
    T = top_idx.shape[0]
    A = T * TOP_K
    e_flat = top_idx.reshape(A)
    w_flat = top_w.reshape(A)
    e_sorted, order = lax.sort_key_val(e_flat, jnp.arange(A, dtype=I32), is_stable=True)
    counts = jnp.sum((e_flat[:, None] == jnp.arange(n_exp, dtype=I32)[None, :]).astype(I32), axis=0)
    padded = (counts + tm - 1) // tm * tm
    start = jnp.cumsum(counts) - counts
    pend = jnp.cumsum(padded)
    pstart = pend - padded
    pos_sorted = pstart[e_sorted] + (jnp.arange(A, dtype=I32) - start[e_sorted])
    _, pos = lax.sort_key_val(order, pos_sorted)
    n_tiles = A // tm + n_exp
    tile_row0 = jnp.arange(n_tiles, dtype=I32) * tm
    tile_exp = jnp.minimum(jnp.searchsorted(pend, tile_row0, side='right'), n_exp - 1).astype(I32)
    tile_valid = jnp.clip(pstart[tile_exp] + counts[tile_exp] - tile_row0, 0, tm).astype(I32)
    off = tile_row0[:, None] - pstart[tile_exp][:, None] + jnp.arange(tm, dtype=I32)[None, :]
    live = jnp.arange(tm, dtype=I32)[None, :] < tile_valid[:, None]
    src = order[jnp.clip(start[tile_exp][:, None] + off, 0, A - 1)]
    buf_tok = jnp.where(live, src // TOP_K, 0)
    buf_w = jnp.where(live, w_flat[src], 0.0)
    return buf_tok, buf_w[:, :, None], pos.reshape(T, TOP_K), tile_exp, tile_valid


def kernel(x, norm_mix_g, w_in, gla_wa2, gla_ba, gla_norm_g, cmp_pe_k, cmp_wk1, cmp_wk2, cmp_pe_v,
           cmp_wv1, cmp_wv2, nsa_norm_g, rel_bias, w_out, norm_ffn_g, router_w, router_b, exp_w1,
           exp_b1, exp_w2, exp_b2, norm_final_g):
    B, S, D = x.shape
    T = B * S
    G = NSA_KV
    n_exp = router_w.shape[-1]
    assert w_in.shape[0] == 1 and S % 1024 == 0 and S // SEL_BLOCK <= LANE and n_exp <= LANE
    x2 = x.reshape(T, D)

    w = w_in[0]
    a0 = COL_NQ
    n0 = a0 + GLA_RANK
    g0 = n0 + (N_MAIN - COL_NQ)
    w_main = jnp.concatenate([
        w[:, :COL_GK] * (GLA_DK ** -0.5), w[:, COL_GK:a0],
        w[:, n0:n0 + NSA_WIDTH] * (NSA_DH ** -0.5), w[:, n0 + NSA_WIDTH:g0]], axis=1).astype(BF16)
    w_nar = jnp.concatenate([w[:, a0:n0], w[:, g0:], jnp.zeros((D, LANE - GLA_RANK - 3 * NSA_HEADS), F32)],
                            axis=1).astype(BF16)
    wa2p = jnp.concatenate([gla_wa2[0], jnp.zeros((LANE - GLA_RANK, GLA_HEADS * GLA_DK), F32)], axis=0).astype(BF16)

    tm1 = min(1024, T)
    proj, log_a, nar = _inproj(x2, norm_mix_g, w_main, w_nar, wa2p, gla_ba, tm1, 512)

    o_gla = _gla(proj, log_a, gla_norm_g, B, S, min(512, S))

    ncp = S // CMP_STRIDE
    half = CMP_STRIDE * NSA_DH

    def stride_groups(col):
        t4 = proj[:, col:col + KV_W].reshape(B, ncp, CMP_STRIDE, G, NSA_DH)
        return t4.transpose(0, 3, 1, 2, 4).reshape(B * G, ncp, half)

    hb = jnp.stack([stride_groups(COL_KC), stride_groups(COL_VC)], axis=0)
    pe = jnp.stack([cmp_pe_k[0].reshape(2, half), cmp_pe_v[0].reshape(2, half)], axis=0)
    w1c = jnp.stack([cmp_wk1[0], cmp_wv1[0]], axis=0).astype(BF16)
    w2c = jnp.stack([cmp_wk2[0], cmp_wv2[0]], axis=0).astype(BF16)
    cmp_kv = _compress(hb, pe, w1c, w2c)

    dtiles, w4, bias_c = _bias_tables(rel_bias, S)
    o_cmp, sel = _cmpattn(proj, cmp_kv, bias_c, _overlap_t(S), B, S)
    o_sel, o_win = _selwin(proj, sel, dtiles, w4, B, S)

    rw = jnp.concatenate([router_w[0], jnp.zeros((D, LANE - n_exp), F32)], axis=1)
    rwh = rw.astype(BF16)
    rwl = (rw - rwh.astype(F32)).astype(BF16)
    rb = jnp.concatenate([router_b[0], jnp.full((LANE - n_exp,), NEG, F32)])[None, :]
    h1, hn, top_i, top_w = _outproj(x2, o_gla, o_cmp, o_sel, o_win, nar, nsa_norm_g, w_out[0].astype(BF16),
                                    norm_ffn_g, rwh, rwl, rb, min(256, T))

    tm6 = min(1024, T)
    tok, wgt, pos, tile_exp, tile_valid = _route(top_i[:, :TOP_K], top_w[:, :TOP_K], n_exp, tm6)
    ys = _moe(tile_exp, tile_valid, tok, hn, wgt, exp_w1[0].astype(BF16), exp_b1[0][:, None, :],
              exp_w2[0].astype(BF16), exp_b2[0][:, None, :], tm6, 512)
    tm7 = min(256, T)
    pos_t = pos.reshape(T // tm7, tm7, TOP_K).transpose(0, 2, 1).reshape(T // tm7, TOP_K * tm7)
    out = _combine(pos_t, ys, h1, norm_final_g[None, :], tm7)
    return out.reshape(B, S, D)
```

```python
import functools
import math

import numpy as np
import jax
import jax.numpy as jnp
from jax import lax
from jax.experimental import pallas as pl
from jax.experimental.pallas import tpu as pltpu

F32 = jnp.float32
BF16 = jnp.bfloat16
I32 = jnp.int32
U32 = jnp.uint32

GLA_HEADS = 4
GLA_DK = 128
GLA_DV = 256
GLA_RANK = 16
GLA_TAU = 16.0
NSA_HEADS = 8
NSA_KV = 2
NSA_REP = NSA_HEADS // NSA_KV
NSA_DH = 128
CMP_LEN = 32
CMP_STRIDE = 16
SEL_BLOCK = 64
SEL_TOPN = 16
WINDOW = 512
N_BUCKETS = 32
MAX_DISTANCE = 128
TOP_K = 4
SWIGLU_ALPHA = 1.702
SWIGLU_LIMIT = 7.0
RMS_EPS = 1e-5
NEG = -1e30
FORCE_SCORE = 1e4

GLA_WIDTH = GLA_HEADS * GLA_DV
NSA_WIDTH = NSA_HEADS * NSA_DH
KV_W = NSA_KV * NSA_DH

COL_GQ = 0
COL_GK = COL_GQ + GLA_HEADS * GLA_DK
COL_GV = COL_GK + GLA_HEADS * GLA_DK
COL_GG = COL_GV + GLA_WIDTH
COL_NQ = COL_GG + GLA_WIDTH
COL_KC = COL_NQ + NSA_WIDTH
COL_VC = COL_KC + KV_W
COL_KS = COL_VC + KV_W
COL_VS = COL_KS + KV_W
COL_KW = COL_VS + KV_W
COL_VW = COL_KW + KV_W
N_MAIN = COL_VW + KV_W
LANE = 128
ATT_T = 128
SEL_T = 256
GLA_C = 64
GLA_SUB = 16
MOE_SUB = 512
GATE_COL0 = GLA_RANK

VMEM_LIMIT = 56 * 1024 * 1024

NT = (((1,), (1,)), ((), ()))
TN = (((0,), (0,)), ((), ()))


def _dot(a, b):
    return jnp.dot(a, b, preferred_element_type=F32)


def _dg(a, b, dims):
    return lax.dot_general(a, b, dims, preferred_element_type=F32)


def _sigmoid(x):
    return 1.0 / (1.0 + jnp.exp(-x))


def _log_sigmoid(z):
    return -(jnp.maximum(-z, 0.0) + jnp.log(1.0 + jnp.exp(-jnp.abs(z))))


def _rms(x, g):
    return x * lax.rsqrt(jnp.mean(x * x, axis=-1, keepdims=True) + RMS_EPS) * g


def _split3(x):
    hi = x.astype(BF16)
    r1 = x - hi.astype(F32)
    mid = r1.astype(BF16)
    lo = (r1 - mid.astype(F32)).astype(BF16)
    return hi, mid, lo


def _pack_pairs(lo, hi):
    lo_bits = lax.bitcast_convert_type(lo.astype(BF16).astype(F32), U32) >> 16
    hi_bits = lax.bitcast_convert_type(hi.astype(BF16).astype(F32), U32) & jnp.uint32(0xFFFF0000)
    return lo_bits | hi_bits


def _unpack_pairs(w):
    lo = lax.bitcast_convert_type(w << 16, F32)
    hi = lax.bitcast_convert_type(w & jnp.uint32(0xFFFF0000), F32)
    return lo, hi


def _params(sem, vmem=VMEM_LIMIT):
    return pltpu.CompilerParams(dimension_semantics=sem, vmem_limit_bytes=vmem)


def _inproj_kernel(x_ref, g_ref, w_ref, wn_ref, wa2_ref, ba_ref, proj_ref, la_ref, nar_ref, xn_ref):
    @pl.when(pl.program_id(1) == 0)
    def _():
        xn = _rms(x_ref[...], g_ref[...]).astype(BF16)
        xn_ref[...] = xn
        nar = _dot(xn, wn_ref[...])
        nar_ref[...] = nar
        z = _dot(nar.astype(BF16), wa2_ref[...]) + ba_ref[...]
        la_ref[...] = _log_sigmoid(z) * (1.0 / GLA_TAU)

    proj_ref[...] = _dot(xn_ref[...], w_ref[...]).astype(BF16)


def _inproj(x2, g, w_main, w_nar, wa2p, ba, tm, tn):
    T, D = x2.shape
    nj = N_MAIN // tn
    return pl.pallas_call(
        _inproj_kernel,
        grid=(T // tm, nj),
        in_specs=[
            pl.BlockSpec((tm, D), lambda i, j: (i, 0)),
            pl.BlockSpec((1, D), lambda i, j: (0, 0)),
            pl.BlockSpec((D, tn), lambda i, j: (0, j)),
            pl.BlockSpec((D, LANE), lambda i, j: (0, 0)),
            pl.BlockSpec((LANE, GLA_HEADS * GLA_DK), lambda i, j: (0, 0)),
            pl.BlockSpec((1, GLA_HEADS * GLA_DK), lambda i, j: (0, 0)),
        ],
        out_specs=[
            pl.BlockSpec((tm, tn), lambda i, j: (i, j)),
            pl.BlockSpec((tm, GLA_HEADS * GLA_DK), lambda i, j: (i, 0)),
            pl.BlockSpec((tm, LANE), lambda i, j: (i, 0)),
        ],
        out_shape=[
            jax.ShapeDtypeStruct((T, N_MAIN), BF16),
            jax.ShapeDtypeStruct((T, GLA_HEADS * GLA_DK), F32),
            jax.ShapeDtypeStruct((T, LANE), F32),
        ],
        scratch_shapes=[pltpu.VMEM((tm, D), BF16)],
        compiler_params=_params(("parallel", "arbitrary")),
    )(x2, g, w_main, w_nar, wa2p, ba)


def _gla_kernel(q_ref, k_ref, v_ref, g_ref, la_ref, ng_ref, o_ref, st_ref, *, cb):
    @pl.when(pl.program_id(2) == 0)
    def _():
        st_ref[...] = jnp.zeros_like(st_ref)

    C, SUB = GLA_C, GLA_SUB
    ri = lax.broadcasted_iota(I32, (C, C), 0)
    ci = lax.broadcasted_iota(I32, (C, C), 1)
    tri = jnp.where(ri >= ci, 1.0, 0.0).astype(BF16)
    for n in range(cb // C):
        r0 = n * C
        hi, mid, lo = _split3(la_ref[r0:r0 + C, :])
        b = _dot(tri, hi) + _dot(tri, mid) + _dot(tri, lo)
        q = q_ref[r0:r0 + C, :].astype(F32)
        k = k_ref[r0:r0 + C, :].astype(F32)
        v = v_ref[r0:r0 + C, :]
        st = st_ref[...]
        o = _dg((q * jnp.exp(b)).astype(BF16), st.astype(BF16), NT)
        parts = []
        for s in range(C // SUB):
            lo_, hi_ = s * SUB, (s + 1) * SUB
            ref = b[lo_ - 1:lo_, :] if s > 0 else jnp.zeros((1, GLA_DK), F32)
            qs = (q[lo_:hi_] * jnp.exp(b[lo_:hi_] - ref)).astype(BF16)
            ks = (k[:hi_] * jnp.exp(ref - b[:hi_])).astype(BF16)
            sc = _dg(qs, ks, NT)
            row = lax.broadcasted_iota(I32, (SUB, hi_), 0) + lo_
            col = lax.broadcasted_iota(I32, (SUB, hi_), 1)
            sc = jnp.where(col <= row, sc, 0.0)
            parts.append(_dot(sc.astype(BF16), v[:hi_]))
        o = o + jnp.concatenate(parts, axis=0)
        bl = b[C - 1:C, :]
        kdec = (k * jnp.exp(bl - b)).astype(BF16)
        st_ref[...] = st * jnp.exp(bl) + _dg(v, kdec, TN)
        gg = g_ref[r0:r0 + C, :].astype(F32)
        o_ref[r0:r0 + C, :] = (_rms(o, ng_ref[...]) * (gg * _sigmoid(gg))).astype(BF16)


def _gla(proj, la, ng, B, S, cb):
    T = B * S
    ncb = S // cb
    row = lambda b, h, c: b * ncb + c
    return pl.pallas_call(
        functools.partial(_gla_kernel, cb=cb),
        grid=(B, GLA_HEADS, ncb),
        in_specs=[
            pl.BlockSpec((cb, GLA_DK), lambda b, h, c: (row(b, h, c), COL_GQ // GLA_DK + h)),
            pl.BlockSpec((cb, GLA_DK), lambda b, h, c: (row(b, h, c), COL_GK // GLA_DK + h)),
            pl.BlockSpec((cb, GLA_DV), lambda b, h, c: (row(b, h, c), COL_GV // GLA_DV + h)),
            pl.BlockSpec((cb, GLA_DV), lambda b, h, c: (row(b, h, c), COL_GG // GLA_DV + h)),
            pl.BlockSpec((cb, GLA_DK), lambda b, h, c: (row(b, h, c), h)),
            pl.BlockSpec((1, GLA_DV), lambda b, h, c: (0, 0)),
        ],
        out_specs=pl.BlockSpec((cb, GLA_DV), lambda b, h, c: (row(b, h, c), h)),
        out_shape=jax.ShapeDtypeStruct((T, GLA_WIDTH), BF16),
        scratch_shapes=[pltpu.VMEM((GLA_DV, GLA_DK), F32)],
        compiler_params=_params(("parallel", "parallel", "arbitrary")),
    )(proj, proj, proj, proj, la, ng)


def _compress_kernel(hb_ref, pe_ref, w1_ref, w2_ref, o_ref, *, ncp):
    half = CMP_STRIDE * NSA_DH
    hb = hb_ref[0, 0].astype(F32)
    pe = pe_ref[0]
    top = (hb + pe[0:1]).astype(BF16)
    bot = (hb + pe[1:2]).astype(BF16)
    a = _dot(top, w1_ref[0, :half, :])
    bm = _dot(bot, w1_ref[0, half:, :])
    pre = a + pltpu.roll(bm, ncp - 1, 0)
    cdf = 0.5 * (1.0 + jnp.tanh(math.sqrt(2.0 / math.pi) * (pre + 0.044715 * (pre * pre * pre))))
    o_ref[0, 0] = _dot((pre * cdf).astype(BF16), w2_ref[0]).astype(BF16)


def _compress(hb, pe, w1, w2):
    _, BG, ncp, half = hb.shape
    return pl.pallas_call(
        functools.partial(_compress_kernel, ncp=ncp),
        grid=(2, BG),
        in_specs=[
            pl.BlockSpec((1, 1, ncp, half), lambda t, n: (t, n, 0, 0)),
            pl.BlockSpec((1, 2, half), lambda t, n: (t, 0, 0)),
            pl.BlockSpec((1, 2 * half, NSA_DH), lambda t, n: (t, 0, 0)),
            pl.BlockSpec((1, NSA_DH, NSA_DH), lambda t, n: (t, 0, 0)),
        ],
        out_specs=pl.BlockSpec((1, 1, ncp, NSA_DH), lambda t, n: (t, n, 0, 0)),
        out_shape=jax.ShapeDtypeStruct((2, BG, ncp, NSA_DH), BF16),
        compiler_params=_params(("parallel", "parallel")),
    )(hb, pe, w1, w2)


def _cmpattn_kernel(q_ref, kc_ref, vc_ref, bias_ref, ovt_ref, o_ref, sel_ref, *, ns, ncp):
    tq = ATT_T
    i = pl.program_id(1)
    kc = kc_ref[0, 0]
    vc = vc_ref[0, 0]
    pos = i * tq + lax.broadcasted_iota(I32, (tq, 1), 0)
    row_valid = jnp.where(pos >= CMP_LEN - 1, 1.0, 0.0)
    psum = jnp.zeros((tq, ncp), F32)
    outs = []
    for r in range(NSA_REP):
        q = q_ref[:, r * NSA_DH:(r + 1) * NSA_DH]
        s = _dg(q, kc, NT) + bias_ref[r]
        e = jnp.exp(s - jnp.max(s, axis=-1, keepdims=True))
        p = e / jnp.sum(e, axis=-1, keepdims=True) * row_valid
        psum = psum + p
        outs.append(_dot(p.astype(BF16), vc))
    o_ref[...] = jnp.concatenate(outs, axis=1).astype(BF16)

    ph = psum.astype(BF16)
    pl_ = (psum - ph.astype(F32)).astype(BF16)
    imp = _dg(ovt_ref[...], ph, NT) + _dg(ovt_ref[...], pl_, NT)
    n_i = lax.broadcasted_iota(I32, (ns, tq), 0)
    cur = (i * tq + lax.broadcasted_iota(I32, (ns, tq), 1)) // SEL_BLOCK
    forced = (n_i == 0) | (n_i == cur) | (n_i == cur - 1)
    score = jnp.where(forced, FORCE_SCORE, jnp.where(n_i <= cur, imp, -1.0))
    cnt = jnp.zeros((ns, tq), F32)
    for m in range(ns):
        sm = score[m:m + 1, :]
        ge = jnp.where(sm >= score, 1.0, 0.0)
        gt = jnp.where(sm > score, 1.0, 0.0)
        cnt = cnt + jnp.where(n_i > m, ge, gt)
    sel_t = jnp.where(cnt < min(SEL_TOPN, ns), 1.0, 0.0)
    if ns < LANE:
        sel_t = jnp.concatenate([sel_t, jnp.zeros((LANE - ns, tq), F32)], axis=0)
    sel_ref[0] = sel_t.T.astype(BF16)


def _cmpattn(proj, cmp_kv, bias_c, ovt, B, S):
    T = B * S
    tq = ATT_T
    nq = S // tq
    ns = S // SEL_BLOCK
    ncp = S // CMP_STRIDE
    G = NSA_KV
    qcol = COL_NQ // (NSA_REP * NSA_DH)
    return pl.pallas_call(
        functools.partial(_cmpattn_kernel, ns=ns, ncp=ncp),
        grid=(G, nq, B),
        in_specs=[
            pl.BlockSpec((tq, NSA_REP * NSA_DH), lambda g, i, b: (b * nq + i, qcol + g)),
            pl.BlockSpec((1, 1, ncp, NSA_DH), lambda g, i, b: (0, b * G + g, 0, 0)),
            pl.BlockSpec((1, 1, ncp, NSA_DH), lambda g, i, b: (1, b * G + g, 0, 0)),
            pl.BlockSpec((NSA_REP, tq, ncp), lambda g, i, b: (g, i, 0)),
            pl.BlockSpec((ns, ncp), lambda g, i, b: (0, 0)),
        ],
        out_specs=[
            pl.BlockSpec((tq, NSA_REP * NSA_DH), lambda g, i, b: (b * nq + i, g)),
            pl.BlockSpec((1, tq, LANE), lambda g, i, b: (b * G + g, i, 0)),
        ],
        out_shape=[
            jax.ShapeDtypeStruct((T, NSA_WIDTH), BF16),
            jax.ShapeDtypeStruct((B * G, S, LANE), BF16),
        ],
        compiler_params=_params(("parallel", "parallel", "parallel")),
    )(proj, cmp_kv, cmp_kv, bias_c, ovt)


def _flash_update(s, v, m_ref, al_ref):
    reps = s.shape[1] // LANE
    m_old = m_ref[...]
    m_new = jnp.maximum(m_old, jnp.max(s, axis=-1, keepdims=True))
    alpha = jnp.exp(m_old - m_new)
    p = jnp.exp(s - jnp.concatenate([m_new] * reps, axis=1))
    v_ext = jnp.concatenate([v, jnp.ones(v.shape, BF16)], axis=1)
    al_ref[...] = jnp.concatenate([alpha, alpha], axis=1) * al_ref[...] + _dot(p.astype(BF16), v_ext)
    m_ref[...] = m_new


def _selwin_kernel(q_ref, ks_ref, vs_ref, kw_ref, vw_ref, sel_ref, d_ref, w4_ref,
                   osel_ref, owin_ref, m_ref, al_ref):
    t = SEL_T
    R = NSA_REP
    i = pl.program_id(2)
    q = jnp.concatenate([q_ref[:, r * NSA_DH:(r + 1) * NSA_DH] for r in range(R)], axis=0)
    sel = sel_ref[0]
    n_i = lax.broadcasted_iota(I32, (LANE, t), 0)
    c_blk = lax.broadcasted_iota(I32, (LANE, t), 1) // SEL_BLOCK
    blk_per_tile = t // SEL_BLOCK

    def init():
        m_ref[...] = jnp.full(m_ref.shape, NEG, F32)
        al_ref[...] = jnp.zeros(al_ref.shape, F32)

    def tile(ref, j):
        return ref[pl.ds(pl.multiple_of(j * t, t), t), :]

    def logits(k_ref, j, add):
        s = _dg(q, tile(k_ref, j), NT).reshape(R, t, t)
        return (s + add).reshape(R * t, t)

    def sel_mask(j):
        expand = jnp.where(n_i == blk_per_tile * j + c_blk, 1.0, 0.0).astype(BF16)
        return (_dot(sel, expand) - 1.0) * (-NEG)

    def finish(o_ref):
        al = al_ref[...]
        o = al[:, :NSA_DH] / al[:, NSA_DH:]
        o_ref[...] = jnp.concatenate([o[r * t:(r + 1) * t] for r in range(R)], axis=1).astype(BF16)

    init()

    def far(j, c):
        _flash_update(logits(ks_ref, j, sel_mask(j)[None]), tile(vs_ref, j), m_ref, al_ref)
        return c

    lax.fori_loop(0, jnp.maximum(i - 1, 0), far, 0)
    jn = jnp.maximum(i - 1, 0)
    off = jnp.where(i == 0, NEG, 0.0)
    _flash_update(logits(ks_ref, jn, (sel_mask(jn) + off)[None] + d_ref[:, 1]), tile(vs_ref, jn),
                  m_ref, al_ref)
    _flash_update(logits(ks_ref, i, sel_mask(i)[None] + d_ref[:, 0]), tile(vs_ref, i),
                  m_ref, al_ref)
    finish(osel_ref)

    init()
    pre = WINDOW // t
    for o in range(pre, -1, -1):
        jw = i - o
        jc = jnp.maximum(jw, 0)
        off = jnp.where(jw < 0, NEG, 0.0)
        if o == 0:
            add = d_ref[:, 0] + off
        elif o == 1:
            add = d_ref[:, 1] + off
        elif o == pre:
            add = (w4_ref[...] + off)[None]
        else:
            add = off
        _flash_update(logits(kw_ref, jc, add), tile(vw_ref, jc), m_ref, al_ref)
    finish(owin_ref)


def _selwin(proj, sel, dtiles, w4, B, S):
    T = B * S
    t = SEL_T
    nq = S // t
    G = NSA_KV
    qcol = COL_NQ // (NSA_REP * NSA_DH)
    kv = lambda col: pl.BlockSpec((S, NSA_DH), lambda b, g, i, col=col: (b, col // NSA_DH + g))
    return pl.pallas_call(
        _selwin_kernel,
        grid=(B, G, nq),
        in_specs=[
            pl.BlockSpec((t, NSA_REP * NSA_DH), lambda b, g, i: (b * nq + i, qcol + g)),
            kv(COL_KS), kv(COL_VS), kv(COL_KW), kv(COL_VW),
            pl.BlockSpec((1, t, LANE), lambda b, g, i: (b * G + g, i, 0)),
            pl.BlockSpec((NSA_REP, 2, t, t), lambda b, g, i: (g, 0, 0, 0)),
            pl.BlockSpec((t, t), lambda b, g, i: (0, 0)),
        ],
        out_specs=[
            pl.BlockSpec((t, NSA_REP * NSA_DH), lambda b, g, i: (b * nq + i, g)),
            pl.BlockSpec((t, NSA_REP * NSA_DH), lambda b, g, i: (b * nq + i, g)),
        ],
        out_shape=[
            jax.ShapeDtypeStruct((T, NSA_WIDTH), BF16),
            jax.ShapeDtypeStruct((T, NSA_WIDTH), BF16),
        ],
        scratch_shapes=[
            pltpu.VMEM((NSA_REP * t, LANE), F32),
            pltpu.VMEM((NSA_REP * t, 2 * NSA_DH), F32),
        ],
        compiler_params=_params(("parallel", "parallel", "parallel")),
    )(proj, proj, proj, proj, proj, sel, dtiles, w4)


def _outproj_kernel(x_ref, og_ref, oc_ref, os_ref, ow_ref, nar_ref, ng_ref, wo_ref, fg_ref,
                    rwh_ref, rwl_ref, rb_ref, h1_ref, hn_ref, ti_ref, tw_ref):
    gates = _sigmoid(nar_ref[...])
    parts = []
    for h in range(NSA_HEADS):
        cs = slice(h * NSA_DH, (h + 1) * NSA_DH)
        g0 = gates[:, GATE_COL0 + h:GATE_COL0 + h + 1]
        g1 = gates[:, GATE_COL0 + NSA_HEADS + h:GATE_COL0 + NSA_HEADS + h + 1]
        g2 = gates[:, GATE_COL0 + 2 * NSA_HEADS + h:GATE_COL0 + 2 * NSA_HEADS + h + 1]
        parts.append(g0 * oc_ref[:, cs].astype(F32) + g1 * os_ref[:, cs].astype(F32)
                     + g2 * ow_ref[:, cs].astype(F32))
    on = _rms(jnp.concatenate(parts, axis=1), ng_ref[...]).astype(BF16)
    h1 = x_ref[...] + _dot(og_ref[...], wo_ref[:GLA_WIDTH, :]) + _dot(on, wo_ref[GLA_WIDTH:, :])
    h1_ref[...] = h1
    hn = _rms(h1, fg_ref[...])
    half = hn.shape[1] // 2
    hn_ref[...] = _pack_pairs(hn[:, :half], hn[:, half:])

    hh = hn.astype(BF16)
    hl = (hn - hh.astype(F32)).astype(BF16)
    lg = _dot(hh, rwh_ref[...]) + _dot(hl, rwh_ref[...]) + _dot(hh, rwl_ref[...]) + rb_ref[...]
    lane = lax.broadcasted_iota(I32, lg.shape, 1)
    vals, idxs = [], []
    for _ in range(TOP_K):
        mx = jnp.max(lg, axis=-1, keepdims=True)
        ix = jnp.min(jnp.where(lg == mx, lane, LANE), axis=-1, keepdims=True)
        vals.append(mx)
        idxs.append(ix)
        lg = jnp.where(lane == ix, -3e38, lg)
    es = [jnp.exp(v - vals[0]) for v in vals]
    den = es[0] + es[1] + es[2] + es[3]
    ti = jnp.zeros(lg.shape, I32)
    tw = jnp.zeros(lg.shape, F32)
    for kk in range(TOP_K):
        ti = jnp.where(lane == kk, idxs[kk], ti)
        tw = jnp.where(lane == kk, es[kk] / den, tw)
    ti_ref[...] = ti
    tw_ref[...] = tw


def _outproj(x2, og, oc, osel, ow, nar, ng, wo, fg, rwh, rwl, rb, tm):
    T, D = x2.shape
    row = lambda w: pl.BlockSpec((tm, w), lambda i: (i, 0))
    full = lambda a: pl.BlockSpec(a.shape, lambda i: (0,) * a.ndim)
    return pl.pallas_call(
        _outproj_kernel,
        grid=(T // tm,),
        in_specs=[row(D), row(GLA_WIDTH), row(NSA_WIDTH), row(NSA_WIDTH), row(NSA_WIDTH), row(LANE),
                  full(ng), full(wo), full(fg), full(rwh), full(rwl), full(rb)],
        out_specs=[row(D), row(D // 2), row(LANE), row(LANE)],
        out_shape=[
            jax.ShapeDtypeStruct((T, D), F32),
            jax.ShapeDtypeStruct((T, D // 2), U32),
            jax.ShapeDtypeStruct((T, LANE), I32),
            jax.ShapeDtypeStruct((T, LANE), F32),
        ],
        compiler_params=_params(("parallel",)),
    )(x2, og, oc, osel, ow, nar, ng, wo, fg, rwh, rwl, rb)


def _moe_kernel(te_ref, tv_ref, tok_hbm, dst_hbm, hn_hbm, wgt_ref, w1g_ref, w1l_ref, b1g_ref, b1l_ref,
                w2_ref, b2_ref, out_hbm, tok_smem, dst_smem, xbuf, xb_ref, acc_ref, ybuf,
                sem_tok, sem_dst, sem_in, sem_out, *, tm, nf, n_tiles):
    i = pl.program_id(0)
    f = pl.program_id(1)
    slot = i % 2
    valid = tv_ref[i]
    sub = MOE_SUB
    nsub = tm // sub
    half = xbuf.shape[2]

    def tok_copy(t, sl):
        return pltpu.make_async_copy(tok_hbm.at[t], tok_smem.at[sl], sem_tok.at[sl])

    def dst_copy():
        return pltpu.make_async_copy(dst_hbm.at[i], dst_smem, sem_dst)

    def issue_gather(t, sl):
        for s in range(nsub):
            @pl.when(s * sub < tv_ref[t])
            def _():
                def body(r, c):
                    pltpu.make_async_copy(hn_hbm.at[pl.ds(tok_smem[sl, r], 1), :],
                                          xbuf.at[sl, pl.ds(r, 1), :], sem_in.at[sl, s]).start()
                    return c

                lax.fori_loop(s * sub, (s + 1) * sub, body, 0, unroll=8)

    def wait_gather(sl, s):
        pltpu.make_async_copy(hn_hbm.at[pl.ds(0, sub), :], xbuf.at[sl, pl.ds(s * sub, sub), :],
                              sem_in.at[sl, s]).wait()

    def wait_scatter(count):
        for s in range(nsub):
            @pl.when(s * sub < count)
            def _():
                pltpu.make_async_copy(ybuf.at[pl.ds(s * sub, sub), :], out_hbm.at[pl.ds(0, sub), :],
                                      sem_out).wait()

    @pl.when(f == 0)
    def _():
        @pl.when(i == 0)
        def _():
            cp = tok_copy(0, 0)
            cp.start()
            cp.wait()
            issue_gather(0, 0)

        @pl.when(i + 1 < n_tiles)
        def _():
            tok_copy(i + 1, 1 - slot).start()

        dst_copy().start()
        for s in range(nsub):
            @pl.when(s * sub < valid)
            def _():
                wait_gather(slot, s)
                lo, hi = _unpack_pairs(xbuf[slot, s * sub:(s + 1) * sub, :])
                xb_ref[s * sub:(s + 1) * sub, :half] = lo.astype(BF16)
                xb_ref[s * sub:(s + 1) * sub, half:] = hi.astype(BF16)

        @pl.when(i + 1 < n_tiles)
        def _():
            tok_copy(i + 1, 1 - slot).wait()
            issue_gather(i + 1, 1 - slot)

    for s in range(nsub):
        @pl.when(s * sub < valid)
        def _():
            rows = slice(s * sub, (s + 1) * sub)
            xb = xb_ref[rows, :]
            glu = jnp.minimum(_dot(xb, w1g_ref[0]) + b1g_ref[0], SWIGLU_LIMIT)
            lin = jnp.clip(_dot(xb, w1l_ref[0]) + b1l_ref[0], -SWIGLU_LIMIT, SWIGLU_LIMIT)
            act = (glu * _sigmoid(SWIGLU_ALPHA * glu) * (lin + 1.0)).astype(BF16)
            part = _dot(act, w2_ref[0])

            @pl.when(f == 0)
            def _():
                acc_ref[rows, :] = part

            @pl.when(f > 0)
            def _():
                acc_ref[rows, :] += part

    @pl.when(f == nf - 1)
    def _():
        @pl.when(i > 0)
        def _():
            wait_scatter(tv_ref[jnp.maximum(i - 1, 0)])

        dst_copy().wait()
        for s in range(nsub):
            @pl.when(s * sub < valid)
            def _():
                rows = slice(s * sub, (s + 1) * sub)
                y = (acc_ref[rows, :] + b2_ref[0]) * wgt_ref[0, rows, :]
                ybuf[rows, :] = _pack_pairs(y[:, :half], y[:, half:])

                def body(r, c):
                    pltpu.make_async_copy(ybuf.at[pl.ds(r, 1), :], out_hbm.at[pl.ds(dst_smem[r], 1), :],
                                          sem_out).start()
                    return c

                lax.fori_loop(s * sub, (s + 1) * sub, body, 0, unroll=8)

        @pl.when(i == n_tiles - 1)
        def _():
            wait_scatter(valid)
            ybuf[...] = jnp.zeros_like(ybuf)
            dump = pltpu.make_async_copy(ybuf, out_hbm.at[pl.ds(out_hbm.shape[0] - tm, tm), :], sem_out)
            dump.start()
            dump.wait()


def _moe(tile_exp, tile_valid, tok, dst, hn, wgt, w1, b1, w2, b2, n_out_rows, tm, fc):
    n_tiles = tok.shape[0]
    half = hn.shape[1]
    D = 2 * half
    dff = w2.shape[1]
    nf = dff // fc
    grid_spec = pltpu.PrefetchScalarGridSpec(
        num_scalar_prefetch=2,
        grid=(n_tiles, nf),
        in_specs=[
            pl.BlockSpec(memory_space=pl.ANY),
            pl.BlockSpec(memory_space=pl.ANY),
            pl.BlockSpec(memory_space=pl.ANY),
            pl.BlockSpec((1, tm, 1), lambda i, f, te, tv: (i, 0, 0)),
            pl.BlockSpec((1, D, fc), lambda i, f, te, tv: (te[i], 0, f)),
            pl.BlockSpec((1, D, fc), lambda i, f, te, tv: (te[i], 0, nf + f)),
            pl.BlockSpec((1, 1, fc), lambda i, f, te, tv: (te[i], 0, f)),
            pl.BlockSpec((1, 1, fc), lambda i, f, te, tv: (te[i], 0, nf + f)),
            pl.BlockSpec((1, fc, D), lambda i, f, te, tv: (te[i], f, 0)),
            pl.BlockSpec((1, 1, D), lambda i, f, te, tv: (te[i], 0, 0)),
        ],
        out_specs=pl.BlockSpec(memory_space=pl.ANY),
        scratch_shapes=[
            pltpu.SMEM((2, tm), I32),
            pltpu.SMEM((tm,), I32),
            pltpu.VMEM((2, tm, half), U32),
            pltpu.VMEM((tm, D), BF16),
            pltpu.VMEM((tm, D), F32),
            pltpu.VMEM((tm, half), U32),
            pltpu.SemaphoreType.DMA((2,)),
            pltpu.SemaphoreType.DMA(()),
            pltpu.SemaphoreType.DMA((2, tm // MOE_SUB)),
            pltpu.SemaphoreType.DMA(()),
        ],
    )
    return pl.pallas_call(
        functools.partial(_moe_kernel, tm=tm, nf=nf, n_tiles=n_tiles),
        grid_spec=grid_spec,
        out_shape=jax.ShapeDtypeStruct((n_out_rows, half), U32),
        compiler_params=_params(("arbitrary", "arbitrary")),
    )(tile_exp, tile_valid, tok, dst, hn, wgt, w1, w1, b1, b1, w2, b2)


def _combine_kernel(h1_ref, y0_ref, y1_ref, y2_ref, y3_ref, g_ref, o_ref):
    h = h1_ref[...]
    for y_ref in (y0_ref, y1_ref, y2_ref, y3_ref):
        lo, hi = _unpack_pairs(y_ref[...])
        h = h + jnp.concatenate([lo, hi], axis=1)
    o_ref[...] = _rms(h, g_ref[...])


def _combine(ys, h1, g, tm):
    T, D = h1.shape
    nb = T // tm
    slab = lambda k: pl.BlockSpec((tm, D // 2), lambda i, k=k: (k * nb + i, 0))
    return pl.pallas_call(
        _combine_kernel,
        grid=(nb,),
        in_specs=[pl.BlockSpec((tm, D), lambda i: (i, 0))] + [slab(k) for k in range(TOP_K)]
        + [pl.BlockSpec((1, D), lambda i: (0, 0))],
        out_specs=pl.BlockSpec((tm, D), lambda i: (i, 0)),
        out_shape=jax.ShapeDtypeStruct((T, D), F32),
        compiler_params=_params(("parallel",)),
    )(h1, ys, ys, ys, ys, g)


def _t5_bucket(dist):
    n = jnp.maximum(dist, 0)
    exact = N_BUCKETS // 2
    scaled = jnp.log(jnp.maximum(n, 1).astype(F32) / exact) / math.log(MAX_DISTANCE / exact)
    large = exact + (scaled * (N_BUCKETS - exact)).astype(I32)
    return jnp.where(n < exact, n, jnp.minimum(large, N_BUCKETS - 1)).astype(I32)


def _bias_tables(rel_bias, S):
    shifted = (rel_bias - rel_bias[N_BUCKETS - 1][None, :]).astype(F32)

    def lookup(dist):
        bucket = _t5_bucket(dist)
        out = jnp.zeros((NSA_HEADS,) + dist.shape, F32)
        for kb in range(N_BUCKETS - 1):
            out = jnp.where((bucket == kb)[None], shifted[kb].reshape((NSA_HEADS,) + (1,) * dist.ndim), out)
        return jnp.where((dist >= 0)[None], out, NEG)

    t = SEL_T
    r = jnp.arange(t)[:, None]
    c = jnp.arange(t)[None, :]
    dtiles = jnp.stack([lookup(r - c), lookup(t + r - c)], axis=1)
    w4 = jnp.where(r < c, 0.0, NEG).astype(F32)
    ncp = S // CMP_STRIDE
    dist_c = jnp.arange(S)[:, None] - (jnp.arange(ncp) * CMP_STRIDE + CMP_LEN - 1)[None, :]
    return dtiles, w4, lookup(dist_c)


def _overlap_t(S):
    ncp = S // CMP_STRIDE
    ns = S // SEL_BLOCK
    cs = jnp.arange(ncp)[None, :] * CMP_STRIDE
    ss = jnp.arange(ns)[:, None] * SEL_BLOCK
    ov = (cs < ss + SEL_BLOCK) & (cs + CMP_LEN > ss) & (jnp.arange(ncp)[None, :] < ncp - 1)
    return ov.astype(BF16)


def _route(top_idx, top_w, n_exp, tm):
    T = top_idx.shape[0]
    A = T * TOP_K
    e_flat = top_idx.reshape(A)
    w_flat = top_w.reshape(A)
    e_sorted, order = lax.sort_key_val(e_flat, jnp.arange(A, dtype=I32), is_stable=True)
    counts = jnp.sum((e_flat[:, None] == jnp.arange(n_exp, dtype=I32)[None, :]).astype(I32), axis=0)
    padded = (counts + tm - 1) // tm * tm
    start = jnp.cumsum(counts) - counts
    pend = jnp.cumsum(padded)
    pstart = pend - padded
    n_tiles = A // tm + n_exp
    tile_row0 = jnp.arange(n_tiles, dtype=I32) * tm
    tile_exp = jnp.minimum(jnp.searchsorted(pend, tile_row0, side='right'), n_exp - 1).astype(I32)
    tile_valid = jnp.clip(pstart[tile_exp] + counts[tile_exp] - tile_row0, 0, tm).astype(I32)
    off = tile_row0[:, None] - pstart[tile_exp][:, None] + jnp.arange(tm, dtype=I32)[None, :]
    live = jnp.arange(tm, dtype=I32)[None, :] < tile_valid[:, None]
    src = order[jnp.clip(start[tile_exp][:, None] + off, 0, A - 1)]
    buf_tok = jnp.where(live, src // TOP_K, 0)
    buf_w = jnp.where(live, w_flat[src], 0.0)
    buf_dst = jnp.where(live, (src % TOP_K) * T + src // TOP_K, TOP_K * T + jnp.arange(tm, dtype=I32)[None, :])
    return buf_tok, buf_dst, buf_w[:, :, None], tile_exp, tile_valid


def kernel(x, norm_mix_g, w_in, gla_wa2, gla_ba, gla_norm_g, cmp_pe_k, cmp_wk1, cmp_wk2, cmp_pe_v,
           cmp_wv1, cmp_wv2, nsa_norm_g, rel_bias, w_out, norm_ffn_g, router_w, router_b, exp_w1,
           exp_b1, exp_w2, exp_b2, norm_final_g):
    B, S, D = x.shape
    T = B * S
    G = NSA_KV
    n_exp = router_w.shape[-1]
    assert w_in.shape[0] == 1 and S % 1024 == 0 and S // SEL_BLOCK <= LANE and n_exp <= LANE
    x2 = x.reshape(T, D)

    w = w_in[0]
    a0 = COL_NQ
    n0 = a0 + GLA_RANK
    g0 = n0 + (N_MAIN - COL_NQ)
    w_main = jnp.concatenate([
        w[:, :COL_GK] * (GLA_DK ** -0.5), w[:, COL_GK:a0],
        w[:, n0:n0 + NSA_WIDTH] * (NSA_DH ** -0.5), w[:, n0 + NSA_WIDTH:g0]], axis=1).astype(BF16)
    w_nar = jnp.concatenate([w[:, a0:n0], w[:, g0:], jnp.zeros((D, LANE - GLA_RANK - 3 * NSA_HEADS), F32)],
                            axis=1).astype(BF16)
    wa2p = jnp.concatenate([gla_wa2[0], jnp.zeros((LANE - GLA_RANK, GLA_HEADS * GLA_DK), F32)], axis=0).astype(BF16)

    tm1 = min(1024, T)
    proj, log_a, nar = _inproj(x2, norm_mix_g, w_main, w_nar, wa2p, gla_ba, tm1, 512)

    o_gla = _gla(proj, log_a, gla_norm_g, B, S, min(512, S))

    ncp = S // CMP_STRIDE
    half = CMP_STRIDE * NSA_DH

    def stride_groups(col):
        t4 = proj[:, col:col + KV_W].reshape(B, ncp, CMP_STRIDE, G, NSA_DH)
        return t4.transpose(0, 3, 1, 2, 4).reshape(B * G, ncp, half)

    hb = jnp.stack([stride_groups(COL_KC), stride_groups(COL_VC)], axis=0)
    pe = jnp.stack([cmp_pe_k[0].reshape(2, half), cmp_pe_v[0].reshape(2, half)], axis=0)
    w1c = jnp.stack([cmp_wk1[0], cmp_wv1[0]], axis=0).astype(BF16)
    w2c = jnp.stack([cmp_wk2[0], cmp_wv2[0]], axis=0).astype(BF16)
    cmp_kv = _compress(hb, pe, w1c, w2c)

    dtiles, w4, bias_c = _bias_tables(rel_bias, S)
    o_cmp, sel = _cmpattn(proj, cmp_kv, bias_c, _overlap_t(S), B, S)
    o_sel, o_win = _selwin(proj, sel, dtiles, w4, B, S)

    rw = jnp.concatenate([router_w[0], jnp.zeros((D, LANE - n_exp), F32)], axis=1)
    rwh = rw.astype(BF16)
    rwl = (rw - rwh.astype(F32)).astype(BF16)
    rb = jnp.concatenate([router_b[0], jnp.full((LANE - n_exp,), NEG, F32)])[None, :]
    h1, hn, top_i, top_w = _outproj(x2, o_gla, o_cmp, o_sel, o_win, nar, nsa_norm_g, w_out[0].astype(BF16),
                                    norm_ffn_g, rwh, rwl, rb, min(256, T))

    tm6 = min(1024, T)
    tok, dst, wgt, tile_exp, tile_valid = _route(top_i[:, :TOP_K], top_w[:, :TOP_K], n_exp, tm6)
    ys = _moe(tile_exp, tile_valid, tok, dst, hn, wgt, exp_w1[0].astype(BF16), exp_b1[0][:, None, :],
              exp_w2[0].astype(BF16), exp_b2[0][:, None, :], TOP_K * T + tm6, tm6, 512)
    out = _combine(ys, h1, norm_final_g[None, :], min(512, T))
    return out.reshape(B, S, D)
```

```python
import functools
import math

import numpy as np
import jax
import jax.numpy as jnp
from jax import lax
from jax.experimental import pallas as pl
from jax.experimental.pallas import tpu as pltpu

F32 = jnp.float32
BF16 = jnp.bfloat16
I32 = jnp.int32
U32 = jnp.uint32

GLA_HEADS = 4
GLA_DK = 128
GLA_DV = 256
GLA_RANK = 16
GLA_TAU = 16.0
NSA_HEADS = 8
NSA_KV = 2
NSA_REP = NSA_HEADS // NSA_KV
NSA_DH = 128
CMP_LEN = 32
CMP_STRIDE = 16
SEL_BLOCK = 64
SEL_TOPN = 16
WINDOW = 512
N_BUCKETS = 32
MAX_DISTANCE = 128
TOP_K = 4
SWIGLU_ALPHA = 1.702
SWIGLU_LIMIT = 7.0
RMS_EPS = 1e-5
NEG = -1e30
FORCE_SCORE = 1e4

GLA_WIDTH = GLA_HEADS * GLA_DV
NSA_WIDTH = NSA_HEADS * NSA_DH
KV_W = NSA_KV * NSA_DH

COL_GQ = 0
COL_GK = COL_GQ + GLA_HEADS * GLA_DK
COL_GV = COL_GK + GLA_HEADS * GLA_DK
COL_GG = COL_GV + GLA_WIDTH
COL_NQ = COL_GG + GLA_WIDTH
COL_KC = COL_NQ + NSA_WIDTH
COL_VC = COL_KC + KV_W
COL_KS = COL_VC + KV_W
COL_VS = COL_KS + KV_W
COL_KW = COL_VS + KV_W
COL_VW = COL_KW + KV_W
N_MAIN = COL_VW + KV_W
LANE = 128
ATT_T = 128
SEL_T = 256
GLA_C = 64
GLA_SUB = 16
MOE_SUB = 512
GATE_COL0 = GLA_RANK

VMEM_LIMIT = 56 * 1024 * 1024

NT = (((1,), (1,)), ((), ()))
TN = (((0,), (0,)), ((), ()))


def _dot(a, b):
    return jnp.dot(a, b, preferred_element_type=F32)


def _dg(a, b, dims):
    return lax.dot_general(a, b, dims, preferred_element_type=F32)


def _sigmoid(x):
    return 1.0 / (1.0 + jnp.exp(-x))


def _log_sigmoid(z):
    return -(jnp.maximum(-z, 0.0) + jnp.log(1.0 + jnp.exp(-jnp.abs(z))))


def _rms(x, g):
    return x * lax.rsqrt(jnp.mean(x * x, axis=-1, keepdims=True) + RMS_EPS) * g


def _split3(x):
    hi = x.astype(BF16)
    r1 = x - hi.astype(F32)
    mid = r1.astype(BF16)
    lo = (r1 - mid.astype(F32)).astype(BF16)
    return hi, mid, lo


def _pack_pairs(lo, hi):
    lo_bits = lax.bitcast_convert_type(lo.astype(BF16).astype(F32), U32) >> 16
    hi_bits = lax.bitcast_convert_type(hi.astype(BF16).astype(F32), U32) & jnp.uint32(0xFFFF0000)
    return lo_bits | hi_bits


def _unpack_pairs(w):
    lo = lax.bitcast_convert_type(w << 16, F32)
    hi = lax.bitcast_convert_type(w & jnp.uint32(0xFFFF0000), F32)
    return lo, hi


SUBLANE = 8


def _store_rows_tiled(ref, base, n, packed):
    for j in range(SUBLANE):
        ref[pl.ds(base * SUBLANE + j, n, stride=SUBLANE), :] = packed[:, j * LANE:(j + 1) * LANE]


def _load_rows_tiled(ref, base, n, lead=()):
    return [ref[lead + (pl.ds(base * SUBLANE + j, n, stride=SUBLANE), slice(None))] for j in range(SUBLANE)]


def _params(sem, vmem=VMEM_LIMIT):
    return pltpu.CompilerParams(dimension_semantics=sem, vmem_limit_bytes=vmem)


def _inproj_kernel(x_ref, g_ref, w_ref, wn_ref, wa2_ref, ba_ref, proj_ref, la_ref, nar_ref, xn_ref):
    @pl.when(pl.program_id(1) == 0)
    def _():
        xn = _rms(x_ref[...], g_ref[...]).astype(BF16)
        xn_ref[...] = xn
        nar = _dot(xn, wn_ref[...])
        nar_ref[...] = nar
        z = _dot(nar.astype(BF16), wa2_ref[...]) + ba_ref[...]
        la_ref[...] = _log_sigmoid(z) * (1.0 / GLA_TAU)

    proj_ref[...] = _dot(xn_ref[...], w_ref[...]).astype(BF16)


def _inproj(x2, g, w_main, w_nar, wa2p, ba, tm, tn):
    T, D = x2.shape
    nj = N_MAIN // tn
    return pl.pallas_call(
        _inproj_kernel,
        grid=(T // tm, nj),
        in_specs=[
            pl.BlockSpec((tm, D), lambda i, j: (i, 0)),
            pl.BlockSpec((1, D), lambda i, j: (0, 0)),
            pl.BlockSpec((D, tn), lambda i, j: (0, j)),
            pl.BlockSpec((D, LANE), lambda i, j: (0, 0)),
            pl.BlockSpec((LANE, GLA_HEADS * GLA_DK), lambda i, j: (0, 0)),
            pl.BlockSpec((1, GLA_HEADS * GLA_DK), lambda i, j: (0, 0)),
        ],
        out_specs=[
            pl.BlockSpec((tm, tn), lambda i, j: (i, j)),
            pl.BlockSpec((tm, GLA_HEADS * GLA_DK), lambda i, j: (i, 0)),
            pl.BlockSpec((tm, LANE), lambda i, j: (i, 0)),
        ],
        out_shape=[
            jax.ShapeDtypeStruct((T, N_MAIN), BF16),
            jax.ShapeDtypeStruct((T, GLA_HEADS * GLA_DK), F32),
            jax.ShapeDtypeStruct((T, LANE), F32),
        ],
        scratch_shapes=[pltpu.VMEM((tm, D), BF16)],
        compiler_params=_params(("parallel", "arbitrary")),
    )(x2, g, w_main, w_nar, wa2p, ba)


def _gla_kernel(q_ref, k_ref, v_ref, g_ref, la_ref, ng_ref, wf_ref, o_ref, wb_ref, st_ref, *, cb):
    wb_ref[...] = wf_ref[...].astype(BF16)

    @pl.when(pl.program_id(2) == 0)
    def _():
        st_ref[...] = jnp.zeros_like(st_ref)

    C, SUB = GLA_C, GLA_SUB
    ri = lax.broadcasted_iota(I32, (C, C), 0)
    ci = lax.broadcasted_iota(I32, (C, C), 1)
    tri = jnp.where(ri >= ci, 1.0, 0.0).astype(BF16)
    for n in range(cb // C):
        r0 = n * C
        hi, mid, lo = _split3(la_ref[r0:r0 + C, :])
        b = _dot(tri, hi) + _dot(tri, mid) + _dot(tri, lo)
        q = q_ref[r0:r0 + C, :].astype(F32)
        k = k_ref[r0:r0 + C, :].astype(F32)
        v = v_ref[r0:r0 + C, :]
        st = st_ref[...]
        o = _dg((q * jnp.exp(b)).astype(BF16), st.astype(BF16), NT)
        parts = []
        for s in range(C // SUB):
            lo_, hi_ = s * SUB, (s + 1) * SUB
            ref = b[lo_ - 1:lo_, :] if s > 0 else jnp.zeros((1, GLA_DK), F32)
            qs = (q[lo_:hi_] * jnp.exp(b[lo_:hi_] - ref)).astype(BF16)
            ks = (k[:hi_] * jnp.exp(ref - b[:hi_])).astype(BF16)
            sc = _dg(qs, ks, NT)
            row = lax.broadcasted_iota(I32, (SUB, hi_), 0) + lo_
            col = lax.broadcasted_iota(I32, (SUB, hi_), 1)
            sc = jnp.where(col <= row, sc, 0.0)
            parts.append(_dot(sc.astype(BF16), v[:hi_]))
        o = o + jnp.concatenate(parts, axis=0)
        bl = b[C - 1:C, :]
        kdec = (k * jnp.exp(bl - b)).astype(BF16)
        st_ref[...] = st * jnp.exp(bl) + _dg(v, kdec, TN)
        gg = g_ref[r0:r0 + C, :].astype(F32)
        o_ref[r0:r0 + C, :] = (_rms(o, ng_ref[...]) * (gg * _sigmoid(gg))).astype(BF16)


def _cast_slab(w2d, nsteps):
    rows, width = w2d.shape
    assert rows % nsteps == 0 and (rows // nsteps) % 16 == 0
    return rows // nsteps, width


def _gla(proj, la, ng, wf, B, S, cb):
    T = B * S
    ncb = S // cb
    row = lambda b, h, c: b * ncb + c
    slab_r, slab_w = _cast_slab(wf, B * GLA_HEADS * ncb)
    slab = pl.BlockSpec((slab_r, slab_w), lambda b, h, c: ((b * GLA_HEADS + h) * ncb + c, 0))
    return pl.pallas_call(
        functools.partial(_gla_kernel, cb=cb),
        grid=(B, GLA_HEADS, ncb),
        in_specs=[
            pl.BlockSpec((cb, GLA_DK), lambda b, h, c: (row(b, h, c), COL_GQ // GLA_DK + h)),
            pl.BlockSpec((cb, GLA_DK), lambda b, h, c: (row(b, h, c), COL_GK // GLA_DK + h)),
            pl.BlockSpec((cb, GLA_DV), lambda b, h, c: (row(b, h, c), COL_GV // GLA_DV + h)),
            pl.BlockSpec((cb, GLA_DV), lambda b, h, c: (row(b, h, c), COL_GG // GLA_DV + h)),
            pl.BlockSpec((cb, GLA_DK), lambda b, h, c: (row(b, h, c), h)),
            pl.BlockSpec((1, GLA_DV), lambda b, h, c: (0, 0)),
            slab,
        ],
        out_specs=[pl.BlockSpec((cb, GLA_DV), lambda b, h, c: (row(b, h, c), h)), slab],
        out_shape=[jax.ShapeDtypeStruct((T, GLA_WIDTH), BF16), jax.ShapeDtypeStruct(wf.shape, BF16)],
        scratch_shapes=[pltpu.VMEM((GLA_DV, GLA_DK), F32)],
        compiler_params=_params(("parallel", "parallel", "arbitrary")),
    )(proj, proj, proj, proj, la, ng, wf)


def _compress_kernel(hb_ref, pe_ref, w1_ref, w2_ref, o_ref, *, ncp):
    half = CMP_STRIDE * NSA_DH
    hb = hb_ref[0, 0].astype(F32)
    pe = pe_ref[0]
    top = (hb + pe[0:1]).astype(BF16)
    bot = (hb + pe[1:2]).astype(BF16)
    a = _dot(top, w1_ref[0, :half, :])
    bm = _dot(bot, w1_ref[0, half:, :])
    pre = a + pltpu.roll(bm, ncp - 1, 0)
    cdf = 0.5 * (1.0 + jnp.tanh(math.sqrt(2.0 / math.pi) * (pre + 0.044715 * (pre * pre * pre))))
    o_ref[0, 0] = _dot((pre * cdf).astype(BF16), w2_ref[0]).astype(BF16)


def _compress(hb, pe, w1, w2):
    _, BG, ncp, half = hb.shape
    return pl.pallas_call(
        functools.partial(_compress_kernel, ncp=ncp),
        grid=(2, BG),
        in_specs=[
            pl.BlockSpec((1, 1, ncp, half), lambda t, n: (t, n, 0, 0)),
            pl.BlockSpec((1, 2, half), lambda t, n: (t, 0, 0)),
            pl.BlockSpec((1, 2 * half, NSA_DH), lambda t, n: (t, 0, 0)),
            pl.BlockSpec((1, NSA_DH, NSA_DH), lambda t, n: (t, 0, 0)),
        ],
        out_specs=pl.BlockSpec((1, 1, ncp, NSA_DH), lambda t, n: (t, n, 0, 0)),
        out_shape=jax.ShapeDtypeStruct((2, BG, ncp, NSA_DH), BF16),
        compiler_params=_params(("parallel", "parallel")),
    )(hb, pe, w1, w2)


def _cmpattn_kernel(q_ref, kc_ref, vc_ref, bias_ref, ovt_ref, o_ref, sel_ref, *, ns, ncp):
    tq = ATT_T
    i = pl.program_id(1)
    kc = kc_ref[0, 0]
    vc = vc_ref[0, 0]
    pos = i * tq + lax.broadcasted_iota(I32, (tq, 1), 0)
    row_valid = jnp.where(pos >= CMP_LEN - 1, 1.0, 0.0)
    psum = jnp.zeros((tq, ncp), F32)
    outs = []
    for r in range(NSA_REP):
        q = q_ref[:, r * NSA_DH:(r + 1) * NSA_DH]
        s = _dg(q, kc, NT) + bias_ref[r]
        e = jnp.exp(s - jnp.max(s, axis=-1, keepdims=True))
        p = e / jnp.sum(e, axis=-1, keepdims=True) * row_valid
        psum = psum + p
        outs.append(_dot(p.astype(BF16), vc))
    o_ref[...] = jnp.concatenate(outs, axis=1).astype(BF16)

    ph = psum.astype(BF16)
    pl_ = (psum - ph.astype(F32)).astype(BF16)
    imp = _dg(ovt_ref[...], ph, NT) + _dg(ovt_ref[...], pl_, NT)
    n_i = lax.broadcasted_iota(I32, (ns, tq), 0)
    cur = (i * tq + lax.broadcasted_iota(I32, (ns, tq), 1)) // SEL_BLOCK
    forced = (n_i == 0) | (n_i == cur) | (n_i == cur - 1)
    score = jnp.where(forced, FORCE_SCORE, jnp.where(n_i <= cur, imp, -1.0))
    cnt = jnp.zeros((ns, tq), F32)
    for m in range(ns):
        sm = score[m:m + 1, :]
        ge = jnp.where(sm >= score, 1.0, 0.0)
        gt = jnp.where(sm > score, 1.0, 0.0)
        cnt = cnt + jnp.where(n_i > m, ge, gt)
    sel_t = jnp.where(cnt < min(SEL_TOPN, ns), 1.0, 0.0)
    if ns < LANE:
        sel_t = jnp.concatenate([sel_t, jnp.zeros((LANE - ns, tq), F32)], axis=0)
    sel_ref[0] = sel_t.T.astype(BF16)


def _cmpattn(proj, cmp_kv, bias_c, ovt, B, S):
    T = B * S
    tq = ATT_T
    nq = S // tq
    ns = S // SEL_BLOCK
    ncp = S // CMP_STRIDE
    G = NSA_KV
    qcol = COL_NQ // (NSA_REP * NSA_DH)
    return pl.pallas_call(
        functools.partial(_cmpattn_kernel, ns=ns, ncp=ncp),
        grid=(G, nq, B),
        in_specs=[
            pl.BlockSpec((tq, NSA_REP * NSA_DH), lambda g, i, b: (b * nq + i, qcol + g)),
            pl.BlockSpec((1, 1, ncp, NSA_DH), lambda g, i, b: (0, b * G + g, 0, 0)),
            pl.BlockSpec((1, 1, ncp, NSA_DH), lambda g, i, b: (1, b * G + g, 0, 0)),
            pl.BlockSpec((NSA_REP, tq, ncp), lambda g, i, b: (g, i, 0)),
            pl.BlockSpec((ns, ncp), lambda g, i, b: (0, 0)),
        ],
        out_specs=[
            pl.BlockSpec((tq, NSA_REP * NSA_DH), lambda g, i, b: (b * nq + i, g)),
            pl.BlockSpec((1, tq, LANE), lambda g, i, b: (b * G + g, i, 0)),
        ],
        out_shape=[
            jax.ShapeDtypeStruct((T, NSA_WIDTH), BF16),
            jax.ShapeDtypeStruct((B * G, S, LANE), BF16),
        ],
        compiler_params=_params(("parallel", "parallel", "parallel")),
    )(proj, cmp_kv, cmp_kv, bias_c, ovt)


def _flash_update(s, v, m_ref, al_ref):
    reps = s.shape[1] // LANE
    m_old = m_ref[...]
    m_new = jnp.maximum(m_old, jnp.max(s, axis=-1, keepdims=True))
    alpha = jnp.exp(m_old - m_new)
    p = jnp.exp(s - jnp.concatenate([m_new] * reps, axis=1))
    v_ext = jnp.concatenate([v, jnp.ones(v.shape, BF16)], axis=1)
    al_ref[...] = jnp.concatenate([alpha, alpha], axis=1) * al_ref[...] + _dot(p.astype(BF16), v_ext)
    m_ref[...] = m_new


def _selwin_kernel(q_ref, ks_ref, vs_ref, kw_ref, vw_ref, sel_ref, d_ref, w4_ref, wf_ref,
                   osel_ref, owin_ref, wb_ref, m_ref, al_ref):
    wb_ref[...] = wf_ref[...].astype(BF16)
    t = SEL_T
    R = NSA_REP
    i = pl.program_id(2)
    q = jnp.concatenate([q_ref[:, r * NSA_DH:(r + 1) * NSA_DH] for r in range(R)], axis=0)
    sel = sel_ref[0]
    n_i = lax.broadcasted_iota(I32, (LANE, t), 0)
    c_blk = lax.broadcasted_iota(I32, (LANE, t), 1) // SEL_BLOCK
    blk_per_tile = t // SEL_BLOCK

    def init():
        m_ref[...] = jnp.full(m_ref.shape, NEG, F32)
        al_ref[...] = jnp.zeros(al_ref.shape, F32)

    def tile(ref, j):
        return ref[pl.ds(pl.multiple_of(j * t, t), t), :]

    def logits(k_ref, j, add):
        s = _dg(q, tile(k_ref, j), NT).reshape(R, t, t)
        return (s + add).reshape(R * t, t)

    def sel_mask(j):
        expand = jnp.where(n_i == blk_per_tile * j + c_blk, 1.0, 0.0).astype(BF16)
        return (_dot(sel, expand) - 1.0) * (-NEG)

    def finish(o_ref):
        al = al_ref[...]
        o = al[:, :NSA_DH] / al[:, NSA_DH:]
        o_ref[...] = jnp.concatenate([o[r * t:(r + 1) * t] for r in range(R)], axis=1).astype(BF16)

    init()

    def far(j, c):
        _flash_update(logits(ks_ref, j, sel_mask(j)[None]), tile(vs_ref, j), m_ref, al_ref)
        return c

    lax.fori_loop(0, jnp.maximum(i - 1, 0), far, 0)
    jn = jnp.maximum(i - 1, 0)
    off = jnp.where(i == 0, NEG, 0.0)
    _flash_update(logits(ks_ref, jn, (sel_mask(jn) + off)[None] + d_ref[:, 1]), tile(vs_ref, jn),
                  m_ref, al_ref)
    _flash_update(logits(ks_ref, i, sel_mask(i)[None] + d_ref[:, 0]), tile(vs_ref, i),
                  m_ref, al_ref)
    finish(osel_ref)

    init()
    pre = WINDOW // t
    for o in range(pre, -1, -1):
        jw = i - o
        jc = jnp.maximum(jw, 0)
        off = jnp.where(jw < 0, NEG, 0.0)
        if o == 0:
            add = d_ref[:, 0] + off
        elif o == 1:
            add = d_ref[:, 1] + off
        elif o == pre:
            add = (w4_ref[...] + off)[None]
        else:
            add = off
        _flash_update(logits(kw_ref, jc, add), tile(vw_ref, jc), m_ref, al_ref)
    finish(owin_ref)


def _selwin(proj, sel, dtiles, w4, wf, B, S):
    T = B * S
    t = SEL_T
    nq = S // t
    G = NSA_KV
    qcol = COL_NQ // (NSA_REP * NSA_DH)
    kv = lambda col: pl.BlockSpec((S, NSA_DH), lambda b, g, i, col=col: (b, col // NSA_DH + g))
    slab_r, slab_w = _cast_slab(wf, B * G * nq)
    slab = pl.BlockSpec((slab_r, slab_w), lambda b, g, i: ((b * G + g) * nq + i, 0))
    return pl.pallas_call(
        _selwin_kernel,
        grid=(B, G, nq),
        in_specs=[
            pl.BlockSpec((t, NSA_REP * NSA_DH), lambda b, g, i: (b * nq + i, qcol + g)),
            kv(COL_KS), kv(COL_VS), kv(COL_KW), kv(COL_VW),
            pl.BlockSpec((1, t, LANE), lambda b, g, i: (b * G + g, i, 0)),
            pl.BlockSpec((NSA_REP, 2, t, t), lambda b, g, i: (g, 0, 0, 0)),
            pl.BlockSpec((t, t), lambda b, g, i: (0, 0)),
            slab,
        ],
        out_specs=[
            pl.BlockSpec((t, NSA_REP * NSA_DH), lambda b, g, i: (b * nq + i, g)),
            pl.BlockSpec((t, NSA_REP * NSA_DH), lambda b, g, i: (b * nq + i, g)),
            slab,
        ],
        out_shape=[
            jax.ShapeDtypeStruct((T, NSA_WIDTH), BF16),
            jax.ShapeDtypeStruct((T, NSA_WIDTH), BF16),
            jax.ShapeDtypeStruct(wf.shape, BF16),
        ],
        scratch_shapes=[
            pltpu.VMEM((NSA_REP * t, LANE), F32),
            pltpu.VMEM((NSA_REP * t, 2 * NSA_DH), F32),
        ],
        compiler_params=_params(("parallel", "parallel", "parallel")),
    )(proj, proj, proj, proj, proj, sel, dtiles, w4, wf)


def _outproj_kernel(x_ref, og_ref, oc_ref, os_ref, ow_ref, nar_ref, ng_ref, wo_ref, fg_ref,
                    rwh_ref, rwl_ref, rb_ref, h1_ref, hn_ref, ti_ref, tw_ref):
    gates = _sigmoid(nar_ref[...])
    parts = []
    for h in range(NSA_HEADS):
        cs = slice(h * NSA_DH, (h + 1) * NSA_DH)
        g0 = gates[:, GATE_COL0 + h:GATE_COL0 + h + 1]
        g1 = gates[:, GATE_COL0 + NSA_HEADS + h:GATE_COL0 + NSA_HEADS + h + 1]
        g2 = gates[:, GATE_COL0 + 2 * NSA_HEADS + h:GATE_COL0 + 2 * NSA_HEADS + h + 1]
        parts.append(g0 * oc_ref[:, cs].astype(F32) + g1 * os_ref[:, cs].astype(F32)
                     + g2 * ow_ref[:, cs].astype(F32))
    on = _rms(jnp.concatenate(parts, axis=1), ng_ref[...]).astype(BF16)
    h1 = x_ref[...] + _dot(og_ref[...], wo_ref[:GLA_WIDTH, :]) + _dot(on, wo_ref[GLA_WIDTH:, :])
    h1_ref[...] = h1
    hn = _rms(h1, fg_ref[...])
    half = hn.shape[1] // 2
    _store_rows_tiled(hn_ref, 0, hn.shape[0], _pack_pairs(hn[:, :half], hn[:, half:]))

    hh = hn.astype(BF16)
    hl = (hn - hh.astype(F32)).astype(BF16)
    lg = _dot(hh, rwh_ref[...]) + _dot(hl, rwh_ref[...]) + _dot(hh, rwl_ref[...]) + rb_ref[...]
    lane = lax.broadcasted_iota(I32, lg.shape, 1)
    vals, idxs = [], []
    for _ in range(TOP_K):
        mx = jnp.max(lg, axis=-1, keepdims=True)
        ix = jnp.min(jnp.where(lg == mx, lane, LANE), axis=-1, keepdims=True)
        vals.append(mx)
        idxs.append(ix)
        lg = jnp.where(lane == ix, -3e38, lg)
    es = [jnp.exp(v - vals[0]) for v in vals]
    den = es[0] + es[1] + es[2] + es[3]
    ti = jnp.zeros(lg.shape, I32)
    tw = jnp.zeros(lg.shape, F32)
    for kk in range(TOP_K):
        ti = jnp.where(lane == kk, idxs[kk], ti)
        tw = jnp.where(lane == kk, es[kk] / den, tw)
    ti_ref[...] = ti
    tw_ref[...] = tw


def _outproj(x2, og, oc, osel, ow, nar, ng, wo, fg, rwh, rwl, rb, tm):
    T, D = x2.shape
    row = lambda w: pl.BlockSpec((tm, w), lambda i: (i, 0))
    full = lambda a: pl.BlockSpec(a.shape, lambda i: (0,) * a.ndim)
    return pl.pallas_call(
        _outproj_kernel,
        grid=(T // tm,),
        in_specs=[row(D), row(GLA_WIDTH), row(NSA_WIDTH), row(NSA_WIDTH), row(NSA_WIDTH), row(LANE),
                  full(ng), full(wo), full(fg), full(rwh), full(rwl), full(rb)],
        out_specs=[row(D), pl.BlockSpec((tm * SUBLANE, LANE), lambda i: (i, 0)), row(LANE), row(LANE)],
        out_shape=[
            jax.ShapeDtypeStruct((T, D), F32),
            jax.ShapeDtypeStruct((T * SUBLANE, LANE), U32),
            jax.ShapeDtypeStruct((T, LANE), I32),
            jax.ShapeDtypeStruct((T, LANE), F32),
        ],
        compiler_params=_params(("parallel",)),
    )(x2, og, oc, osel, ow, nar, ng, wo, fg, rwh, rwl, rb)


def _moe_kernel(te_ref, tv_ref, tok_hbm, dst_hbm, hn_hbm, wgt_ref, w1g_ref, w1l_ref, b1g_ref, b1l_ref,
                w2_ref, b2_ref, out_hbm, tok_smem, dst_smem, xbuf, xb_ref, acc_ref, ybuf,
                sem_tok, sem_dst, sem_in, sem_out, *, tm, nf, n_tiles):
    i = pl.program_id(0)
    f = pl.program_id(1)
    slot = i % 2
    valid = tv_ref[i]
    sub = MOE_SUB
    nsub = tm // sub
    half = SUBLANE * LANE
    sl8 = SUBLANE

    def tile_of(r):
        return pl.ds(pl.multiple_of(r * sl8, sl8), sl8)

    def tok_copy(t, sl):
        return pltpu.make_async_copy(tok_hbm.at[t], tok_smem.at[sl], sem_tok.at[sl])

    def dst_copy():
        return pltpu.make_async_copy(dst_hbm.at[i], dst_smem, sem_dst)

    def issue_gather(t, sl):
        for s in range(nsub):
            @pl.when(s * sub < tv_ref[t])
            def _():
                def body(r, c):
                    pltpu.make_async_copy(hn_hbm.at[tile_of(tok_smem[sl, r]), :],
                                          xbuf.at[sl, tile_of(r), :], sem_in.at[sl, s]).start()
                    return c

                lax.fori_loop(s * sub, (s + 1) * sub, body, 0, unroll=8)

    def wait_gather(sl, s):
        pltpu.make_async_copy(hn_hbm.at[pl.ds(0, sub * sl8), :], xbuf.at[sl, pl.ds(s * sub * sl8, sub * sl8), :],
                              sem_in.at[sl, s]).wait()

    def wait_scatter(count):
        for s in range(nsub):
            @pl.when(s * sub < count)
            def _():
                pltpu.make_async_copy(ybuf.at[pl.ds(s * sub * sl8, sub * sl8), :],
                                      out_hbm.at[pl.ds(0, sub * sl8), :], sem_out).wait()

    @pl.when(f == 0)
    def _():
        @pl.when(i == 0)
        def _():
            cp = tok_copy(0, 0)
            cp.start()
            cp.wait()
            issue_gather(0, 0)

        @pl.when(i + 1 < n_tiles)
        def _():
            tok_copy(i + 1, 1 - slot).start()

        dst_copy().start()
        for s in range(nsub):
            @pl.when(s * sub < valid)
            def _():
                wait_gather(slot, s)
                for j, w in enumerate(_load_rows_tiled(xbuf, s * sub, sub, lead=(slot,))):
                    lo, hi = _unpack_pairs(w)
                    xb_ref[s * sub:(s + 1) * sub, j * LANE:(j + 1) * LANE] = lo.astype(BF16)
                    xb_ref[s * sub:(s + 1) * sub, half + j * LANE:half + (j + 1) * LANE] = hi.astype(BF16)

        @pl.when(i + 1 < n_tiles)
        def _():
            tok_copy(i + 1, 1 - slot).wait()
            issue_gather(i + 1, 1 - slot)

    for s in range(nsub):
        @pl.when(s * sub < valid)
        def _():
            rows = slice(s * sub, (s + 1) * sub)
            xb = xb_ref[rows, :]
            glu = jnp.minimum(_dot(xb, w1g_ref[0]) + b1g_ref[0], SWIGLU_LIMIT)
            lin = jnp.clip(_dot(xb, w1l_ref[0]) + b1l_ref[0], -SWIGLU_LIMIT, SWIGLU_LIMIT)
            act = (glu * _sigmoid(SWIGLU_ALPHA * glu) * (lin + 1.0)).astype(BF16)
            part = _dot(act, w2_ref[0])

            @pl.when(f == 0)
            def _():
                acc_ref[rows, :] = part

            @pl.when(f > 0)
            def _():
                acc_ref[rows, :] += part

    @pl.when(f == nf - 1)
    def _():
        @pl.when(i > 0)
        def _():
            wait_scatter(tv_ref[jnp.maximum(i - 1, 0)])

        dst_copy().wait()
        for s in range(nsub):
            @pl.when(s * sub < valid)
            def _():
                rows = slice(s * sub, (s + 1) * sub)
                y = (acc_ref[rows, :] + b2_ref[0]) * wgt_ref[0, rows, :]
                _store_rows_tiled(ybuf, s * sub, sub, _pack_pairs(y[:, :half], y[:, half:]))

                def body(r, c):
                    pltpu.make_async_copy(ybuf.at[tile_of(r), :], out_hbm.at[tile_of(dst_smem[r]), :],
                                          sem_out).start()
                    return c

                lax.fori_loop(s * sub, (s + 1) * sub, body, 0, unroll=8)

        @pl.when(i == n_tiles - 1)
        def _():
            wait_scatter(valid)
            ybuf[...] = jnp.zeros_like(ybuf)
            dump = pltpu.make_async_copy(ybuf, out_hbm.at[pl.ds(out_hbm.shape[0] - tm * sl8, tm * sl8), :], sem_out)
            dump.start()
            dump.wait()


def _moe(tile_exp, tile_valid, tok, dst, hn, wgt, w1, b1, w2, b2, n_out_rows, tm, fc):
    n_tiles = tok.shape[0]
    D = w1.shape[1]
    assert D == 2 * SUBLANE * LANE
    dff = w2.shape[1]
    nf = dff // fc
    grid_spec = pltpu.PrefetchScalarGridSpec(
        num_scalar_prefetch=2,
        grid=(n_tiles, nf),
        in_specs=[
            pl.BlockSpec(memory_space=pl.ANY),
            pl.BlockSpec(memory_space=pl.ANY),
            pl.BlockSpec(memory_space=pl.ANY),
            pl.BlockSpec((1, tm, 1), lambda i, f, te, tv: (i, 0, 0)),
            pl.BlockSpec((1, D, fc), lambda i, f, te, tv: (te[i], 0, f)),
            pl.BlockSpec((1, D, fc), lambda i, f, te, tv: (te[i], 0, nf + f)),
            pl.BlockSpec((1, 1, fc), lambda i, f, te, tv: (te[i], 0, f)),
            pl.BlockSpec((1, 1, fc), lambda i, f, te, tv: (te[i], 0, nf + f)),
            pl.BlockSpec((1, fc, D), lambda i, f, te, tv: (te[i], f, 0)),
            pl.BlockSpec((1, 1, D), lambda i, f, te, tv: (te[i], 0, 0)),
        ],
        out_specs=pl.BlockSpec(memory_space=pl.ANY),
        scratch_shapes=[
            pltpu.SMEM((2, tm), I32),
            pltpu.SMEM((tm,), I32),
            pltpu.VMEM((2, tm * SUBLANE, LANE), U32),
            pltpu.VMEM((tm, D), BF16),
            pltpu.VMEM((tm, D), F32),
            pltpu.VMEM((tm * SUBLANE, LANE), U32),
            pltpu.SemaphoreType.DMA((2,)),
            pltpu.SemaphoreType.DMA(()),
            pltpu.SemaphoreType.DMA((2, tm // MOE_SUB)),
            pltpu.SemaphoreType.DMA(()),
        ],
    )
    return pl.pallas_call(
        functools.partial(_moe_kernel, tm=tm, nf=nf, n_tiles=n_tiles),
        grid_spec=grid_spec,
        out_shape=jax.ShapeDtypeStruct((n_out_rows * SUBLANE, LANE), U32),
        compiler_params=_params(("arbitrary", "arbitrary")),
    )(tile_exp, tile_valid, tok, dst, hn, wgt, w1, w1, b1, b1, w2, b2)


def _combine_kernel(h1_ref, y0_ref, y1_ref, y2_ref, y3_ref, g_ref, o_ref):
    h = h1_ref[...]
    tm = h.shape[0]
    for y_ref in (y0_ref, y1_ref, y2_ref, y3_ref):
        pairs = [_unpack_pairs(w) for w in _load_rows_tiled(y_ref, 0, tm)]
        h = h + jnp.concatenate([p[0] for p in pairs] + [p[1] for p in pairs], axis=1)
    o_ref[...] = _rms(h, g_ref[...])


def _combine(ys, h1, g, tm):
    T, D = h1.shape
    nb = T // tm
    slab = lambda k: pl.BlockSpec((tm * SUBLANE, LANE), lambda i, k=k: (k * nb + i, 0))
    return pl.pallas_call(
        _combine_kernel,
        grid=(nb,),
        in_specs=[pl.BlockSpec((tm, D), lambda i: (i, 0))] + [slab(k) for k in range(TOP_K)]
        + [pl.BlockSpec((1, D), lambda i: (0, 0))],
        out_specs=pl.BlockSpec((tm, D), lambda i: (i, 0)),
        out_shape=jax.ShapeDtypeStruct((T, D), F32),
        compiler_params=_params(("parallel",)),
    )(h1, ys, ys, ys, ys, g)


def _t5_bucket(dist):
    n = jnp.maximum(dist, 0)
    exact = N_BUCKETS // 2
    scaled = jnp.log(jnp.maximum(n, 1).astype(F32) / exact) / math.log(MAX_DISTANCE / exact)
    large = exact + (scaled * (N_BUCKETS - exact)).astype(I32)
    return jnp.where(n < exact, n, jnp.minimum(large, N_BUCKETS - 1)).astype(I32)


def _bias_tables(rel_bias, S):
    shifted = (rel_bias - rel_bias[N_BUCKETS - 1][None, :]).astype(F32)

    def lookup(dist):
        bucket = _t5_bucket(dist)
        out = jnp.zeros((NSA_HEADS,) + dist.shape, F32)
        for kb in range(N_BUCKETS - 1):
            out = jnp.where((bucket == kb)[None], shifted[kb].reshape((NSA_HEADS,) + (1,) * dist.ndim), out)
        return jnp.where((dist >= 0)[None], out, NEG)

    t = SEL_T
    r = jnp.arange(t)[:, None]
    c = jnp.arange(t)[None, :]
    dtiles = jnp.stack([lookup(r - c), lookup(t + r - c)], axis=1)
    w4 = jnp.where(r < c, 0.0, NEG).astype(F32)
    ncp = S // CMP_STRIDE
    dist_c = jnp.arange(S)[:, None] - (jnp.arange(ncp) * CMP_STRIDE + CMP_LEN - 1)[None, :]
    return dtiles, w4, lookup(dist_c)


def _overlap_t(S):
    ncp = S // CMP_STRIDE
    ns = S // SEL_BLOCK
    cs = jnp.arange(ncp)[None, :] * CMP_STRIDE
    ss = jnp.arange(ns)[:, None] * SEL_BLOCK
    ov = (cs < ss + SEL_BLOCK) & (cs + CMP_LEN > ss) & (jnp.arange(ncp)[None, :] < ncp - 1)
    return ov.astype(BF16)


def _route(top_idx, top_w, n_exp, tm):
    T = top_idx.shape[0]
    A = T * TOP_K
    e_flat = top_idx.reshape(A)
    w_flat = top_w.reshape(A)
    e_sorted, order = lax.sort_key_val(e_flat, jnp.arange(A, dtype=I32), is_stable=True)
    counts = jnp.sum((e_flat[:, None] == jnp.arange(n_exp, dtype=I32)[None, :]).astype(I32), axis=0)
    padded = (counts + tm - 1) // tm * tm
    start = jnp.cumsum(counts) - counts
    pend = jnp.cumsum(padded)
    pstart = pend - padded
    n_tiles = A // tm + n_exp
    tile_row0 = jnp.arange(n_tiles, dtype=I32) * tm
    tile_exp = jnp.minimum(jnp.sum((pend[None, :] <= tile_row0[:, None]).astype(I32), axis=1), n_exp - 1)
    tile_valid = jnp.clip(pstart[tile_exp] + counts[tile_exp] - tile_row0, 0, tm).astype(I32)
    off = tile_row0[:, None] - pstart[tile_exp][:, None] + jnp.arange(tm, dtype=I32)[None, :]
    live = jnp.arange(tm, dtype=I32)[None, :] < tile_valid[:, None]
    src = order[jnp.clip(start[tile_exp][:, None] + off, 0, A - 1)]
    buf_tok = jnp.where(live, src // TOP_K, 0)
    buf_w = jnp.where(live, w_flat[src], 0.0)
    buf_dst = jnp.where(live, (src % TOP_K) * T + src // TOP_K, TOP_K * T + jnp.arange(tm, dtype=I32)[None, :])
    return buf_tok, buf_dst, buf_w[:, :, None], tile_exp, tile_valid


def kernel(x, norm_mix_g, w_in, gla_wa2, gla_ba, gla_norm_g, cmp_pe_k, cmp_wk1, cmp_wk2, cmp_pe_v,
           cmp_wv1, cmp_wv2, nsa_norm_g, rel_bias, w_out, norm_ffn_g, router_w, router_b, exp_w1,
           exp_b1, exp_w2, exp_b2, norm_final_g):
    B, S, D = x.shape
    T = B * S
    G = NSA_KV
    n_exp = router_w.shape[-1]
    assert w_in.shape[0] == 1 and S % 1024 == 0 and S // SEL_BLOCK <= LANE and n_exp <= LANE
    x2 = x.reshape(T, D)

    w = w_in[0]
    a0 = COL_NQ
    n0 = a0 + GLA_RANK
    g0 = n0 + (N_MAIN - COL_NQ)
    w_main = jnp.concatenate([
        w[:, :COL_GK] * (GLA_DK ** -0.5), w[:, COL_GK:a0],
        w[:, n0:n0 + NSA_WIDTH] * (NSA_DH ** -0.5), w[:, n0 + NSA_WIDTH:g0]], axis=1).astype(BF16)
    w_nar = jnp.concatenate([w[:, a0:n0], w[:, g0:], jnp.zeros((D, LANE - GLA_RANK - 3 * NSA_HEADS), F32)],
                            axis=1).astype(BF16)
    wa2p = jnp.concatenate([gla_wa2[0], jnp.zeros((LANE - GLA_RANK, GLA_HEADS * GLA_DK), F32)], axis=0).astype(BF16)

    tm1 = min(1024, T)
    proj, log_a, nar = _inproj(x2, norm_mix_g, w_main, w_nar, wa2p, gla_ba, tm1, 512)

    e_w1, e_w2 = exp_w1[0], exp_w2[0]
    o_gla, w1b = _gla(proj, log_a, gla_norm_g, e_w1.reshape(-1, e_w1.shape[-1]), B, S, min(512, S))
    w1b = w1b.reshape(e_w1.shape)

    ncp = S // CMP_STRIDE
    half = CMP_STRIDE * NSA_DH

    def stride_groups(col):
        t4 = proj[:, col:col + KV_W].reshape(B, ncp, CMP_STRIDE, G, NSA_DH)
        return t4.transpose(0, 3, 1, 2, 4).reshape(B * G, ncp, half)

    hb = jnp.stack([stride_groups(COL_KC), stride_groups(COL_VC)], axis=0)
    pe = jnp.stack([cmp_pe_k[0].reshape(2, half), cmp_pe_v[0].reshape(2, half)], axis=0)
    w1c = jnp.stack([cmp_wk1[0], cmp_wv1[0]], axis=0).astype(BF16)
    w2c = jnp.stack([cmp_wk2[0], cmp_wv2[0]], axis=0).astype(BF16)
    cmp_kv = _compress(hb, pe, w1c, w2c)

    dtiles, w4, bias_c = _bias_tables(rel_bias, S)
    o_cmp, sel = _cmpattn(proj, cmp_kv, bias_c, _overlap_t(S), B, S)
    o_sel, o_win, w2b = _selwin(proj, sel, dtiles, w4, e_w2.reshape(-1, e_w2.shape[-1]), B, S)
    w2b = w2b.reshape(e_w2.shape)

    rw = jnp.concatenate([router_w[0], jnp.zeros((D, LANE - n_exp), F32)], axis=1)
    rwh = rw.astype(BF16)
    rwl = (rw - rwh.astype(F32)).astype(BF16)
    rb = jnp.concatenate([router_b[0], jnp.full((LANE - n_exp,), NEG, F32)])[None, :]
    h1, hn, top_i, top_w = _outproj(x2, o_gla, o_cmp, o_sel, o_win, nar, nsa_norm_g, w_out[0].astype(BF16),
                                    norm_ffn_g, rwh, rwl, rb, min(256, T))

    tm6 = min(1024, T)
    tok, dst, wgt, tile_exp, tile_valid = _route(top_i[:, :TOP_K], top_w[:, :TOP_K], n_exp, tm6)
    ys = _moe(tile_exp, tile_valid, tok, dst, hn, wgt, w1b, exp_b1[0][:, None, :],
              w2b, exp_b2[0][:, None, :], TOP_K * T + tm6, tm6, 512)
    out = _combine(ys, h1, norm_final_g[None, :], min(512, T))
    return out.reshape(B, S, D)
```

```python
import functools
import math

import numpy as np
import jax
import jax.numpy as jnp
from jax import lax
from jax.experimental import pallas as pl
from jax.experimental.pallas import tpu as pltpu

F32 = jnp.float32
BF16 = jnp.bfloat16
I32 = jnp.int32
U32 = jnp.uint32

GLA_HEADS = 4
GLA_DK = 128
GLA_DV = 256
GLA_RANK = 16
GLA_TAU = 16.0
NSA_HEADS = 8
NSA_KV = 2
NSA_REP = NSA_HEADS // NSA_KV
NSA_DH = 128
CMP_LEN = 32
CMP_STRIDE = 16
SEL_BLOCK = 64
SEL_TOPN = 16
WINDOW = 512
N_BUCKETS = 32
MAX_DISTANCE = 128
TOP_K = 4
SWIGLU_ALPHA = 1.702
SWIGLU_LIMIT = 7.0
RMS_EPS = 1e-5
NEG = -1e30
FORCE_SCORE = 1e4

GLA_WIDTH = GLA_HEADS * GLA_DV
NSA_WIDTH = NSA_HEADS * NSA_DH
KV_W = NSA_KV * NSA_DH

COL_GQ = 0
COL_GK = COL_GQ + GLA_HEADS * GLA_DK
COL_GV = COL_GK + GLA_HEADS * GLA_DK
COL_GG = COL_GV + GLA_WIDTH
COL_NQ = COL_GG + GLA_WIDTH
COL_KC = COL_NQ + NSA_WIDTH
COL_VC = COL_KC + KV_W
COL_KS = COL_VC + KV_W
COL_VS = COL_KS + KV_W
COL_KW = COL_VS + KV_W
COL_VW = COL_KW + KV_W
N_MAIN = COL_VW + KV_W
LANE = 128
ATT_T = 128
SEL_T = 256
GLA_C = 64
GLA_SUB = 16
MOE_SUB = 512
GATE_COL0 = GLA_RANK

VMEM_LIMIT = 56 * 1024 * 1024

NT = (((1,), (1,)), ((), ()))
TN = (((0,), (0,)), ((), ()))


def _dot(a, b):
    return jnp.dot(a, b, preferred_element_type=F32)


def _dg(a, b, dims):
    return lax.dot_general(a, b, dims, preferred_element_type=F32)


def _sigmoid(x):
    return 1.0 / (1.0 + jnp.exp(-x))


def _log_sigmoid(z):
    return -(jnp.maximum(-z, 0.0) + jnp.log(1.0 + jnp.exp(-jnp.abs(z))))


def _rms(x, g):
    return x * lax.rsqrt(jnp.mean(x * x, axis=-1, keepdims=True) + RMS_EPS) * g


def _split3(x):
    hi = x.astype(BF16)
    r1 = x - hi.astype(F32)
    mid = r1.astype(BF16)
    lo = (r1 - mid.astype(F32)).astype(BF16)
    return hi, mid, lo


def _pack_pairs(lo, hi):
    lo_bits = lax.bitcast_convert_type(lo.astype(BF16).astype(F32), U32) >> 16
    hi_bits = lax.bitcast_convert_type(hi.astype(BF16).astype(F32), U32) & jnp.uint32(0xFFFF0000)
    return lo_bits | hi_bits


def _unpack_pairs(w):
    lo = lax.bitcast_convert_type(w << 16, F32)
    hi = lax.bitcast_convert_type(w & jnp.uint32(0xFFFF0000), F32)
    return lo, hi


SUBLANE = 8


def _store_rows_tiled(ref, base, n, packed):
    for j in range(SUBLANE):
        ref[pl.ds(base * SUBLANE + j, n, stride=SUBLANE), :] = packed[:, j * LANE:(j + 1) * LANE]


def _load_rows_tiled(ref, base, n, lead=()):
    return [ref[lead + (pl.ds(base * SUBLANE + j, n, stride=SUBLANE), slice(None))] for j in range(SUBLANE)]


def _params(sem, vmem=VMEM_LIMIT):
    return pltpu.CompilerParams(dimension_semantics=sem, vmem_limit_bytes=vmem)


def _inproj_kernel(x_ref, g_ref, w_ref, wn_ref, wa2_ref, ba_ref, proj_ref, la_ref, nar_ref, xn_ref):
    @pl.when(pl.program_id(1) == 0)
    def _():
        xn = _rms(x_ref[...], g_ref[...]).astype(BF16)
        xn_ref[...] = xn
        nar = _dot(xn, wn_ref[...])
        nar_ref[...] = nar
        z = _dot(nar.astype(BF16), wa2_ref[...]) + ba_ref[...]
        la_ref[...] = _log_sigmoid(z) * (1.0 / GLA_TAU)

    proj_ref[...] = _dot(xn_ref[...], w_ref[...]).astype(BF16)


def _inproj(x2, g, w_main, w_nar, wa2p, ba, tm, tn):
    T, D = x2.shape
    nj = N_MAIN // tn
    return pl.pallas_call(
        _inproj_kernel,
        grid=(T // tm, nj),
        in_specs=[
            pl.BlockSpec((tm, D), lambda i, j: (i, 0)),
            pl.BlockSpec((1, D), lambda i, j: (0, 0)),
            pl.BlockSpec((D, tn), lambda i, j: (0, j)),
            pl.BlockSpec((D, LANE), lambda i, j: (0, 0)),
            pl.BlockSpec((LANE, GLA_HEADS * GLA_DK), lambda i, j: (0, 0)),
            pl.BlockSpec((1, GLA_HEADS * GLA_DK), lambda i, j: (0, 0)),
        ],
        out_specs=[
            pl.BlockSpec((tm, tn), lambda i, j: (i, j)),
            pl.BlockSpec((tm, GLA_HEADS * GLA_DK), lambda i, j: (i, 0)),
            pl.BlockSpec((tm, LANE), lambda i, j: (i, 0)),
        ],
        out_shape=[
            jax.ShapeDtypeStruct((T, N_MAIN), BF16),
            jax.ShapeDtypeStruct((T, GLA_HEADS * GLA_DK), F32),
            jax.ShapeDtypeStruct((T, LANE), F32),
        ],
        scratch_shapes=[pltpu.VMEM((tm, D), BF16)],
        compiler_params=_params(("parallel", "arbitrary")),
    )(x2, g, w_main, w_nar, wa2p, ba)


def _gla_kernel(q_ref, k_ref, v_ref, g_ref, la_ref, ng_ref, o_ref, st_ref, *, cb):
    @pl.when(pl.program_id(1) == 0)
    def _():
        st_ref[...] = jnp.zeros_like(st_ref)

    C, SUB = GLA_C, GLA_SUB
    heads = range(GLA_HEADS)
    dk = lambda h: slice(h * GLA_DK, (h + 1) * GLA_DK)
    dv = lambda h: slice(h * GLA_DV, (h + 1) * GLA_DV)
    ri = lax.broadcasted_iota(I32, (C, C), 0)
    ci = lax.broadcasted_iota(I32, (C, C), 1)
    tri = jnp.where(ri >= ci, 1.0, 0.0).astype(BF16)
    nsub = C // SUB
    sub_of_row = lax.broadcasted_iota(I32, (C, GLA_DK), 0) // SUB
    for n in range(cb // C):
        rows = slice(n * C, (n + 1) * C)
        b = []
        for h in heads:
            hi, mid, lo = _split3(la_ref[rows, dk(h)])
            b.append(_dot(tri, hi) + _dot(tri, mid) + _dot(tri, lo))
        q = [q_ref[rows, dk(h)].astype(F32) for h in heads]
        k = [k_ref[rows, dk(h)].astype(F32) for h in heads]
        v = [v_ref[rows, dv(h)] for h in heads]
        st = [st_ref[h] for h in heads]
        o = [_dg((q[h] * jnp.exp(b[h])).astype(BF16), st[h].astype(BF16), NT) for h in heads]
        intra = []
        for h in heads:
            refs = [jnp.zeros((1, GLA_DK), F32)] + [b[h][s * SUB - 1:s * SUB, :] for s in range(1, nsub)]
            ref_row = jnp.concatenate([jnp.broadcast_to(r, (SUB, GLA_DK)) for r in refs], axis=0)
            qsc = q[h] * jnp.exp(b[h] - ref_row)
            q_cat = jnp.concatenate([jnp.where(sub_of_row == s, qsc, 0.0) for s in range(nsub)], axis=1)
            k_cat = jnp.concatenate([jnp.where(sub_of_row <= s, k[h] * jnp.exp(refs[s] - b[h]), 0.0)
                                     for s in range(nsub)], axis=1)
            sc = jnp.where(ci <= ri, _dg(q_cat.astype(BF16), k_cat.astype(BF16), NT), 0.0)
            intra.append(_dot(sc.astype(BF16), v[h]))
        for h in heads:
            bl = b[h][C - 1:C, :]
            kdec = (k[h] * jnp.exp(bl - b[h])).astype(BF16)
            st_ref[h] = st[h] * jnp.exp(bl) + _dg(v[h], kdec, TN)
        for h in heads:
            oh = o[h] + intra[h]
            gg = g_ref[rows, dv(h)].astype(F32)
            o_ref[rows, dv(h)] = (_rms(oh, ng_ref[...]) * (gg * _sigmoid(gg))).astype(BF16)


def _cast_slab(w2d, nsteps):
    rows, width = w2d.shape
    assert rows % nsteps == 0 and (rows // nsteps) % 16 == 0
    return rows // nsteps, width


def _gla(proj, la, ng, B, S, cb):
    T = B * S
    ncb = S // cb
    dkw = GLA_HEADS * GLA_DK
    row = lambda b, c: b * ncb + c
    return pl.pallas_call(
        functools.partial(_gla_kernel, cb=cb),
        grid=(B, ncb),
        in_specs=[
            pl.BlockSpec((cb, dkw), lambda b, c: (row(b, c), COL_GQ // dkw)),
            pl.BlockSpec((cb, dkw), lambda b, c: (row(b, c), COL_GK // dkw)),
            pl.BlockSpec((cb, GLA_WIDTH), lambda b, c: (row(b, c), COL_GV // GLA_WIDTH)),
            pl.BlockSpec((cb, GLA_WIDTH), lambda b, c: (row(b, c), COL_GG // GLA_WIDTH)),
            pl.BlockSpec((cb, dkw), lambda b, c: (row(b, c), 0)),
            pl.BlockSpec((1, GLA_DV), lambda b, c: (0, 0)),
        ],
        out_specs=pl.BlockSpec((cb, GLA_WIDTH), lambda b, c: (row(b, c), 0)),
        out_shape=jax.ShapeDtypeStruct((T, GLA_WIDTH), BF16),
        scratch_shapes=[pltpu.VMEM((GLA_HEADS, GLA_DV, GLA_DK), F32)],
        compiler_params=_params(("parallel", "arbitrary")),
    )(proj, proj, proj, proj, la, ng)


def _compress_kernel(hb_ref, pe_ref, w1_ref, w2_ref, o_ref, *, ncp):
    half = CMP_STRIDE * NSA_DH
    hb = hb_ref[0, 0].astype(F32)
    pe = pe_ref[0]
    top = (hb + pe[0:1]).astype(BF16)
    bot = (hb + pe[1:2]).astype(BF16)
    a = _dot(top, w1_ref[0, :half, :])
    bm = _dot(bot, w1_ref[0, half:, :])
    pre = a + pltpu.roll(bm, ncp - 1, 0)
    cdf = 0.5 * (1.0 + jnp.tanh(math.sqrt(2.0 / math.pi) * (pre + 0.044715 * (pre * pre * pre))))
    o_ref[0, 0] = _dot((pre * cdf).astype(BF16), w2_ref[0]).astype(BF16)


def _compress(hb, pe, w1, w2):
    _, BG, ncp, half = hb.shape
    return pl.pallas_call(
        functools.partial(_compress_kernel, ncp=ncp),
        grid=(2, BG),
        in_specs=[
            pl.BlockSpec((1, 1, ncp, half), lambda t, n: (t, n, 0, 0)),
            pl.BlockSpec((1, 2, half), lambda t, n: (t, 0, 0)),
            pl.BlockSpec((1, 2 * half, NSA_DH), lambda t, n: (t, 0, 0)),
            pl.BlockSpec((1, NSA_DH, NSA_DH), lambda t, n: (t, 0, 0)),
        ],
        out_specs=pl.BlockSpec((1, 1, ncp, NSA_DH), lambda t, n: (t, n, 0, 0)),
        out_shape=jax.ShapeDtypeStruct((2, BG, ncp, NSA_DH), BF16),
        compiler_params=_params(("parallel", "parallel")),
    )(hb, pe, w1, w2)


def _cmpattn_kernel(q_ref, kc_ref, vc_ref, bias_ref, ovt_ref, wf_ref, o_ref, sel_ref, wb_ref, *, ns, ncp):
    wb_ref[...] = wf_ref[...].astype(BF16)
    tq = ATT_T
    i = pl.program_id(1)
    kc = kc_ref[0, 0]
    vc = vc_ref[0, 0]
    pos = i * tq + lax.broadcasted_iota(I32, (tq, 1), 0)
    row_valid = jnp.where(pos >= CMP_LEN - 1, 1.0, 0.0)
    psum = jnp.zeros((tq, ncp), F32)
    outs = []
    for r in range(NSA_REP):
        q = q_ref[:, r * NSA_DH:(r + 1) * NSA_DH]
        s = _dg(q, kc, NT) + bias_ref[r]
        e = jnp.exp(s - jnp.max(s, axis=-1, keepdims=True))
        p = e / jnp.sum(e, axis=-1, keepdims=True) * row_valid
        psum = psum + p
        outs.append(_dot(p.astype(BF16), vc))
    o_ref[...] = jnp.concatenate(outs, axis=1).astype(BF16)

    ph = psum.astype(BF16)
    pl_ = (psum - ph.astype(F32)).astype(BF16)
    imp = _dg(ovt_ref[...], ph, NT) + _dg(ovt_ref[...], pl_, NT)
    n_i = lax.broadcasted_iota(I32, (ns, tq), 0)
    cur = (i * tq + lax.broadcasted_iota(I32, (ns, tq), 1)) // SEL_BLOCK
    forced = (n_i == 0) | (n_i == cur) | (n_i == cur - 1)
    score = jnp.where(forced, FORCE_SCORE, jnp.where(n_i <= cur, imp, -1.0))
    cnt = jnp.zeros((ns, tq), F32)
    for m in range(ns):
        sm = score[m:m + 1, :]
        ge = jnp.where(sm >= score, 1.0, 0.0)
        gt = jnp.where(sm > score, 1.0, 0.0)
        cnt = cnt + jnp.where(n_i > m, ge, gt)
    sel_t = jnp.where(cnt < min(SEL_TOPN, ns), 1.0, 0.0)
    if ns < LANE:
        sel_t = jnp.concatenate([sel_t, jnp.zeros((LANE - ns, tq), F32)], axis=0)
    sel_ref[0] = sel_t.T.astype(BF16)


def _cmpattn(proj, cmp_kv, bias_c, ovt, wf, B, S):
    T = B * S
    tq = ATT_T
    nq = S // tq
    ns = S // SEL_BLOCK
    ncp = S // CMP_STRIDE
    G = NSA_KV
    qcol = COL_NQ // (NSA_REP * NSA_DH)
    slab_r, slab_w = _cast_slab(wf, G * nq * B)
    slab = pl.BlockSpec((slab_r, slab_w), lambda g, i, b: ((g * nq + i) * B + b, 0))
    return pl.pallas_call(
        functools.partial(_cmpattn_kernel, ns=ns, ncp=ncp),
        grid=(G, nq, B),
        in_specs=[
            pl.BlockSpec((tq, NSA_REP * NSA_DH), lambda g, i, b: (b * nq + i, qcol + g)),
            pl.BlockSpec((1, 1, ncp, NSA_DH), lambda g, i, b: (0, b * G + g, 0, 0)),
            pl.BlockSpec((1, 1, ncp, NSA_DH), lambda g, i, b: (1, b * G + g, 0, 0)),
            pl.BlockSpec((NSA_REP, tq, ncp), lambda g, i, b: (g, i, 0)),
            pl.BlockSpec((ns, ncp), lambda g, i, b: (0, 0)),
            slab,
        ],
        out_specs=[
            pl.BlockSpec((tq, NSA_REP * NSA_DH), lambda g, i, b: (b * nq + i, g)),
            pl.BlockSpec((1, tq, LANE), lambda g, i, b: (b * G + g, i, 0)),
            slab,
        ],
        out_shape=[
            jax.ShapeDtypeStruct((T, NSA_WIDTH), BF16),
            jax.ShapeDtypeStruct((B * G, S, LANE), BF16),
            jax.ShapeDtypeStruct(wf.shape, BF16),
        ],
        compiler_params=_params(("parallel", "parallel", "parallel")),
    )(proj, cmp_kv, cmp_kv, bias_c, ovt, wf)


def _flash_update(s, v, m_ref, al_ref):
    reps = s.shape[1] // LANE
    m_old = m_ref[...]
    m_new = jnp.maximum(m_old, jnp.max(s, axis=-1, keepdims=True))
    alpha = jnp.exp(m_old - m_new)
    p = jnp.exp(s - jnp.concatenate([m_new] * reps, axis=1))
    v_ext = jnp.concatenate([v, jnp.ones(v.shape, BF16)], axis=1)
    al_ref[...] = jnp.concatenate([alpha, alpha], axis=1) * al_ref[...] + _dot(p.astype(BF16), v_ext)
    m_ref[...] = m_new


def _selwin_kernel(q_ref, ks_ref, vs_ref, kw_ref, vw_ref, sel_ref, d_ref, w4_ref, wf_ref,
                   osel_ref, owin_ref, wb_ref, m_ref, al_ref):
    wb_ref[...] = wf_ref[...].astype(BF16)
    t = SEL_T
    R = NSA_REP
    i = pl.program_id(2)
    q = jnp.concatenate([q_ref[:, r * NSA_DH:(r + 1) * NSA_DH] for r in range(R)], axis=0)
    sel = sel_ref[0]
    n_i = lax.broadcasted_iota(I32, (LANE, t), 0)
    c_blk = lax.broadcasted_iota(I32, (LANE, t), 1) // SEL_BLOCK
    blk_per_tile = t // SEL_BLOCK

    def init():
        m_ref[...] = jnp.full(m_ref.shape, NEG, F32)
        al_ref[...] = jnp.zeros(al_ref.shape, F32)

    def tile(ref, j):
        return ref[pl.ds(pl.multiple_of(j * t, t), t), :]

    def logits(k_ref, j, add):
        s = _dg(q, tile(k_ref, j), NT).reshape(R, t, t)
        return (s + add).reshape(R * t, t)

    def sel_mask(j):
        expand = jnp.where(n_i == blk_per_tile * j + c_blk, 1.0, 0.0).astype(BF16)
        return (_dot(sel, expand) - 1.0) * (-NEG)

    def finish(o_ref):
        al = al_ref[...]
        o = al[:, :NSA_DH] / al[:, NSA_DH:]
        o_ref[...] = jnp.concatenate([o[r * t:(r + 1) * t] for r in range(R)], axis=1).astype(BF16)

    init()

    def far(j, c):
        _flash_update(logits(ks_ref, j, sel_mask(j)[None]), tile(vs_ref, j), m_ref, al_ref)
        return c

    lax.fori_loop(0, jnp.maximum(i - 1, 0), far, 0)
    jn = jnp.maximum(i - 1, 0)
    off = jnp.where(i == 0, NEG, 0.0)
    _flash_update(logits(ks_ref, jn, (sel_mask(jn) + off)[None] + d_ref[:, 1]), tile(vs_ref, jn),
                  m_ref, al_ref)
    _flash_update(logits(ks_ref, i, sel_mask(i)[None] + d_ref[:, 0]), tile(vs_ref, i),
                  m_ref, al_ref)
    finish(osel_ref)

    init()
    pre = WINDOW // t
    for o in range(pre, -1, -1):
        jw = i - o
        jc = jnp.maximum(jw, 0)
        off = jnp.where(jw < 0, NEG, 0.0)
        if o == 0:
            add = d_ref[:, 0] + off
        elif o == 1:
            add = d_ref[:, 1] + off
        elif o == pre:
            add = (w4_ref[...] + off)[None]
        else:
            add = off
        _flash_update(logits(kw_ref, jc, add), tile(vw_ref, jc), m_ref, al_ref)
    finish(owin_ref)


def _selwin(proj, sel, dtiles, w4, wf, B, S):
    T = B * S
    t = SEL_T
    nq = S // t
    G = NSA_KV
    qcol = COL_NQ // (NSA_REP * NSA_DH)
    kv = lambda col: pl.BlockSpec((S, NSA_DH), lambda b, g, i, col=col: (b, col // NSA_DH + g))
    slab_r, slab_w = _cast_slab(wf, B * G * nq)
    slab = pl.BlockSpec((slab_r, slab_w), lambda b, g, i: ((b * G + g) * nq + i, 0))
    return pl.pallas_call(
        _selwin_kernel,
        grid=(B, G, nq),
        in_specs=[
            pl.BlockSpec((t, NSA_REP * NSA_DH), lambda b, g, i: (b * nq + i, qcol + g)),
            kv(COL_KS), kv(COL_VS), kv(COL_KW), kv(COL_VW),
            pl.BlockSpec((1, t, LANE), lambda b, g, i: (b * G + g, i, 0)),
            pl.BlockSpec((NSA_REP, 2, t, t), lambda b, g, i: (g, 0, 0, 0)),
            pl.BlockSpec((t, t), lambda b, g, i: (0, 0)),
            slab,
        ],
        out_specs=[
            pl.BlockSpec((t, NSA_REP * NSA_DH), lambda b, g, i: (b * nq + i, g)),
            pl.BlockSpec((t, NSA_REP * NSA_DH), lambda b, g, i: (b * nq + i, g)),
            slab,
        ],
        out_shape=[
            jax.ShapeDtypeStruct((T, NSA_WIDTH), BF16),
            jax.ShapeDtypeStruct((T, NSA_WIDTH), BF16),
            jax.ShapeDtypeStruct(wf.shape, BF16),
        ],
        scratch_shapes=[
            pltpu.VMEM((NSA_REP * t, LANE), F32),
            pltpu.VMEM((NSA_REP * t, 2 * NSA_DH), F32),
        ],
        compiler_params=_params(("parallel", "parallel", "parallel")),
    )(proj, proj, proj, proj, proj, sel, dtiles, w4, wf)


def _outproj_kernel(x_ref, og_ref, oc_ref, os_ref, ow_ref, nar_ref, ng_ref, wo_ref, fg_ref,
                    rwh_ref, rwl_ref, rb_ref, h1_ref, hn_ref, ti_ref, tw_ref):
    gates = _sigmoid(nar_ref[...])
    parts = []
    for h in range(NSA_HEADS):
        cs = slice(h * NSA_DH, (h + 1) * NSA_DH)
        g0 = gates[:, GATE_COL0 + h:GATE_COL0 + h + 1]
        g1 = gates[:, GATE_COL0 + NSA_HEADS + h:GATE_COL0 + NSA_HEADS + h + 1]
        g2 = gates[:, GATE_COL0 + 2 * NSA_HEADS + h:GATE_COL0 + 2 * NSA_HEADS + h + 1]
        parts.append(g0 * oc_ref[:, cs].astype(F32) + g1 * os_ref[:, cs].astype(F32)
                     + g2 * ow_ref[:, cs].astype(F32))
    on = _rms(jnp.concatenate(parts, axis=1), ng_ref[...]).astype(BF16)
    h1 = x_ref[...] + _dot(og_ref[...], wo_ref[:GLA_WIDTH, :]) + _dot(on, wo_ref[GLA_WIDTH:, :])
    h1_ref[...] = h1
    hn = _rms(h1, fg_ref[...])
    half = hn.shape[1] // 2
    _store_rows_tiled(hn_ref, 0, hn.shape[0], _pack_pairs(hn[:, :half], hn[:, half:]))

    hh = hn.astype(BF16)
    hl = (hn - hh.astype(F32)).astype(BF16)
    lg = _dot(hh, rwh_ref[...]) + _dot(hl, rwh_ref[...]) + _dot(hh, rwl_ref[...]) + rb_ref[...]
    lane = lax.broadcasted_iota(I32, lg.shape, 1)
    vals, idxs = [], []
    for _ in range(TOP_K):
        mx = jnp.max(lg, axis=-1, keepdims=True)
        ix = jnp.min(jnp.where(lg == mx, lane, LANE), axis=-1, keepdims=True)
        vals.append(mx)
        idxs.append(ix)
        lg = jnp.where(lane == ix, -3e38, lg)
    es = [jnp.exp(v - vals[0]) for v in vals]
    den = es[0] + es[1] + es[2] + es[3]
    ti = jnp.zeros(lg.shape, I32)
    tw = jnp.zeros(lg.shape, F32)
    for kk in range(TOP_K):
        ti = jnp.where(lane == kk, idxs[kk], ti)
        tw = jnp.where(lane == kk, es[kk] / den, tw)
    ti_ref[...] = ti
    tw_ref[...] = tw


def _outproj(x2, og, oc, osel, ow, nar, ng, wo, fg, rwh, rwl, rb, tm):
    T, D = x2.shape
    row = lambda w: pl.BlockSpec((tm, w), lambda i: (i, 0))
    full = lambda a: pl.BlockSpec(a.shape, lambda i: (0,) * a.ndim)
    return pl.pallas_call(
        _outproj_kernel,
        grid=(T // tm,),
        in_specs=[row(D), row(GLA_WIDTH), row(NSA_WIDTH), row(NSA_WIDTH), row(NSA_WIDTH), row(LANE),
                  full(ng), full(wo), full(fg), full(rwh), full(rwl), full(rb)],
        out_specs=[row(D), pl.BlockSpec((tm * SUBLANE, LANE), lambda i: (i, 0)), row(LANE), row(LANE)],
        out_shape=[
            jax.ShapeDtypeStruct((T, D), F32),
            jax.ShapeDtypeStruct((T * SUBLANE, LANE), U32),
            jax.ShapeDtypeStruct((T, LANE), I32),
            jax.ShapeDtypeStruct((T, LANE), F32),
        ],
        compiler_params=_params(("parallel",)),
    )(x2, og, oc, osel, ow, nar, ng, wo, fg, rwh, rwl, rb)


def _moe_kernel(te_ref, tv_ref, tok_hbm, dst_hbm, hn_hbm, wgt_ref, w1g_ref, w1l_ref, b1g_ref, b1l_ref,
                w2_ref, b2_ref, out_hbm, tok_smem, dst_smem, xbuf, xb_ref, act_ref, ybuf,
                sem_tok, sem_dst, sem_in, sem_out, *, tm, nf, n_tiles):
    i = pl.program_id(0)
    f = pl.program_id(1)
    slot = i % 2
    valid = tv_ref[i]
    sub = MOE_SUB
    nsub = tm // sub
    half = SUBLANE * LANE
    sl8 = SUBLANE

    def tile_of(r):
        return pl.ds(pl.multiple_of(r * sl8, sl8), sl8)

    def tok_copy(t, sl):
        return pltpu.make_async_copy(tok_hbm.at[t], tok_smem.at[sl], sem_tok.at[sl])

    def dst_copy():
        return pltpu.make_async_copy(dst_hbm.at[i], dst_smem, sem_dst)

    def issue_gather(t, sl):
        for s in range(nsub):
            @pl.when(s * sub < tv_ref[t])
            def _():
                def body(r, c):
                    pltpu.make_async_copy(hn_hbm.at[tile_of(tok_smem[sl, r]), :],
                                          xbuf.at[sl, tile_of(r), :], sem_in.at[sl, s]).start()
                    return c

                lax.fori_loop(s * sub, (s + 1) * sub, body, 0, unroll=8)

    def wait_gather(sl, s):
        pltpu.make_async_copy(hn_hbm.at[pl.ds(0, sub * sl8), :], xbuf.at[sl, pl.ds(s * sub * sl8, sub * sl8), :],
                              sem_in.at[sl, s]).wait()

    def wait_scatter(count):
        for s in range(nsub):
            @pl.when(s * sub < count)
            def _():
                pltpu.make_async_copy(ybuf.at[pl.ds(s * sub * sl8, sub * sl8), :],
                                      out_hbm.at[pl.ds(0, sub * sl8), :], sem_out).wait()

    @pl.when(f == 0)
    def _():
        @pl.when(i == 0)
        def _():
            cp = tok_copy(0, 0)
            cp.start()
            cp.wait()
            issue_gather(0, 0)

        @pl.when(i + 1 < n_tiles)
        def _():
            tok_copy(i + 1, 1 - slot).start()

        dst_copy().start()
        for s in range(nsub):
            @pl.when(s * sub < valid)
            def _():
                wait_gather(slot, s)
                for j, w in enumerate(_load_rows_tiled(xbuf, s * sub, sub, lead=(slot,))):
                    lo, hi = _unpack_pairs(w)
                    xb_ref[s * sub:(s + 1) * sub, j * LANE:(j + 1) * LANE] = lo.astype(BF16)
                    xb_ref[s * sub:(s + 1) * sub, half + j * LANE:half + (j + 1) * LANE] = hi.astype(BF16)

        @pl.when(i + 1 < n_tiles)
        def _():
            tok_copy(i + 1, 1 - slot).wait()
            issue_gather(i + 1, 1 - slot)

    for s in range(nsub):
        @pl.when(s * sub < valid)
        def _():
            rows = slice(s * sub, (s + 1) * sub)
            xb = xb_ref[rows, :]
            glu = jnp.minimum(_dot(xb, w1g_ref[0]) + b1g_ref[0], SWIGLU_LIMIT)
            lin = jnp.clip(_dot(xb, w1l_ref[0]) + b1l_ref[0], -SWIGLU_LIMIT, SWIGLU_LIMIT)
            act_ref[f, rows, :] = (glu * _sigmoid(SWIGLU_ALPHA * glu) * (lin + 1.0)).astype(BF16)

    @pl.when(f == nf - 1)
    def _():
        @pl.when(i > 0)
        def _():
            wait_scatter(tv_ref[jnp.maximum(i - 1, 0)])

        dst_copy().wait()
        for s in range(nsub):
            @pl.when(s * sub < valid)
            def _():
                rows = slice(s * sub, (s + 1) * sub)
                fc = act_ref.shape[2]
                y = _dot(act_ref[0, rows, :], w2_ref[0, :fc, :])
                for c in range(1, nf):
                    y = y + _dot(act_ref[c, rows, :], w2_ref[0, c * fc:(c + 1) * fc, :])
                y = (y + b2_ref[0]) * wgt_ref[0, rows, :]
                _store_rows_tiled(ybuf, s * sub, sub, _pack_pairs(y[:, :half], y[:, half:]))

                def body(r, c):
                    pltpu.make_async_copy(ybuf.at[tile_of(r), :], out_hbm.at[tile_of(dst_smem[r]), :],
                                          sem_out).start()
                    return c

                lax.fori_loop(s * sub, (s + 1) * sub, body, 0, unroll=8)

        @pl.when(i == n_tiles - 1)
        def _():
            wait_scatter(valid)
            ybuf[...] = jnp.zeros_like(ybuf)
            dump = pltpu.make_async_copy(ybuf, out_hbm.at[pl.ds(out_hbm.shape[0] - tm * sl8, tm * sl8), :], sem_out)
            dump.start()
            dump.wait()


def _moe(tile_exp, tile_valid, tok, dst, hn, wgt, w1, b1, w2, b2, n_out_rows, tm, fc):
    n_tiles = tok.shape[0]
    D = w1.shape[1]
    assert D == 2 * SUBLANE * LANE
    dff = w2.shape[1]
    nf = dff // fc
    grid_spec = pltpu.PrefetchScalarGridSpec(
        num_scalar_prefetch=2,
        grid=(n_tiles, nf),
        in_specs=[
            pl.BlockSpec(memory_space=pl.ANY),
            pl.BlockSpec(memory_space=pl.ANY),
            pl.BlockSpec(memory_space=pl.ANY),
            pl.BlockSpec((1, tm, 1), lambda i, f, te, tv: (i, 0, 0)),
            pl.BlockSpec((1, D, fc), lambda i, f, te, tv: (te[i], 0, f)),
            pl.BlockSpec((1, D, fc), lambda i, f, te, tv: (te[i], 0, nf + f)),
            pl.BlockSpec((1, 1, fc), lambda i, f, te, tv: (te[i], 0, f)),
            pl.BlockSpec((1, 1, fc), lambda i, f, te, tv: (te[i], 0, nf + f)),
            pl.BlockSpec((1, dff, D), lambda i, f, te, tv: (te[i], 0, 0)),
            pl.BlockSpec((1, 1, D), lambda i, f, te, tv: (te[i], 0, 0)),
        ],
        out_specs=pl.BlockSpec(memory_space=pl.ANY),
        scratch_shapes=[
            pltpu.SMEM((2, tm), I32),
            pltpu.SMEM((tm,), I32),
            pltpu.VMEM((2, tm * SUBLANE, LANE), U32),
            pltpu.VMEM((tm, D), BF16),
            pltpu.VMEM((nf, tm, fc), BF16),
            pltpu.VMEM((tm * SUBLANE, LANE), U32),
            pltpu.SemaphoreType.DMA((2,)),
            pltpu.SemaphoreType.DMA(()),
            pltpu.SemaphoreType.DMA((2, tm // MOE_SUB)),
            pltpu.SemaphoreType.DMA(()),
        ],
    )
    return pl.pallas_call(
        functools.partial(_moe_kernel, tm=tm, nf=nf, n_tiles=n_tiles),
        grid_spec=grid_spec,
        out_shape=jax.ShapeDtypeStruct((n_out_rows * SUBLANE, LANE), U32),
        compiler_params=_params(("arbitrary", "arbitrary")),
    )(tile_exp, tile_valid, tok, dst, hn, wgt, w1, w1, b1, b1, w2, b2)


def _combine_kernel(h1_ref, y0_ref, y1_ref, y2_ref, y3_ref, g_ref, o_ref):
    h = h1_ref[...]
    tm = h.shape[0]
    for y_ref in (y0_ref, y1_ref, y2_ref, y3_ref):
        pairs = [_unpack_pairs(w) for w in _load_rows_tiled(y_ref, 0, tm)]
        h = h + jnp.concatenate([p[0] for p in pairs] + [p[1] for p in pairs], axis=1)
    o_ref[...] = _rms(h, g_ref[...])


def _combine(ys, h1, g, tm):
    T, D = h1.shape
    nb = T // tm
    slab = lambda k: pl.BlockSpec((tm * SUBLANE, LANE), lambda i, k=k: (k * nb + i, 0))
    return pl.pallas_call(
        _combine_kernel,
        grid=(nb,),
        in_specs=[pl.BlockSpec((tm, D), lambda i: (i, 0))] + [slab(k) for k in range(TOP_K)]
        + [pl.BlockSpec((1, D), lambda i: (0, 0))],
        out_specs=pl.BlockSpec((tm, D), lambda i: (i, 0)),
        out_shape=jax.ShapeDtypeStruct((T, D), F32),
        compiler_params=_params(("parallel",)),
    )(h1, ys, ys, ys, ys, g)


def _t5_bucket(dist):
    n = jnp.maximum(dist, 0)
    exact = N_BUCKETS // 2
    scaled = jnp.log(jnp.maximum(n, 1).astype(F32) / exact) / math.log(MAX_DISTANCE / exact)
    large = exact + (scaled * (N_BUCKETS - exact)).astype(I32)
    return jnp.where(n < exact, n, jnp.minimum(large, N_BUCKETS - 1)).astype(I32)


def _bias_tables(rel_bias, S):
    shifted = (rel_bias - rel_bias[N_BUCKETS - 1][None, :]).astype(F32)

    def lookup(dist):
        bucket = _t5_bucket(dist)
        out = jnp.zeros((NSA_HEADS,) + dist.shape, F32)
        for kb in range(N_BUCKETS - 1):
            out = jnp.where((bucket == kb)[None], shifted[kb].reshape((NSA_HEADS,) + (1,) * dist.ndim), out)
        return jnp.where((dist >= 0)[None], out, NEG)

    t = SEL_T
    r = jnp.arange(t)[:, None]
    c = jnp.arange(t)[None, :]
    dtiles = jnp.stack([lookup(r - c), lookup(t + r - c)], axis=1)
    w4 = jnp.where(r < c, 0.0, NEG).astype(F32)
    ncp = S // CMP_STRIDE
    dist_c = jnp.arange(S)[:, None] - (jnp.arange(ncp) * CMP_STRIDE + CMP_LEN - 1)[None, :]
    return dtiles, w4, lookup(dist_c)


def _overlap_t(S):
    ncp = S // CMP_STRIDE
    ns = S // SEL_BLOCK
    cs = jnp.arange(ncp)[None, :] * CMP_STRIDE
    ss = jnp.arange(ns)[:, None] * SEL_BLOCK
    ov = (cs < ss + SEL_BLOCK) & (cs + CMP_LEN > ss) & (jnp.arange(ncp)[None, :] < ncp - 1)
    return ov.astype(BF16)


def _route(top_idx, top_w, n_exp, tm):
    T = top_idx.shape[0]
    A = T * TOP_K
    e_flat = top_idx.reshape(A)
    w_flat = top_w.reshape(A)
    e_sorted, order = lax.sort_key_val(e_flat, jnp.arange(A, dtype=I32), is_stable=True)
    counts = jnp.sum((e_flat[:, None] == jnp.arange(n_exp, dtype=I32)[None, :]).astype(I32), axis=0)
    padded = (counts + tm - 1) // tm * tm
    start = jnp.cumsum(counts) - counts
    pend = jnp.cumsum(padded)
    pstart = pend - padded
    n_tiles = A // tm + n_exp
    tile_row0 = jnp.arange(n_tiles, dtype=I32) * tm
    tile_exp = jnp.minimum(jnp.sum((pend[None, :] <= tile_row0[:, None]).astype(I32), axis=1), n_exp - 1)
    tile_valid = jnp.clip(pstart[tile_exp] + counts[tile_exp] - tile_row0, 0, tm).astype(I32)
    off = tile_row0[:, None] - pstart[tile_exp][:, None] + jnp.arange(tm, dtype=I32)[None, :]
    live = jnp.arange(tm, dtype=I32)[None, :] < tile_valid[:, None]
    src = order[jnp.clip(start[tile_exp][:, None] + off, 0, A - 1)]
    buf_tok = jnp.where(live, src // TOP_K, 0)
    buf_w = jnp.where(live, w_flat[src], 0.0)
    buf_dst = jnp.where(live, (src % TOP_K) * T + src // TOP_K, TOP_K * T + jnp.arange(tm, dtype=I32)[None, :])
    return buf_tok, buf_dst, buf_w[:, :, None], tile_exp, tile_valid


def kernel(x, norm_mix_g, w_in, gla_wa2, gla_ba, gla_norm_g, cmp_pe_k, cmp_wk1, cmp_wk2, cmp_pe_v,
           cmp_wv1, cmp_wv2, nsa_norm_g, rel_bias, w_out, norm_ffn_g, router_w, router_b, exp_w1,
           exp_b1, exp_w2, exp_b2, norm_final_g):
    B, S, D = x.shape
    T = B * S
    G = NSA_KV
    n_exp = router_w.shape[-1]
    assert w_in.shape[0] == 1 and S % 1024 == 0 and S // SEL_BLOCK <= LANE and n_exp <= LANE
    x2 = x.reshape(T, D)

    w = w_in[0]
    a0 = COL_NQ
    n0 = a0 + GLA_RANK
    g0 = n0 + (N_MAIN - COL_NQ)
    w_main = jnp.concatenate([
        w[:, :COL_GK] * (GLA_DK ** -0.5), w[:, COL_GK:a0],
        w[:, n0:n0 + NSA_WIDTH] * (NSA_DH ** -0.5), w[:, n0 + NSA_WIDTH:g0]], axis=1).astype(BF16)
    w_nar = jnp.concatenate([w[:, a0:n0], w[:, g0:], jnp.zeros((D, LANE - GLA_RANK - 3 * NSA_HEADS), F32)],
                            axis=1).astype(BF16)
    wa2p = jnp.concatenate([gla_wa2[0], jnp.zeros((LANE - GLA_RANK, GLA_HEADS * GLA_DK), F32)], axis=0).astype(BF16)

    tm1 = min(1024, T)
    proj, log_a, nar = _inproj(x2, norm_mix_g, w_main, w_nar, wa2p, gla_ba, tm1, 512)

    e_w1, e_w2 = exp_w1[0], exp_w2[0]
    o_gla = _gla(proj, log_a, gla_norm_g, B, S, min(256, S))

    ncp = S // CMP_STRIDE
    half = CMP_STRIDE * NSA_DH

    def stride_groups(col):
        t4 = proj[:, col:col + KV_W].reshape(B, ncp, CMP_STRIDE, G, NSA_DH)
        return t4.transpose(0, 3, 1, 2, 4).reshape(B * G, ncp, half)

    hb = jnp.stack([stride_groups(COL_KC), stride_groups(COL_VC)], axis=0)
    pe = jnp.stack([cmp_pe_k[0].reshape(2, half), cmp_pe_v[0].reshape(2, half)], axis=0)
    w1c = jnp.stack([cmp_wk1[0], cmp_wv1[0]], axis=0).astype(BF16)
    w2c = jnp.stack([cmp_wk2[0], cmp_wv2[0]], axis=0).astype(BF16)
    cmp_kv = _compress(hb, pe, w1c, w2c)

    dtiles, w4, bias_c = _bias_tables(rel_bias, S)
    o_cmp, sel, w1b = _cmpattn(proj, cmp_kv, bias_c, _overlap_t(S), e_w1.reshape(-1, e_w1.shape[-1]), B, S)
    w1b = w1b.reshape(e_w1.shape)
    o_sel, o_win, w2b = _selwin(proj, sel, dtiles, w4, e_w2.reshape(-1, e_w2.shape[-1]), B, S)
    w2b = w2b.reshape(e_w2.shape)

    rw = jnp.concatenate([router_w[0], jnp.zeros((D, LANE - n_exp), F32)], axis=1)
    rwh = rw.astype(BF16)
    rwl = (rw - rwh.astype(F32)).astype(BF16)
    rb = jnp.concatenate([router_b[0], jnp.full((LANE - n_exp,), NEG, F32)])[None, :]
    h1, hn, top_i, top_w = _outproj(x2, o_gla, o_cmp, o_sel, o_win, nar, nsa_norm_g, w_out[0].astype(BF16),
                                    norm_ffn_g, rwh, rwl, rb, min(256, T))

    tm6 = min(1024, T)
    tok, dst, wgt, tile_exp, tile_valid = _route(top_i[:, :TOP_K], top_w[:, :TOP_K], n_exp, tm6)
    ys = _moe(tile_exp, tile_valid, tok, dst, hn, wgt, w1b, exp_b1[0][:, None, :],
              w2b, exp_b2[0][:, None, :], TOP_K * T + tm6, tm6, 512)
    out = _combine(ys, h1, norm_final_g[None, :], min(512, T))
    return out.reshape(B, S, D)
```

```python
import functools
import math

import numpy as np
import jax
import jax.numpy as jnp
from jax import lax
from jax.experimental import pallas as pl
from jax.experimental.pallas import tpu as pltpu

F32 = jnp.float32
BF16 = jnp.bfloat16
I32 = jnp.int32
U32 = jnp.uint32

GLA_HEADS = 4
GLA_DK = 128
GLA_DV = 256
GLA_RANK = 16
GLA_TAU = 16.0
NSA_HEADS = 8
NSA_KV = 2
NSA_REP = NSA_HEADS // NSA_KV
NSA_DH = 128
CMP_LEN = 32
CMP_STRIDE = 16
SEL_BLOCK = 64
SEL_TOPN = 16
WINDOW = 512
N_BUCKETS = 32
MAX_DISTANCE = 128
TOP_K = 4
SWIGLU_ALPHA = 1.702
SWIGLU_LIMIT = 7.0
RMS_EPS = 1e-5
NEG = -1e30
FORCE_SCORE = 1e4

GLA_WIDTH = GLA_HEADS * GLA_DV
NSA_WIDTH = NSA_HEADS * NSA_DH
KV_W = NSA_KV * NSA_DH

COL_GQ = 0
COL_GK = COL_GQ + GLA_HEADS * GLA_DK
COL_GV = COL_GK + GLA_HEADS * GLA_DK
COL_GG = COL_GV + GLA_WIDTH
COL_NQ = COL_GG + GLA_WIDTH
COL_KC = COL_NQ + NSA_WIDTH
COL_VC = COL_KC + KV_W
COL_KS = COL_VC + KV_W
COL_VS = COL_KS + KV_W
COL_KW = COL_VS + KV_W
COL_VW = COL_KW + KV_W
N_MAIN = COL_VW + KV_W
LANE = 128
ATT_T = 128
SEL_T = 256
GLA_C = 64
GLA_SUB = 16
MOE_SUB = 512
DMA_UNROLL = 32
GATE_COL0 = GLA_RANK

VMEM_LIMIT = 56 * 1024 * 1024

NT = (((1,), (1,)), ((), ()))
TN = (((0,), (0,)), ((), ()))


def _dot(a, b):
    return jnp.dot(a, b, preferred_element_type=F32)


def _dg(a, b, dims):
    return lax.dot_general(a, b, dims, preferred_element_type=F32)


def _sigmoid(x):
    return 1.0 / (1.0 + jnp.exp(-x))


def _log_sigmoid(z):
    return -(jnp.maximum(-z, 0.0) + jnp.log(1.0 + jnp.exp(-jnp.abs(z))))


def _rms(x, g):
    return x * lax.rsqrt(jnp.mean(x * x, axis=-1, keepdims=True) + RMS_EPS) * g


def _split3(x):
    hi = x.astype(BF16)
    r1 = x - hi.astype(F32)
    mid = r1.astype(BF16)
    lo = (r1 - mid.astype(F32)).astype(BF16)
    return hi, mid, lo


def _pack_pairs(lo, hi):
    lo_bits = lax.bitcast_convert_type(lo.astype(BF16).astype(F32), U32) >> 16
    hi_bits = lax.bitcast_convert_type(hi.astype(BF16).astype(F32), U32) & jnp.uint32(0xFFFF0000)
    return lo_bits | hi_bits


def _unpack_pairs(w):
    lo = lax.bitcast_convert_type(w << 16, F32)
    hi = lax.bitcast_convert_type(w & jnp.uint32(0xFFFF0000), F32)
    return lo, hi


SUBLANE = 8


def _store_rows_tiled(ref, base, n, packed):
    for j in range(SUBLANE):
        ref[pl.ds(base * SUBLANE + j, n, stride=SUBLANE), :] = packed[:, j * LANE:(j + 1) * LANE]


def _load_rows_tiled(ref, base, n, lead=()):
    return [ref[lead + (pl.ds(base * SUBLANE + j, n, stride=SUBLANE), slice(None))] for j in range(SUBLANE)]


def _params(sem, vmem=VMEM_LIMIT):
    return pltpu.CompilerParams(dimension_semantics=sem, vmem_limit_bytes=vmem)


def _inproj_kernel(x_ref, g_ref, w_ref, wn_ref, wa2_ref, ba_ref, proj_ref, la_ref, nar_ref, xn_ref):
    @pl.when(pl.program_id(1) == 0)
    def _():
        xn = _rms(x_ref[...], g_ref[...]).astype(BF16)
        xn_ref[...] = xn
        nar = _dot(xn, wn_ref[...])
        nar_ref[...] = nar
        z = _dot(nar.astype(BF16), wa2_ref[...]) + ba_ref[...]
        la_ref[...] = _log_sigmoid(z) * (1.0 / GLA_TAU)

    proj_ref[...] = _dot(xn_ref[...], w_ref[...]).astype(BF16)


def _inproj(x2, g, w_main, w_nar, wa2p, ba, tm, tn):
    T, D = x2.shape
    nj = N_MAIN // tn
    return pl.pallas_call(
        _inproj_kernel,
        grid=(T // tm, nj),
        in_specs=[
            pl.BlockSpec((tm, D), lambda i, j: (i, 0)),
            pl.BlockSpec((1, D), lambda i, j: (0, 0)),
            pl.BlockSpec((D, tn), lambda i, j: (0, j)),
            pl.BlockSpec((D, LANE), lambda i, j: (0, 0)),
            pl.BlockSpec((LANE, GLA_HEADS * GLA_DK), lambda i, j: (0, 0)),
            pl.BlockSpec((1, GLA_HEADS * GLA_DK), lambda i, j: (0, 0)),
        ],
        out_specs=[
            pl.BlockSpec((tm, tn), lambda i, j: (i, j)),
            pl.BlockSpec((tm, GLA_HEADS * GLA_DK), lambda i, j: (i, 0)),
            pl.BlockSpec((tm, LANE), lambda i, j: (i, 0)),
        ],
        out_shape=[
            jax.ShapeDtypeStruct((T, N_MAIN), BF16),
            jax.ShapeDtypeStruct((T, GLA_HEADS * GLA_DK), F32),
            jax.ShapeDtypeStruct((T, LANE), F32),
        ],
        scratch_shapes=[pltpu.VMEM((tm, D), BF16)],
        compiler_params=_params(("parallel", "arbitrary")),
    )(x2, g, w_main, w_nar, wa2p, ba)


def _gla_kernel(q_ref, k_ref, v_ref, g_ref, la_ref, ng_ref, o_ref, st_ref, *, cb):
    @pl.when(pl.program_id(1) == 0)
    def _():
        st_ref[...] = jnp.zeros_like(st_ref)

    C, SUB = GLA_C, GLA_SUB
    heads = range(GLA_HEADS)
    dk = lambda h: slice(h * GLA_DK, (h + 1) * GLA_DK)
    dv = lambda h: slice(h * GLA_DV, (h + 1) * GLA_DV)
    ri = lax.broadcasted_iota(I32, (C, C), 0)
    ci = lax.broadcasted_iota(I32, (C, C), 1)
    tri = jnp.where(ri >= ci, 1.0, 0.0).astype(BF16)
    nsub = C // SUB
    sub_of_row = lax.broadcasted_iota(I32, (C, GLA_DK), 0) // SUB
    for n in range(cb // C):
        rows = slice(n * C, (n + 1) * C)
        b = []
        for h in heads:
            hi, mid, lo = _split3(la_ref[rows, dk(h)])
            b.append(_dot(tri, hi) + _dot(tri, mid) + _dot(tri, lo))
        q = [q_ref[rows, dk(h)].astype(F32) for h in heads]
        k = [k_ref[rows, dk(h)].astype(F32) for h in heads]
        v = [v_ref[rows, dv(h)] for h in heads]
        st = [st_ref[h] for h in heads]
        o = [_dg((q[h] * jnp.exp(b[h])).astype(BF16), st[h].astype(BF16), NT) for h in heads]
        intra = []
        for h in heads:
            refs = [jnp.zeros((1, GLA_DK), F32)] + [b[h][s * SUB - 1:s * SUB, :] for s in range(1, nsub)]
            ref_row = jnp.concatenate([jnp.broadcast_to(r, (SUB, GLA_DK)) for r in refs], axis=0)
            qsc = q[h] * jnp.exp(b[h] - ref_row)
            q_cat = jnp.concatenate([jnp.where(sub_of_row == s, qsc, 0.0) for s in range(nsub)], axis=1)
            k_cat = jnp.concatenate([jnp.where(sub_of_row <= s, k[h] * jnp.exp(refs[s] - b[h]), 0.0)
                                     for s in range(nsub)], axis=1)
            sc = jnp.where(ci <= ri, _dg(q_cat.astype(BF16), k_cat.astype(BF16), NT), 0.0)
            intra.append(_dot(sc.astype(BF16), v[h]))
        for h in heads:
            bl = b[h][C - 1:C, :]
            kdec = (k[h] * jnp.exp(bl - b[h])).astype(BF16)
            st_ref[h] = st[h] * jnp.exp(bl) + _dg(v[h], kdec, TN)
        for h in heads:
            oh = o[h] + intra[h]
            gg = g_ref[rows, dv(h)].astype(F32)
            o_ref[rows, dv(h)] = (_rms(oh, ng_ref[...]) * (gg * _sigmoid(gg))).astype(BF16)


def _cast_slab(w2d, nsteps):
    rows, width = w2d.shape
    assert rows % nsteps == 0 and (rows // nsteps) % 16 == 0
    return rows // nsteps, width


def _gla(proj, la, ng, B, S, cb):
    T = B * S
    ncb = S // cb
    dkw = GLA_HEADS * GLA_DK
    row = lambda b, c: b * ncb + c
    return pl.pallas_call(
        functools.partial(_gla_kernel, cb=cb),
        grid=(B, ncb),
        in_specs=[
            pl.BlockSpec((cb, dkw), lambda b, c: (row(b, c), COL_GQ // dkw)),
            pl.BlockSpec((cb, dkw), lambda b, c: (row(b, c), COL_GK // dkw)),
            pl.BlockSpec((cb, GLA_WIDTH), lambda b, c: (row(b, c), COL_GV // GLA_WIDTH)),
            pl.BlockSpec((cb, GLA_WIDTH), lambda b, c: (row(b, c), COL_GG // GLA_WIDTH)),
            pl.BlockSpec((cb, dkw), lambda b, c: (row(b, c), 0)),
            pl.BlockSpec((1, GLA_DV), lambda b, c: (0, 0)),
        ],
        out_specs=pl.BlockSpec((cb, GLA_WIDTH), lambda b, c: (row(b, c), 0)),
        out_shape=jax.ShapeDtypeStruct((T, GLA_WIDTH), BF16),
        scratch_shapes=[pltpu.VMEM((GLA_HEADS, GLA_DV, GLA_DK), F32)],
        compiler_params=_params(("parallel", "arbitrary")),
    )(proj, proj, proj, proj, la, ng)


def _compress_kernel(hb_ref, pe_ref, w1_ref, w2_ref, o_ref, *, ncp):
    half = CMP_STRIDE * NSA_DH
    hb = hb_ref[0, 0].astype(F32)
    pe = pe_ref[0]
    top = (hb + pe[0:1]).astype(BF16)
    bot = (hb + pe[1:2]).astype(BF16)
    a = _dot(top, w1_ref[0, :half, :])
    bm = _dot(bot, w1_ref[0, half:, :])
    pre = a + pltpu.roll(bm, ncp - 1, 0)
    cdf = 0.5 * (1.0 + jnp.tanh(math.sqrt(2.0 / math.pi) * (pre + 0.044715 * (pre * pre * pre))))
    o_ref[0, 0] = _dot((pre * cdf).astype(BF16), w2_ref[0]).astype(BF16)


def _compress(hb, pe, w1, w2):
    _, BG, ncp, half = hb.shape
    return pl.pallas_call(
        functools.partial(_compress_kernel, ncp=ncp),
        grid=(2, BG),
        in_specs=[
            pl.BlockSpec((1, 1, ncp, half), lambda t, n: (t, n, 0, 0)),
            pl.BlockSpec((1, 2, half), lambda t, n: (t, 0, 0)),
            pl.BlockSpec((1, 2 * half, NSA_DH), lambda t, n: (t, 0, 0)),
            pl.BlockSpec((1, NSA_DH, NSA_DH), lambda t, n: (t, 0, 0)),
        ],
        out_specs=pl.BlockSpec((1, 1, ncp, NSA_DH), lambda t, n: (t, n, 0, 0)),
        out_shape=jax.ShapeDtypeStruct((2, BG, ncp, NSA_DH), BF16),
        compiler_params=_params(("parallel", "parallel")),
    )(hb, pe, w1, w2)


def _cmpattn_kernel(q_ref, kc_ref, vc_ref, bias_ref, ovt_ref, o_ref, sel_ref, *, ns, ncp):
    tq = ATT_T
    i = pl.program_id(1)
    kc = kc_ref[0, 0]
    vc = vc_ref[0, 0]
    pos = i * tq + lax.broadcasted_iota(I32, (tq, 1), 0)
    row_valid = jnp.where(pos >= CMP_LEN - 1, 1.0, 0.0)
    psum = jnp.zeros((tq, ncp), F32)
    outs = []
    for r in range(NSA_REP):
        q = q_ref[:, r * NSA_DH:(r + 1) * NSA_DH]
        s = _dg(q, kc, NT) + bias_ref[r]
        e = jnp.exp(s - jnp.max(s, axis=-1, keepdims=True))
        p = e / jnp.sum(e, axis=-1, keepdims=True) * row_valid
        psum = psum + p
        outs.append(_dot(p.astype(BF16), vc))
    o_ref[...] = jnp.concatenate(outs, axis=1).astype(BF16)

    ph = psum.astype(BF16)
    pl_ = (psum - ph.astype(F32)).astype(BF16)
    imp = _dg(ovt_ref[...], ph, NT) + _dg(ovt_ref[...], pl_, NT)
    n_i = lax.broadcasted_iota(I32, (ns, tq), 0)
    cur = (i * tq + lax.broadcasted_iota(I32, (ns, tq), 1)) // SEL_BLOCK
    forced = (n_i == 0) | (n_i == cur) | (n_i == cur - 1)
    score = jnp.where(forced, FORCE_SCORE, jnp.where(n_i <= cur, imp, -1.0))
    cnt = jnp.zeros((ns, tq), F32)
    for m in range(ns):
        sm = score[m:m + 1, :]
        ge = jnp.where(sm >= score, 1.0, 0.0)
        gt = jnp.where(sm > score, 1.0, 0.0)
        cnt = cnt + jnp.where(n_i > m, ge, gt)
    sel_t = jnp.where(cnt < min(SEL_TOPN, ns), 1.0, 0.0)
    if ns < LANE:
        sel_t = jnp.concatenate([sel_t, jnp.zeros((LANE - ns, tq), F32)], axis=0)
    sel_ref[0] = sel_t.T.astype(BF16)


def _cmpattn(proj, cmp_kv, bias_c, ovt, B, S):
    T = B * S
    tq = ATT_T
    nq = S // tq
    ns = S // SEL_BLOCK
    ncp = S // CMP_STRIDE
    G = NSA_KV
    qcol = COL_NQ // (NSA_REP * NSA_DH)
    return pl.pallas_call(
        functools.partial(_cmpattn_kernel, ns=ns, ncp=ncp),
        grid=(G, nq, B),
        in_specs=[
            pl.BlockSpec((tq, NSA_REP * NSA_DH), lambda g, i, b: (b * nq + i, qcol + g)),
            pl.BlockSpec((1, 1, ncp, NSA_DH), lambda g, i, b: (0, b * G + g, 0, 0)),
            pl.BlockSpec((1, 1, ncp, NSA_DH), lambda g, i, b: (1, b * G + g, 0, 0)),
            pl.BlockSpec((NSA_REP, tq, ncp), lambda g, i, b: (g, i, 0)),
            pl.BlockSpec((ns, ncp), lambda g, i, b: (0, 0)),
        ],
        out_specs=[
            pl.BlockSpec((tq, NSA_REP * NSA_DH), lambda g, i, b: (b * nq + i, g)),
            pl.BlockSpec((1, tq, LANE), lambda g, i, b: (b * G + g, i, 0)),
        ],
        out_shape=[
            jax.ShapeDtypeStruct((T, NSA_WIDTH), BF16),
            jax.ShapeDtypeStruct((B * G, S, LANE), BF16),
        ],
        compiler_params=_params(("parallel", "parallel", "parallel")),
    )(proj, cmp_kv, cmp_kv, bias_c, ovt)


def _flash_update(s, v, m_ref, al_ref):
    reps = s.shape[1] // LANE
    m_old = m_ref[...]
    m_new = jnp.maximum(m_old, jnp.max(s, axis=-1, keepdims=True))
    alpha = jnp.exp(m_old - m_new)
    p = jnp.exp(s - jnp.concatenate([m_new] * reps, axis=1))
    v_ext = jnp.concatenate([v, jnp.ones(v.shape, BF16)], axis=1)
    al_ref[...] = jnp.concatenate([alpha, alpha], axis=1) * al_ref[...] + _dot(p.astype(BF16), v_ext)
    m_ref[...] = m_new


def _selwin_kernel(q_ref, ks_ref, vs_ref, kw_ref, vw_ref, sel_ref, d_ref, w4_ref, wf1_ref, wf2_ref,
                   osel_ref, owin_ref, wb1_ref, wb2_ref, m_ref, al_ref):
    wb1_ref[...] = wf1_ref[...].astype(BF16)
    wb2_ref[...] = wf2_ref[...].astype(BF16)
    t = SEL_T
    R = NSA_REP
    i = pl.program_id(2)
    q = jnp.concatenate([q_ref[:, r * NSA_DH:(r + 1) * NSA_DH] for r in range(R)], axis=0)
    sel = sel_ref[0]
    n_i = lax.broadcasted_iota(I32, (LANE, t), 0)
    c_blk = lax.broadcasted_iota(I32, (LANE, t), 1) // SEL_BLOCK
    blk_per_tile = t // SEL_BLOCK

    def init():
        m_ref[...] = jnp.full(m_ref.shape, NEG, F32)
        al_ref[...] = jnp.zeros(al_ref.shape, F32)

    def tile(ref, j):
        return ref[pl.ds(pl.multiple_of(j * t, t), t), :]

    def logits(k_ref, j, add):
        s = _dg(q, tile(k_ref, j), NT).reshape(R, t, t)
        return (s + add).reshape(R * t, t)

    def sel_mask(j):
        expand = jnp.where(n_i == blk_per_tile * j + c_blk, 1.0, 0.0).astype(BF16)
        return (_dot(sel, expand) - 1.0) * (-NEG)

    def finish(o_ref):
        al = al_ref[...]
        o = al[:, :NSA_DH] / al[:, NSA_DH:]
        o_ref[...] = jnp.concatenate([o[r * t:(r + 1) * t] for r in range(R)], axis=1).astype(BF16)

    init()

    def far(j, c):
        _flash_update(logits(ks_ref, j, sel_mask(j)[None]), tile(vs_ref, j), m_ref, al_ref)
        return c

    lax.fori_loop(0, jnp.maximum(i - 1, 0), far, 0)
    jn = jnp.maximum(i - 1, 0)
    off = jnp.where(i == 0, NEG, 0.0)
    _flash_update(logits(ks_ref, jn, (sel_mask(jn) + off)[None] + d_ref[:, 1]), tile(vs_ref, jn),
                  m_ref, al_ref)
    _flash_update(logits(ks_ref, i, sel_mask(i)[None] + d_ref[:, 0]), tile(vs_ref, i),
                  m_ref, al_ref)
    finish(osel_ref)

    init()
    pre = WINDOW // t
    for o in range(pre, -1, -1):
        jw = i - o
        jc = jnp.maximum(jw, 0)
        off = jnp.where(jw < 0, NEG, 0.0)
        if o == 0:
            add = d_ref[:, 0] + off
        elif o == 1:
            add = d_ref[:, 1] + off
        elif o == pre:
            add = (w4_ref[...] + off)[None]
        else:
            add = off
        _flash_update(logits(kw_ref, jc, add), tile(vw_ref, jc), m_ref, al_ref)
    finish(owin_ref)


def _selwin(proj, sel, dtiles, w4, wf1, wf2, B, S):
    T = B * S
    t = SEL_T
    nq = S // t
    G = NSA_KV
    qcol = COL_NQ // (NSA_REP * NSA_DH)
    kv = lambda col: pl.BlockSpec((S, NSA_DH), lambda b, g, i, col=col: (b, col // NSA_DH + g))
    slab = lambda wf: pl.BlockSpec(_cast_slab(wf, B * G * nq), lambda b, g, i: ((b * G + g) * nq + i, 0))
    return pl.pallas_call(
        _selwin_kernel,
        grid=(B, G, nq),
        in_specs=[
            pl.BlockSpec((t, NSA_REP * NSA_DH), lambda b, g, i: (b * nq + i, qcol + g)),
            kv(COL_KS), kv(COL_VS), kv(COL_KW), kv(COL_VW),
            pl.BlockSpec((1, t, LANE), lambda b, g, i: (b * G + g, i, 0)),
            pl.BlockSpec((NSA_REP, 2, t, t), lambda b, g, i: (g, 0, 0, 0)),
            pl.BlockSpec((t, t), lambda b, g, i: (0, 0)),
            slab(wf1), slab(wf2),
        ],
        out_specs=[
            pl.BlockSpec((t, NSA_REP * NSA_DH), lambda b, g, i: (b * nq + i, g)),
            pl.BlockSpec((t, NSA_REP * NSA_DH), lambda b, g, i: (b * nq + i, g)),
            slab(wf1), slab(wf2),
        ],
        out_shape=[
            jax.ShapeDtypeStruct((T, NSA_WIDTH), BF16),
            jax.ShapeDtypeStruct((T, NSA_WIDTH), BF16),
            jax.ShapeDtypeStruct(wf1.shape, BF16),
            jax.ShapeDtypeStruct(wf2.shape, BF16),
        ],
        scratch_shapes=[
            pltpu.VMEM((NSA_REP * t, LANE), F32),
            pltpu.VMEM((NSA_REP * t, 2 * NSA_DH), F32),
        ],
        compiler_params=_params(("parallel", "parallel", "parallel")),
    )(proj, proj, proj, proj, proj, sel, dtiles, w4, wf1, wf2)


def _outproj_kernel(x_ref, og_ref, oc_ref, os_ref, ow_ref, nar_ref, ng_ref, wo_ref, fg_ref,
                    rwh_ref, rwl_ref, rb_ref, h1_ref, hn_ref, ti_ref, tw_ref):
    gates = _sigmoid(nar_ref[...])
    parts = []
    for h in range(NSA_HEADS):
        cs = slice(h * NSA_DH, (h + 1) * NSA_DH)
        g0 = gates[:, GATE_COL0 + h:GATE_COL0 + h + 1]
        g1 = gates[:, GATE_COL0 + NSA_HEADS + h:GATE_COL0 + NSA_HEADS + h + 1]
        g2 = gates[:, GATE_COL0 + 2 * NSA_HEADS + h:GATE_COL0 + 2 * NSA_HEADS + h + 1]
        parts.append(g0 * oc_ref[:, cs].astype(F32) + g1 * os_ref[:, cs].astype(F32)
                     + g2 * ow_ref[:, cs].astype(F32))
    on = _rms(jnp.concatenate(parts, axis=1), ng_ref[...]).astype(BF16)
    h1 = x_ref[...] + _dot(og_ref[...], wo_ref[:GLA_WIDTH, :]) + _dot(on, wo_ref[GLA_WIDTH:, :])
    h1_ref[...] = h1
    hn = _rms(h1, fg_ref[...])
    half = hn.shape[1] // 2
    _store_rows_tiled(hn_ref, 0, hn.shape[0], _pack_pairs(hn[:, :half], hn[:, half:]))

    hh = hn.astype(BF16)
    hl = (hn - hh.astype(F32)).astype(BF16)
    lg = _dot(hh, rwh_ref[...]) + _dot(hl, rwh_ref[...]) + _dot(hh, rwl_ref[...]) + rb_ref[...]
    lane = lax.broadcasted_iota(I32, lg.shape, 1)
    vals, idxs = [], []
    for _ in range(TOP_K):
        mx = jnp.max(lg, axis=-1, keepdims=True)
        ix = jnp.min(jnp.where(lg == mx, lane, LANE), axis=-1, keepdims=True)
        vals.append(mx)
        idxs.append(ix)
        lg = jnp.where(lane == ix, -3e38, lg)
    es = [jnp.exp(v - vals[0]) for v in vals]
    den = es[0] + es[1] + es[2] + es[3]
    ti = jnp.zeros(lg.shape, I32)
    tw = jnp.zeros(lg.shape, F32)
    for kk in range(TOP_K):
        ti = jnp.where(lane == kk, idxs[kk], ti)
        tw = jnp.where(lane == kk, es[kk] / den, tw)
    ti_ref[...] = ti
    tw_ref[...] = tw


def _outproj(x2, og, oc, osel, ow, nar, ng, wo, fg, rwh, rwl, rb, tm):
    T, D = x2.shape
    row = lambda w: pl.BlockSpec((tm, w), lambda i: (i, 0))
    full = lambda a: pl.BlockSpec(a.shape, lambda i: (0,) * a.ndim)
    return pl.pallas_call(
        _outproj_kernel,
        grid=(T // tm,),
        in_specs=[row(D), row(GLA_WIDTH), row(NSA_WIDTH), row(NSA_WIDTH), row(NSA_WIDTH), row(LANE),
                  full(ng), full(wo), full(fg), full(rwh), full(rwl), full(rb)],
        out_specs=[row(D), pl.BlockSpec((tm * SUBLANE, LANE), lambda i: (i, 0)), row(LANE), row(LANE)],
        out_shape=[
            jax.ShapeDtypeStruct((T, D), F32),
            jax.ShapeDtypeStruct((T * SUBLANE, LANE), U32),
            jax.ShapeDtypeStruct((T, LANE), I32),
            jax.ShapeDtypeStruct((T, LANE), F32),
        ],
        compiler_params=_params(("parallel",)),
    )(x2, og, oc, osel, ow, nar, ng, wo, fg, rwh, rwl, rb)


def _moe_kernel(te_ref, tv_ref, tok_hbm, dst_hbm, hn_hbm, wgt_ref, w1g_ref, w1l_ref, b1g_ref, b1l_ref,
                w2_ref, b2_ref, out_hbm, tok_smem, dst_smem, xbuf, xb_ref, act_ref, ybuf,
                sem_tok, sem_dst, sem_in, sem_out, *, tm, nf, n_tiles):
    i = pl.program_id(0)
    f = pl.program_id(1)
    slot = i % 2
    valid = tv_ref[i]
    sub = MOE_SUB
    nsub = tm // sub
    half = SUBLANE * LANE
    sl8 = SUBLANE

    def tile_of(r):
        return pl.ds(pl.multiple_of(r * sl8, sl8), sl8)

    def tok_copy(t, sl):
        return pltpu.make_async_copy(tok_hbm.at[t], tok_smem.at[sl], sem_tok.at[sl])

    def dst_copy():
        return pltpu.make_async_copy(dst_hbm.at[i], dst_smem, sem_dst)

    def issue_gather(t, sl):
        for s in range(nsub):
            @pl.when(s * sub < tv_ref[t])
            def _():
                def body(r, c):
                    pltpu.make_async_copy(hn_hbm.at[tile_of(tok_smem[sl, r]), :],
                                          xbuf.at[sl, tile_of(r), :], sem_in.at[sl, s]).start()
                    return c

                lax.fori_loop(s * sub, (s + 1) * sub, body, 0, unroll=DMA_UNROLL)

    def wait_gather(sl, s):
        pltpu.make_async_copy(hn_hbm.at[pl.ds(0, sub * sl8), :], xbuf.at[sl, pl.ds(s * sub * sl8, sub * sl8), :],
                              sem_in.at[sl, s]).wait()

    def wait_scatter(count):
        for s in range(nsub):
            @pl.when(s * sub < count)
            def _():
                pltpu.make_async_copy(ybuf.at[pl.ds(s * sub * sl8, sub * sl8), :],
                                      out_hbm.at[pl.ds(0, sub * sl8), :], sem_out).wait()

    @pl.when(f == 0)
    def _():
        @pl.when(i == 0)
        def _():
            cp = tok_copy(0, 0)
            cp.start()
            cp.wait()
            issue_gather(0, 0)

        @pl.when(i + 1 < n_tiles)
        def _():
            tok_copy(i + 1, 1 - slot).start()

        dst_copy().start()
        for s in range(nsub):
            @pl.when(s * sub < valid)
            def _():
                wait_gather(slot, s)
                for j, w in enumerate(_load_rows_tiled(xbuf, s * sub, sub, lead=(slot,))):
                    lo, hi = _unpack_pairs(w)
                    xb_ref[s * sub:(s + 1) * sub, j * LANE:(j + 1) * LANE] = lo.astype(BF16)
                    xb_ref[s * sub:(s + 1) * sub, half + j * LANE:half + (j + 1) * LANE] = hi.astype(BF16)

        @pl.when(i + 1 < n_tiles)
        def _():
            tok_copy(i + 1, 1 - slot).wait()
            issue_gather(i + 1, 1 - slot)

    for s in range(nsub):
        @pl.when(s * sub < valid)
        def _():
            rows = slice(s * sub, (s + 1) * sub)
            xb = xb_ref[rows, :]
            glu = jnp.minimum(_dot(xb, w1g_ref[0]) + b1g_ref[0], SWIGLU_LIMIT)
            lin = jnp.clip(_dot(xb, w1l_ref[0]) + b1l_ref[0], -SWIGLU_LIMIT, SWIGLU_LIMIT)
            act_ref[f, rows, :] = (glu * _sigmoid(SWIGLU_ALPHA * glu) * (lin + 1.0)).astype(BF16)

    @pl.when(f == nf - 1)
    def _():
        @pl.when(i > 0)
        def _():
            wait_scatter(tv_ref[jnp.maximum(i - 1, 0)])

        dst_copy().wait()
        for s in range(nsub):
            @pl.when(s * sub < valid)
            def _():
                rows = slice(s * sub, (s + 1) * sub)
                fc = act_ref.shape[2]
                y = _dot(act_ref[0, rows, :], w2_ref[0, :fc, :])
                for c in range(1, nf):
                    y = y + _dot(act_ref[c, rows, :], w2_ref[0, c * fc:(c + 1) * fc, :])
                y = (y + b2_ref[0]) * wgt_ref[0, rows, :]
                _store_rows_tiled(ybuf, s * sub, sub, _pack_pairs(y[:, :half], y[:, half:]))

                def body(r, c):
                    pltpu.make_async_copy(ybuf.at[tile_of(r), :], out_hbm.at[tile_of(dst_smem[r]), :],
                                          sem_out).start()
                    return c

                lax.fori_loop(s * sub, (s + 1) * sub, body, 0, unroll=DMA_UNROLL)

        @pl.when(i == n_tiles - 1)
        def _():
            wait_scatter(valid)
            ybuf[...] = jnp.zeros_like(ybuf)
            dump = pltpu.make_async_copy(ybuf, out_hbm.at[pl.ds(out_hbm.shape[0] - tm * sl8, tm * sl8), :], sem_out)
            dump.start()
            dump.wait()


def _moe(tile_exp, tile_valid, tok, dst, hn, wgt, w1, b1, w2, b2, n_out_rows, tm, fc):
    n_tiles = tok.shape[0]
    D = w1.shape[1]
    assert D == 2 * SUBLANE * LANE
    dff = w2.shape[1]
    nf = dff // fc
    grid_spec = pltpu.PrefetchScalarGridSpec(
        num_scalar_prefetch=2,
        grid=(n_tiles, nf),
        in_specs=[
            pl.BlockSpec(memory_space=pl.ANY),
            pl.BlockSpec(memory_space=pl.ANY),
            pl.BlockSpec(memory_space=pl.ANY),
            pl.BlockSpec((1, tm, 1), lambda i, f, te, tv: (i, 0, 0)),
            pl.BlockSpec((1, D, fc), lambda i, f, te, tv: (te[i], 0, f)),
            pl.BlockSpec((1, D, fc), lambda i, f, te, tv: (te[i], 0, nf + f)),
            pl.BlockSpec((1, 1, fc), lambda i, f, te, tv: (te[i], 0, f)),
            pl.BlockSpec((1, 1, fc), lambda i, f, te, tv: (te[i], 0, nf + f)),
            pl.BlockSpec((1, dff, D), lambda i, f, te, tv: (te[i], 0, 0)),
            pl.BlockSpec((1, 1, D), lambda i, f, te, tv: (te[i], 0, 0)),
        ],
        out_specs=pl.BlockSpec(memory_space=pl.ANY),
        scratch_shapes=[
            pltpu.SMEM((2, tm), I32),
            pltpu.SMEM((tm,), I32),
            pltpu.VMEM((2, tm * SUBLANE, LANE), U32),
            pltpu.VMEM((tm, D), BF16),
            pltpu.VMEM((nf, tm, fc), BF16),
            pltpu.VMEM((tm * SUBLANE, LANE), U32),
            pltpu.SemaphoreType.DMA((2,)),
            pltpu.SemaphoreType.DMA(()),
            pltpu.SemaphoreType.DMA((2, tm // MOE_SUB)),
            pltpu.SemaphoreType.DMA(()),
        ],
    )
    return pl.pallas_call(
        functools.partial(_moe_kernel, tm=tm, nf=nf, n_tiles=n_tiles),
        grid_spec=grid_spec,
        out_shape=jax.ShapeDtypeStruct((n_out_rows * SUBLANE, LANE), U32),
        compiler_params=_params(("arbitrary", "arbitrary")),
    )(tile_exp, tile_valid, tok, dst, hn, wgt, w1, w1, b1, b1, w2, b2)


def _combine_kernel(h1_ref, y0_ref, y1_ref, y2_ref, y3_ref, g_ref, o_ref):
    h = h1_ref[...]
    tm = h.shape[0]
    for y_ref in (y0_ref, y1_ref, y2_ref, y3_ref):
        pairs = [_unpack_pairs(w) for w in _load_rows_tiled(y_ref, 0, tm)]
        h = h + jnp.concatenate([p[0] for p in pairs] + [p[1] for p in pairs], axis=1)
    o_ref[...] = _rms(h, g_ref[...])


def _combine(ys, h1, g, tm):
    T, D = h1.shape
    nb = T // tm
    slab = lambda k: pl.BlockSpec((tm * SUBLANE, LANE), lambda i, k=k: (k * nb + i, 0))
    return pl.pallas_call(
        _combine_kernel,
        grid=(nb,),
        in_specs=[pl.BlockSpec((tm, D), lambda i: (i, 0))] + [slab(k) for k in range(TOP_K)]
        + [pl.BlockSpec((1, D), lambda i: (0, 0))],
        out_specs=pl.BlockSpec((tm, D), lambda i: (i, 0)),
        out_shape=jax.ShapeDtypeStruct((T, D), F32),
        compiler_params=_params(("parallel",)),
    )(h1, ys, ys, ys, ys, g)


def _t5_bucket(dist):
    n = jnp.maximum(dist, 0)
    exact = N_BUCKETS // 2
    scaled = jnp.log(jnp.maximum(n, 1).astype(F32) / exact) / math.log(MAX_DISTANCE / exact)
    large = exact + (scaled * (N_BUCKETS - exact)).astype(I32)
    return jnp.where(n < exact, n, jnp.minimum(large, N_BUCKETS - 1)).astype(I32)


def _bias_tables(rel_bias, S):
    shifted = (rel_bias - rel_bias[N_BUCKETS - 1][None, :]).astype(F32)

    def lookup(dist):
        bucket = _t5_bucket(dist)
        out = jnp.zeros((NSA_HEADS,) + dist.shape, F32)
        for kb in range(N_BUCKETS - 1):
            out = jnp.where((bucket == kb)[None], shifted[kb].reshape((NSA_HEADS,) + (1,) * dist.ndim), out)
        return jnp.where((dist >= 0)[None], out, NEG)

    t = SEL_T
    r = jnp.arange(t)[:, None]
    c = jnp.arange(t)[None, :]
    dtiles = jnp.stack([lookup(r - c), lookup(t + r - c)], axis=1)
    w4 = jnp.where(r < c, 0.0, NEG).astype(F32)
    ncp = S // CMP_STRIDE
    dist_c = jnp.arange(S)[:, None] - (jnp.arange(ncp) * CMP_STRIDE + CMP_LEN - 1)[None, :]
    return dtiles, w4, lookup(dist_c)


def _overlap_t(S):
    ncp = S // CMP_STRIDE
    ns = S // SEL_BLOCK
    cs = jnp.arange(ncp)[None, :] * CMP_STRIDE
    ss = jnp.arange(ns)[:, None] * SEL_BLOCK
    ov = (cs < ss + SEL_BLOCK) & (cs + CMP_LEN > ss) & (jnp.arange(ncp)[None, :] < ncp - 1)
    return ov.astype(BF16)


def _route(top_idx, top_w, n_exp, tm):
    T = top_idx.shape[0]
    A = T * TOP_K
    e_flat = top_idx.reshape(A)
    w_flat = top_w.reshape(A)
    e_sorted, order = lax.sort_key_val(e_flat, jnp.arange(A, dtype=I32), is_stable=True)
    counts = jnp.sum((e_flat[:, None] == jnp.arange(n_exp, dtype=I32)[None, :]).astype(I32), axis=0)
    padded = (counts + tm - 1) // tm * tm
    start = jnp.cumsum(counts) - counts
    pend = jnp.cumsum(padded)
    pstart = pend - padded
    n_tiles = A // tm + n_exp
    tile_row0 = jnp.arange(n_tiles, dtype=I32) * tm
    tile_exp = jnp.minimum(jnp.sum((pend[None, :] <= tile_row0[:, None]).astype(I32), axis=1), n_exp - 1)
    tile_valid = jnp.clip(pstart[tile_exp] + counts[tile_exp] - tile_row0, 0, tm).astype(I32)
    off = tile_row0[:, None] - pstart[tile_exp][:, None] + jnp.arange(tm, dtype=I32)[None, :]
    live = jnp.arange(tm, dtype=I32)[None, :] < tile_valid[:, None]
    src = order[jnp.clip(start[tile_exp][:, None] + off, 0, A - 1)]
    buf_tok = jnp.where(live, src // TOP_K, 0)
    buf_w = jnp.where(live, w_flat[src], 0.0)
    buf_dst = jnp.where(live, (src % TOP_K) * T + src // TOP_K, TOP_K * T + jnp.arange(tm, dtype=I32)[None, :])
    return buf_tok, buf_dst, buf_w[:, :, None], tile_exp, tile_valid


def kernel(x, norm_mix_g, w_in, gla_wa2, gla_ba, gla_norm_g, cmp_pe_k, cmp_wk1, cmp_wk2, cmp_pe_v,
           cmp_wv1, cmp_wv2, nsa_norm_g, rel_bias, w_out, norm_ffn_g, router_w, router_b, exp_w1,
           exp_b1, exp_w2, exp_b2, norm_final_g):
    B, S, D = x.shape
    T = B * S
    G = NSA_KV
    n_exp = router_w.shape[-1]
    assert w_in.shape[0] == 1 and S % 1024 == 0 and S // SEL_BLOCK <= LANE and n_exp <= LANE
    x2 = x.reshape(T, D)

    w = w_in[0]
    a0 = COL_NQ
    n0 = a0 + GLA_RANK
    g0 = n0 + (N_MAIN - COL_NQ)
    w_main = jnp.concatenate([
        w[:, :COL_GK] * (GLA_DK ** -0.5), w[:, COL_GK:a0],
        w[:, n0:n0 + NSA_WIDTH] * (NSA_DH ** -0.5), w[:, n0 + NSA_WIDTH:g0]], axis=1).astype(BF16)
    w_nar = jnp.concatenate([w[:, a0:n0], w[:, g0:], jnp.zeros((D, LANE - GLA_RANK - 3 * NSA_HEADS), F32)],
                            axis=1).astype(BF16)
    wa2p = jnp.concatenate([gla_wa2[0], jnp.zeros((LANE - GLA_RANK, GLA_HEADS * GLA_DK), F32)], axis=0).astype(BF16)

    tm1 = min(1024, T)
    proj, log_a, nar = _inproj(x2, norm_mix_g, w_main, w_nar, wa2p, gla_ba, tm1, 512)

    e_w1, e_w2 = exp_w1[0], exp_w2[0]
    o_gla = _gla(proj, log_a, gla_norm_g, B, S, min(256, S))

    ncp = S // CMP_STRIDE
    half = CMP_STRIDE * NSA_DH

    def stride_groups(col):
        t4 = proj[:, col:col + KV_W].reshape(B, ncp, CMP_STRIDE, G, NSA_DH)
        return t4.transpose(0, 3, 1, 2, 4).reshape(B * G, ncp, half)

    hb = jnp.stack([stride_groups(COL_KC), stride_groups(COL_VC)], axis=0)
    pe = jnp.stack([cmp_pe_k[0].reshape(2, half), cmp_pe_v[0].reshape(2, half)], axis=0)
    w1c = jnp.stack([cmp_wk1[0], cmp_wv1[0]], axis=0).astype(BF16)
    w2c = jnp.stack([cmp_wk2[0], cmp_wv2[0]], axis=0).astype(BF16)
    cmp_kv = _compress(hb, pe, w1c, w2c)

    dtiles, w4, bias_c = _bias_tables(rel_bias, S)
    o_cmp, sel = _cmpattn(proj, cmp_kv, bias_c, _overlap_t(S), B, S)
    o_sel, o_win, w1b, w2b = _selwin(proj, sel, dtiles, w4, e_w1.reshape(-1, e_w1.shape[-1]),
                                     e_w2.reshape(-1, e_w2.shape[-1]), B, S)
    w1b, w2b = w1b.reshape(e_w1.shape), w2b.reshape(e_w2.shape)

    rw = jnp.concatenate([router_w[0], jnp.zeros((D, LANE - n_exp), F32)], axis=1)
    rwh = rw.astype(BF16)
    rwl = (rw - rwh.astype(F32)).astype(BF16)
    rb = jnp.concatenate([router_b[0], jnp.full((LANE - n_exp,), NEG, F32)])[None, :]
    h1, hn, top_i, top_w = _outproj(x2, o_gla, o_cmp, o_sel, o_win, nar, nsa_norm_g, w_out[0].astype(BF16),
                                    norm_ffn_g, rwh, rwl, rb, min(512, T))

    tm6 = min(1024, T)
    tok, dst, wgt, tile_exp, tile_valid = _route(top_i[:, :TOP_K], top_w[:, :TOP_K], n_exp, tm6)
    ys = _moe(tile_exp, tile_valid, tok, dst, hn, wgt, w1b, exp_b1[0][:, None, :],
              w2b, exp_b2[0][:, None, :], TOP_K * T + tm6, tm6, 512)
    out = _combine(ys, h1, norm_final_g[None, :], min(512, T))
    return out.reshape(B, S, D)
```

```python
import functools
import math

import numpy as np
import jax
import jax.numpy as jnp
from jax import lax
from jax.experimental import pallas as pl
from jax.experimental.pallas import tpu as pltpu

F32 = jnp.float32
BF16 = jnp.bfloat16
I32 = jnp.int32
U32 = jnp.uint32

GLA_HEADS = 4
GLA_DK = 128
GLA_DV = 256
GLA_RANK = 16
GLA_TAU = 16.0
NSA_HEADS = 8
NSA_KV = 2
NSA_REP = NSA_HEADS // NSA_KV
NSA_DH = 128
CMP_LEN = 32
CMP_STRIDE = 16
SEL_BLOCK = 64
SEL_TOPN = 16
WINDOW = 512
N_BUCKETS = 32
MAX_DISTANCE = 128
TOP_K = 4
SWIGLU_ALPHA = 1.702
SWIGLU_LIMIT = 7.0
RMS_EPS = 1e-5
NEG = -1e30
FORCE_SCORE = 1e4

GLA_WIDTH = GLA_HEADS * GLA_DV
NSA_WIDTH = NSA_HEADS * NSA_DH
KV_W = NSA_KV * NSA_DH

COL_GQ = 0
COL_GK = COL_GQ + GLA_HEADS * GLA_DK
COL_GV = COL_GK + GLA_HEADS * GLA_DK
COL_GG = COL_GV + GLA_WIDTH
COL_NQ = COL_GG + GLA_WIDTH
COL_KC = COL_NQ + NSA_WIDTH
COL_VC = COL_KC + KV_W
COL_KS = COL_VC + KV_W
COL_VS = COL_KS + KV_W
COL_KW = COL_VS + KV_W
COL_VW = COL_KW + KV_W
N_MAIN = COL_VW + KV_W
LANE = 128
ATT_T = 128
SEL_T = 256
GLA_C = 64
GLA_SUB = 16
MOE_SUB = 512
DMA_UNROLL = 32
GATE_COL0 = GLA_RANK

VMEM_LIMIT = 56 * 1024 * 1024

NT = (((1,), (1,)), ((), ()))
TN = (((0,), (0,)), ((), ()))


def _dot(a, b):
    return jnp.dot(a, b, preferred_element_type=F32)


def _dg(a, b, dims):
    return lax.dot_general(a, b, dims, preferred_element_type=F32)


def _sigmoid(x):
    return 1.0 / (1.0 + jnp.exp(-x))


def _log_sigmoid(z):
    return -(jnp.maximum(-z, 0.0) + jnp.log(1.0 + jnp.exp(-jnp.abs(z))))


def _rms(x, g):
    return x * lax.rsqrt(jnp.mean(x * x, axis=-1, keepdims=True) + RMS_EPS) * g


def _split3(x):
    hi = x.astype(BF16)
    r1 = x - hi.astype(F32)
    mid = r1.astype(BF16)
    lo = (r1 - mid.astype(F32)).astype(BF16)
    return hi, mid, lo


def _pack_pairs(lo, hi):
    lo_bits = lax.bitcast_convert_type(lo.astype(BF16).astype(F32), U32) >> 16
    hi_bits = lax.bitcast_convert_type(hi.astype(BF16).astype(F32), U32) & jnp.uint32(0xFFFF0000)
    return lo_bits | hi_bits


def _unpack_pairs(w):
    lo = lax.bitcast_convert_type(w << 16, F32)
    hi = lax.bitcast_convert_type(w & jnp.uint32(0xFFFF0000), F32)
    return lo, hi


SUBLANE = 8


def _store_rows_tiled(ref, base, n, packed):
    for j in range(SUBLANE):
        ref[pl.ds(base * SUBLANE + j, n, stride=SUBLANE), :] = packed[:, j * LANE:(j + 1) * LANE]


def _load_rows_tiled(ref, base, n, lead=()):
    return [ref[lead + (pl.ds(base * SUBLANE + j, n, stride=SUBLANE), slice(None))] for j in range(SUBLANE)]


def _params(sem, vmem=VMEM_LIMIT):
    return pltpu.CompilerParams(dimension_semantics=sem, vmem_limit_bytes=vmem)


def _inproj_kernel(x_ref, g_ref, w_ref, wn_ref, wa2_ref, ba_ref, proj_ref, la_ref, nar_ref, xn_ref):
    @pl.when(pl.program_id(1) == 0)
    def _():
        xn = _rms(x_ref[...], g_ref[...]).astype(BF16)
        xn_ref[...] = xn
        nar = _dot(xn, wn_ref[...])
        nar_ref[...] = nar
        z = _dot(nar.astype(BF16), wa2_ref[...]) + ba_ref[...]
        la_ref[...] = _log_sigmoid(z) * (1.0 / GLA_TAU)

    proj_ref[...] = _dot(xn_ref[...], w_ref[...]).astype(BF16)


def _inproj(x2, g, w_main, w_nar, wa2p, ba, tm, tn):
    T, D = x2.shape
    nj = N_MAIN // tn
    return pl.pallas_call(
        _inproj_kernel,
        grid=(T // tm, nj),
        in_specs=[
            pl.BlockSpec((tm, D), lambda i, j: (i, 0)),
            pl.BlockSpec((1, D), lambda i, j: (0, 0)),
            pl.BlockSpec((D, tn), lambda i, j: (0, j)),
            pl.BlockSpec((D, LANE), lambda i, j: (0, 0)),
            pl.BlockSpec((LANE, GLA_HEADS * GLA_DK), lambda i, j: (0, 0)),
            pl.BlockSpec((1, GLA_HEADS * GLA_DK), lambda i, j: (0, 0)),
        ],
        out_specs=[
            pl.BlockSpec((tm, tn), lambda i, j: (i, j)),
            pl.BlockSpec((tm, GLA_HEADS * GLA_DK), lambda i, j: (i, 0)),
            pl.BlockSpec((tm, LANE), lambda i, j: (i, 0)),
        ],
        out_shape=[
            jax.ShapeDtypeStruct((T, N_MAIN), BF16),
            jax.ShapeDtypeStruct((T, GLA_HEADS * GLA_DK), F32),
            jax.ShapeDtypeStruct((T, LANE), F32),
        ],
        scratch_shapes=[pltpu.VMEM((tm, D), BF16)],
        compiler_params=_params(("parallel", "arbitrary")),
    )(x2, g, w_main, w_nar, wa2p, ba)


def _gla_kernel(q_ref, k_ref, v_ref, g_ref, la_ref, ng_ref, o_ref, st_ref, *, cb):
    @pl.when(pl.program_id(1) == 0)
    def _():
        st_ref[...] = jnp.zeros_like(st_ref)

    C, SUB = GLA_C, GLA_SUB
    heads = range(GLA_HEADS)
    dk = lambda h: slice(h * GLA_DK, (h + 1) * GLA_DK)
    dv = lambda h: slice(h * GLA_DV, (h + 1) * GLA_DV)
    ri = lax.broadcasted_iota(I32, (C, C), 0)
    ci = lax.broadcasted_iota(I32, (C, C), 1)
    tri = jnp.where(ri >= ci, 1.0, 0.0).astype(BF16)
    nsub = C // SUB
    sub_of_row = lax.broadcasted_iota(I32, (C, GLA_DK), 0) // SUB
    for n in range(cb // C):
        rows = slice(n * C, (n + 1) * C)
        b = []
        for h in heads:
            hi, mid, lo = _split3(la_ref[rows, dk(h)])
            b.append(_dot(tri, hi) + _dot(tri, mid) + _dot(tri, lo))
        q = [q_ref[rows, dk(h)].astype(F32) for h in heads]
        k = [k_ref[rows, dk(h)].astype(F32) for h in heads]
        v = [v_ref[rows, dv(h)] for h in heads]
        st = [st_ref[h] for h in heads]
        o = [_dg((q[h] * jnp.exp(b[h])).astype(BF16), st[h].astype(BF16), NT) for h in heads]
        intra = []
        for h in heads:
            refs = [jnp.zeros((1, GLA_DK), F32)] + [b[h][s * SUB - 1:s * SUB, :] for s in range(1, nsub)]
            ref_row = jnp.concatenate([jnp.broadcast_to(r, (SUB, GLA_DK)) for r in refs], axis=0)
            qsc = q[h] * jnp.exp(b[h] - ref_row)
            q_cat = jnp.concatenate([jnp.where(sub_of_row == s, qsc, 0.0) for s in range(nsub)], axis=1)
            k_cat = jnp.concatenate([jnp.where(sub_of_row <= s, k[h] * jnp.exp(refs[s] - b[h]), 0.0)
                                     for s in range(nsub)], axis=1)
            sc = jnp.where(ci <= ri, _dg(q_cat.astype(BF16), k_cat.astype(BF16), NT), 0.0)
            intra.append(_dot(sc.astype(BF16), v[h]))
        for h in heads:
            bl = b[h][C - 1:C, :]
            kdec = (k[h] * jnp.exp(bl - b[h])).astype(BF16)
            st_ref[h] = st[h] * jnp.exp(bl) + _dg(v[h], kdec, TN)
        for h in heads:
            oh = o[h] + intra[h]
            gg = g_ref[rows, dv(h)].astype(F32)
            o_ref[rows, dv(h)] = (_rms(oh, ng_ref[...]) * (gg * _sigmoid(gg))).astype(BF16)


def _cast_slab(w2d, nsteps):
    rows, width = w2d.shape
    assert rows % nsteps == 0 and (rows // nsteps) % 16 == 0
    return rows // nsteps, width


def _gla(proj, la, ng, B, S, cb):
    T = B * S
    ncb = S // cb
    dkw = GLA_HEADS * GLA_DK
    row = lambda b, c: b * ncb + c
    return pl.pallas_call(
        functools.partial(_gla_kernel, cb=cb),
        grid=(B, ncb),
        in_specs=[
            pl.BlockSpec((cb, dkw), lambda b, c: (row(b, c), COL_GQ // dkw)),
            pl.BlockSpec((cb, dkw), lambda b, c: (row(b, c), COL_GK // dkw)),
            pl.BlockSpec((cb, GLA_WIDTH), lambda b, c: (row(b, c), COL_GV // GLA_WIDTH)),
            pl.BlockSpec((cb, GLA_WIDTH), lambda b, c: (row(b, c), COL_GG // GLA_WIDTH)),
            pl.BlockSpec((cb, dkw), lambda b, c: (row(b, c), 0)),
            pl.BlockSpec((1, GLA_DV), lambda b, c: (0, 0)),
        ],
        out_specs=pl.BlockSpec((cb, GLA_WIDTH), lambda b, c: (row(b, c), 0)),
        out_shape=jax.ShapeDtypeStruct((T, GLA_WIDTH), BF16),
        scratch_shapes=[pltpu.VMEM((GLA_HEADS, GLA_DV, GLA_DK), F32)],
        compiler_params=_params(("parallel", "arbitrary")),
    )(proj, proj, proj, proj, la, ng)


def _compress_kernel(hb_ref, pe_ref, w1_ref, w2_ref, o_ref, *, ncp):
    half = CMP_STRIDE * NSA_DH
    hb = hb_ref[0, 0].astype(F32)
    pe = pe_ref[0]
    top = (hb + pe[0:1]).astype(BF16)
    bot = (hb + pe[1:2]).astype(BF16)
    a = _dot(top, w1_ref[0, :half, :])
    bm = _dot(bot, w1_ref[0, half:, :])
    pre = a + pltpu.roll(bm, ncp - 1, 0)
    cdf = 0.5 * (1.0 + jnp.tanh(math.sqrt(2.0 / math.pi) * (pre + 0.044715 * (pre * pre * pre))))
    o_ref[0, 0] = _dot((pre * cdf).astype(BF16), w2_ref[0]).astype(BF16)


def _compress(hb, pe, w1, w2):
    _, BG, ncp, half = hb.shape
    return pl.pallas_call(
        functools.partial(_compress_kernel, ncp=ncp),
        grid=(2, BG),
        in_specs=[
            pl.BlockSpec((1, 1, ncp, half), lambda t, n: (t, n, 0, 0)),
            pl.BlockSpec((1, 2, half), lambda t, n: (t, 0, 0)),
            pl.BlockSpec((1, 2 * half, NSA_DH), lambda t, n: (t, 0, 0)),
            pl.BlockSpec((1, NSA_DH, NSA_DH), lambda t, n: (t, 0, 0)),
        ],
        out_specs=pl.BlockSpec((1, 1, ncp, NSA_DH), lambda t, n: (t, n, 0, 0)),
        out_shape=jax.ShapeDtypeStruct((2, BG, ncp, NSA_DH), BF16),
        compiler_params=_params(("parallel", "parallel")),
    )(hb, pe, w1, w2)


def _cmpattn_kernel(q_ref, kc_ref, vc_ref, bias_ref, ovt_ref, o_ref, sel_ref, *, ns, ncp):
    tq = ATT_T
    i = pl.program_id(1)
    kc = kc_ref[0, 0]
    vc = vc_ref[0, 0]
    pos = i * tq + lax.broadcasted_iota(I32, (tq, 1), 0)
    row_valid = jnp.where(pos >= CMP_LEN - 1, 1.0, 0.0)
    psum = jnp.zeros((tq, ncp), F32)
    outs = []
    for r in range(NSA_REP):
        q = q_ref[:, r * NSA_DH:(r + 1) * NSA_DH]
        s = _dg(q, kc, NT) + bias_ref[r]
        e = jnp.exp(s - jnp.max(s, axis=-1, keepdims=True))
        p = e / jnp.sum(e, axis=-1, keepdims=True) * row_valid
        psum = psum + p
        outs.append(_dot(p.astype(BF16), vc))
    o_ref[...] = jnp.concatenate(outs, axis=1).astype(BF16)

    ph = psum.astype(BF16)
    pl_ = (psum - ph.astype(F32)).astype(BF16)
    imp = _dg(ovt_ref[...], ph, NT) + _dg(ovt_ref[...], pl_, NT)
    n_i = lax.broadcasted_iota(I32, (ns, tq), 0)
    cur = (i * tq + lax.broadcasted_iota(I32, (ns, tq), 1)) // SEL_BLOCK
    forced = (n_i == 0) | (n_i == cur) | (n_i == cur - 1)
    score = jnp.where(forced, FORCE_SCORE, jnp.where(n_i <= cur, imp, -1.0))
    cnt = jnp.zeros((ns, tq), F32)
    for m in range(ns):
        sm = score[m:m + 1, :]
        ge = jnp.where(sm >= score, 1.0, 0.0)
        gt = jnp.where(sm > score, 1.0, 0.0)
        cnt = cnt + jnp.where(n_i > m, ge, gt)
    sel_t = jnp.where(cnt < min(SEL_TOPN, ns), 1.0, 0.0)
    if ns < LANE:
        sel_t = jnp.concatenate([sel_t, jnp.zeros((LANE - ns, tq), F32)], axis=0)
    sel_ref[0] = sel_t.T.astype(BF16)


def _cmpattn(proj, cmp_kv, bias_c, ovt, B, S):
    T = B * S
    tq = ATT_T
    nq = S // tq
    ns = S // SEL_BLOCK
    ncp = S // CMP_STRIDE
    G = NSA_KV
    qcol = COL_NQ // (NSA_REP * NSA_DH)
    return pl.pallas_call(
        functools.partial(_cmpattn_kernel, ns=ns, ncp=ncp),
        grid=(G, nq, B),
        in_specs=[
            pl.BlockSpec((tq, NSA_REP * NSA_DH), lambda g, i, b: (b * nq + i, qcol + g)),
            pl.BlockSpec((1, 1, ncp, NSA_DH), lambda g, i, b: (0, b * G + g, 0, 0)),
            pl.BlockSpec((1, 1, ncp, NSA_DH), lambda g, i, b: (1, b * G + g, 0, 0)),
            pl.BlockSpec((NSA_REP, tq, ncp), lambda g, i, b: (g, i, 0)),
            pl.BlockSpec((ns, ncp), lambda g, i, b: (0, 0)),
        ],
        out_specs=[
            pl.BlockSpec((tq, NSA_REP * NSA_DH), lambda g, i, b: (b * nq + i, g)),
            pl.BlockSpec((1, tq, LANE), lambda g, i, b: (b * G + g, i, 0)),
        ],
        out_shape=[
            jax.ShapeDtypeStruct((T, NSA_WIDTH), BF16),
            jax.ShapeDtypeStruct((B * G, S, LANE), BF16),
        ],
        compiler_params=_params(("parallel", "parallel", "parallel")),
    )(proj, cmp_kv, cmp_kv, bias_c, ovt)


def _flash_update(s, v, m_ref, al_ref):
    reps = s.shape[1] // LANE
    m_old = m_ref[...]
    part = s[:, :LANE]
    for c in range(1, reps):
        part = jnp.maximum(part, s[:, c * LANE:(c + 1) * LANE])
    m_new = jnp.maximum(m_old, jnp.max(part, axis=-1, keepdims=True))
    alpha = jnp.exp(m_old - m_new)
    p = jnp.exp(s - jnp.concatenate([m_new] * reps, axis=1))
    v_ext = jnp.concatenate([v, jnp.ones(v.shape, BF16)], axis=1)
    al_ref[...] = jnp.concatenate([alpha, alpha], axis=1) * al_ref[...] + _dot(p.astype(BF16), v_ext)
    m_ref[...] = m_new


def _selwin_kernel(q_ref, ks_ref, vs_ref, kw_ref, vw_ref, sel_ref, d_ref, w4_ref, wf1_ref, wf2_ref,
                   osel_ref, owin_ref, wb1_ref, wb2_ref, m_ref, al_ref, s_ref):
    wb1_ref[...] = wf1_ref[...].astype(BF16)
    wb2_ref[...] = wf2_ref[...].astype(BF16)
    t = SEL_T
    R = NSA_REP
    i = pl.program_id(2)
    q = jnp.concatenate([q_ref[:, r * NSA_DH:(r + 1) * NSA_DH] for r in range(R)], axis=0)
    unsel = (1.0 - sel_ref[0].astype(F32)).astype(BF16)
    q_sel = jnp.concatenate([q, jnp.concatenate([unsel] * R, axis=0)], axis=1)
    c_blk = lax.broadcasted_iota(I32, (t, LANE), 0) // SEL_BLOCK
    n_i = lax.broadcasted_iota(I32, (t, LANE), 1)
    blk_per_tile = t // SEL_BLOCK

    def init():
        m_ref[...] = jnp.full(m_ref.shape, NEG, F32)
        al_ref[...] = jnp.zeros(al_ref.shape, F32)

    def tile(ref, j):
        return ref[pl.ds(pl.multiple_of(j * t, t), t), :]

    def logits(k_ref, j, add):
        s = _dg(q, tile(k_ref, j), NT).reshape(R, t, t)
        return (s + add).reshape(R * t, t)

    def sel_logits(j):
        own_block = jnp.where(n_i == blk_per_tile * j + c_blk, NEG, 0.0).astype(BF16)
        return _dg(q_sel, jnp.concatenate([tile(ks_ref, j), own_block], axis=1), NT)

    def write(o_ref, al):
        o = al[:, :NSA_DH] / al[:, NSA_DH:]
        o_ref[...] = jnp.concatenate([o[r * t:(r + 1) * t] for r in range(R)], axis=1).astype(BF16)

    init()

    jn = jnp.maximum(i - 1, 0)
    s_ref[...] = sel_logits(0)

    def far(j, c):
        s = s_ref[...]
        s_ref[...] = sel_logits(j + 1)
        _flash_update(s, tile(vs_ref, j), m_ref, al_ref)
        return c

    def with_bias(s, add):
        return (s.reshape(R, t, t) + add).reshape(R * t, t)

    lax.fori_loop(0, jn, far, 0)
    off = jnp.where(i == 0, NEG, 0.0)
    s_near = s_ref[...]
    s_diag = sel_logits(i)
    _flash_update(with_bias(s_near, d_ref[:, 1] + off), tile(vs_ref, jn), m_ref, al_ref)
    _flash_update(with_bias(s_diag, d_ref[:, 0]), tile(vs_ref, i), m_ref, al_ref)
    write(osel_ref, al_ref[...])

    init()
    pre = WINDOW // t
    for o in range(pre, -1, -1):
        jw = i - o
        jc = jnp.maximum(jw, 0)
        off = jnp.where(jw < 0, NEG, 0.0)
        if o == 0:
            add = d_ref[:, 0] + off
        elif o == 1:
            add = d_ref[:, 1] + off
        elif o == pre:
            add = (w4_ref[...] + off)[None]
        else:
            add = off
        _flash_update(logits(kw_ref, jc, add), tile(vw_ref, jc), m_ref, al_ref)
    write(owin_ref, al_ref[...])


def _selwin(proj, sel, dtiles, w4, wf1, wf2, B, S):
    T = B * S
    t = SEL_T
    nq = S // t
    G = NSA_KV
    qcol = COL_NQ // (NSA_REP * NSA_DH)
    kv = lambda col: pl.BlockSpec((S, NSA_DH), lambda b, g, i, col=col: (b, col // NSA_DH + g))
    slab = lambda wf: pl.BlockSpec(_cast_slab(wf, B * G * nq), lambda b, g, i: ((b * G + g) * nq + i, 0))
    return pl.pallas_call(
        _selwin_kernel,
        grid=(B, G, nq),
        in_specs=[
            pl.BlockSpec((t, NSA_REP * NSA_DH), lambda b, g, i: (b * nq + i, qcol + g)),
            kv(COL_KS), kv(COL_VS), kv(COL_KW), kv(COL_VW),
            pl.BlockSpec((1, t, LANE), lambda b, g, i: (b * G + g, i, 0)),
            pl.BlockSpec((NSA_REP, 2, t, t), lambda b, g, i: (g, 0, 0, 0)),
            pl.BlockSpec((t, t), lambda b, g, i: (0, 0)),
            slab(wf1), slab(wf2),
        ],
        out_specs=[
            pl.BlockSpec((t, NSA_REP * NSA_DH), lambda b, g, i: (b * nq + i, g)),
            pl.BlockSpec((t, NSA_REP * NSA_DH), lambda b, g, i: (b * nq + i, g)),
            slab(wf1), slab(wf2),
        ],
        out_shape=[
            jax.ShapeDtypeStruct((T, NSA_WIDTH), BF16),
            jax.ShapeDtypeStruct((T, NSA_WIDTH), BF16),
            jax.ShapeDtypeStruct(wf1.shape, BF16),
            jax.ShapeDtypeStruct(wf2.shape, BF16),
        ],
        scratch_shapes=[
            pltpu.VMEM((NSA_REP * t, LANE), F32),
            pltpu.VMEM((NSA_REP * t, 2 * NSA_DH), F32),
            pltpu.VMEM((NSA_REP * t, t), F32),
        ],
        compiler_params=_params(("parallel", "parallel", "parallel")),
    )(proj, proj, proj, proj, proj, sel, dtiles, w4, wf1, wf2)


def _outproj_kernel(x_ref, og_ref, oc_ref, os_ref, ow_ref, nar_ref, ng_ref, wo_ref, fg_ref,
                    rwh_ref, rwl_ref, rb_ref, h1_ref, hn_ref, ti_ref, tw_ref):
    gates = _sigmoid(nar_ref[...])
    parts = []
    for h in range(NSA_HEADS):
        cs = slice(h * NSA_DH, (h + 1) * NSA_DH)
        g0 = gates[:, GATE_COL0 + h:GATE_COL0 + h + 1]
        g1 = gates[:, GATE_COL0 + NSA_HEADS + h:GATE_COL0 + NSA_HEADS + h + 1]
        g2 = gates[:, GATE_COL0 + 2 * NSA_HEADS + h:GATE_COL0 + 2 * NSA_HEADS + h + 1]
        parts.append(g0 * oc_ref[:, cs].astype(F32) + g1 * os_ref[:, cs].astype(F32)
                     + g2 * ow_ref[:, cs].astype(F32))
    on = _rms(jnp.concatenate(parts, axis=1), ng_ref[...]).astype(BF16)
    h1 = x_ref[...] + _dot(og_ref[...], wo_ref[:GLA_WIDTH, :]) + _dot(on, wo_ref[GLA_WIDTH:, :])
    h1_ref[...] = h1
    hn = _rms(h1, fg_ref[...])
    half = hn.shape[1] // 2
    _store_rows_tiled(hn_ref, 0, hn.shape[0], _pack_pairs(hn[:, :half], hn[:, half:]))

    hh = hn.astype(BF16)
    hl = (hn - hh.astype(F32)).astype(BF16)
    lg = _dot(hh, rwh_ref[...]) + _dot(hl, rwh_ref[...]) + _dot(hh, rwl_ref[...]) + rb_ref[...]
    lane = lax.broadcasted_iota(I32, lg.shape, 1)
    vals, idxs = [], []
    for _ in range(TOP_K):
        mx = jnp.max(lg, axis=-1, keepdims=True)
        ix = jnp.min(jnp.where(lg == mx, lane, LANE), axis=-1, keepdims=True)
        vals.append(mx)
        idxs.append(ix)
        lg = jnp.where(lane == ix, -3e38, lg)
    es = [jnp.exp(v - vals[0]) for v in vals]
    den = es[0] + es[1] + es[2] + es[3]
    ti = jnp.zeros(lg.shape, I32)
    tw = jnp.zeros(lg.shape, F32)
    for kk in range(TOP_K):
        ti = jnp.where(lane == kk, idxs[kk], ti)
        tw = jnp.where(lane == kk, es[kk] / den, tw)
    ti_ref[...] = ti
    tw_ref[...] = tw


def _outproj(x2, og, oc, osel, ow, nar, ng, wo, fg, rwh, rwl, rb, tm):
    T, D = x2.shape
    row = lambda w: pl.BlockSpec((tm, w), lambda i: (i, 0))
    full = lambda a: pl.BlockSpec(a.shape, lambda i: (0,) * a.ndim)
    return pl.pallas_call(
        _outproj_kernel,
        grid=(T // tm,),
        in_specs=[row(D), row(GLA_WIDTH), row(NSA_WIDTH), row(NSA_WIDTH), row(NSA_WIDTH), row(LANE),
                  full(ng), full(wo), full(fg), full(rwh), full(rwl), full(rb)],
        out_specs=[row(D), pl.BlockSpec((tm * SUBLANE, LANE), lambda i: (i, 0)), row(LANE), row(LANE)],
        out_shape=[
            jax.ShapeDtypeStruct((T, D), F32),
            jax.ShapeDtypeStruct((T * SUBLANE, LANE), U32),
            jax.ShapeDtypeStruct((T, LANE), I32),
            jax.ShapeDtypeStruct((T, LANE), F32),
        ],
        compiler_params=_params(("parallel",)),
    )(x2, og, oc, osel, ow, nar, ng, wo, fg, rwh, rwl, rb)


def _moe_kernel(te_ref, tv_ref, tok_hbm, dst_hbm, hn_hbm, wgt_ref, w1g_ref, w1l_ref, b1g_ref, b1l_ref,
                w2_ref, b2_ref, out_hbm, tok_smem, dst_smem, xbuf, xb_ref, act_ref, ybuf,
                sem_tok, sem_dst, sem_in, sem_out, *, tm, nf, n_tiles):
    i = pl.program_id(0)
    f = pl.program_id(1)
    slot = i % 2
    valid = tv_ref[i]
    sub = MOE_SUB
    nsub = tm // sub
    half = SUBLANE * LANE
    sl8 = SUBLANE

    def tile_of(r):
        return pl.ds(pl.multiple_of(r * sl8, sl8), sl8)

    def tok_copy(t, sl):
        return pltpu.make_async_copy(tok_hbm.at[t], tok_smem.at[sl], sem_tok.at[sl])

    def dst_copy():
        return pltpu.make_async_copy(dst_hbm.at[i], dst_smem, sem_dst)

    def issue_gather(t, sl):
        for s in range(nsub):
            @pl.when(s * sub < tv_ref[t])
            def _():
                def body(r, c):
                    pltpu.make_async_copy(hn_hbm.at[tile_of(tok_smem[sl, r]), :],
                                          xbuf.at[sl, tile_of(r), :], sem_in.at[sl, s]).start()
                    return c

                lax.fori_loop(s * sub, (s + 1) * sub, body, 0, unroll=DMA_UNROLL)

    def wait_gather(sl, s):
        pltpu.make_async_copy(hn_hbm.at[pl.ds(0, sub * sl8), :], xbuf.at[sl, pl.ds(s * sub * sl8, sub * sl8), :],
                              sem_in.at[sl, s]).wait()

    def wait_scatter(count):
        for s in range(nsub):
            @pl.when(s * sub < count)
            def _():
                pltpu.make_async_copy(ybuf.at[pl.ds(s * sub * sl8, sub * sl8), :],
                                      out_hbm.at[pl.ds(0, sub * sl8), :], sem_out).wait()

    @pl.when(f == 0)
    def _():
        @pl.when(i == 0)
        def _():
            cp = tok_copy(0, 0)
            cp.start()
            cp.wait()
            issue_gather(0, 0)

        @pl.when(i + 1 < n_tiles)
        def _():
            tok_copy(i + 1, 1 - slot).start()

        dst_copy().start()
        for s in range(nsub):
            @pl.when(s * sub < valid)
            def _():
                wait_gather(slot, s)
                for j, w in enumerate(_load_rows_tiled(xbuf, s * sub, sub, lead=(slot,))):
                    lo, hi = _unpack_pairs(w)
                    xb_ref[s * sub:(s + 1) * sub, j * LANE:(j + 1) * LANE] = lo.astype(BF16)
                    xb_ref[s * sub:(s + 1) * sub, half + j * LANE:half + (j + 1) * LANE] = hi.astype(BF16)

        @pl.when(i + 1 < n_tiles)
        def _():
            tok_copy(i + 1, 1 - slot).wait()
            issue_gather(i + 1, 1 - slot)

    for s in range(nsub):
        @pl.when(s * sub < valid)
        def _():
            rows = slice(s * sub, (s + 1) * sub)
            xb = xb_ref[rows, :]
            glu = jnp.minimum(_dot(xb, w1g_ref[0]) + b1g_ref[0], SWIGLU_LIMIT)
            lin = jnp.clip(_dot(xb, w1l_ref[0]) + b1l_ref[0], -SWIGLU_LIMIT, SWIGLU_LIMIT)
            act_ref[f, rows, :] = (glu * _sigmoid(SWIGLU_ALPHA * glu) * (lin + 1.0)).astype(BF16)

    @pl.when(f == nf - 1)
    def _():
        @pl.when(i > 0)
        def _():
            wait_scatter(tv_ref[jnp.maximum(i - 1, 0)])

        dst_copy().wait()
        for s in range(nsub):
            @pl.when(s * sub < valid)
            def _():
                rows = slice(s * sub, (s + 1) * sub)
                fc = act_ref.shape[2]
                y = _dot(act_ref[0, rows, :], w2_ref[0, :fc, :])
                for c in range(1, nf):
                    y = y + _dot(act_ref[c, rows, :], w2_ref[0, c * fc:(c + 1) * fc, :])
                y = (y + b2_ref[0]) * wgt_ref[0, rows, :]
                _store_rows_tiled(ybuf, s * sub, sub, _pack_pairs(y[:, :half], y[:, half:]))

                def body(r, c):
                    pltpu.make_async_copy(ybuf.at[tile_of(r), :], out_hbm.at[tile_of(dst_smem[r]), :],
                                          sem_out).start()
                    return c

                lax.fori_loop(s * sub, (s + 1) * sub, body, 0, unroll=DMA_UNROLL)

        @pl.when(i == n_tiles - 1)
        def _():
            wait_scatter(valid)
            ybuf[...] = jnp.zeros_like(ybuf)
            dump = pltpu.make_async_copy(ybuf, out_hbm.at[pl.ds(out_hbm.shape[0] - tm * sl8, tm * sl8), :], sem_out)
            dump.start()
            dump.wait()


def _moe(tile_exp, tile_valid, tok, dst, hn, wgt, w1, b1, w2, b2, n_out_rows, tm, fc):
    n_tiles = tok.shape[0]
    D = w1.shape[1]
    assert D == 2 * SUBLANE * LANE
    dff = w2.shape[1]
    nf = dff // fc
    grid_spec = pltpu.PrefetchScalarGridSpec(
        num_scalar_prefetch=2,
        grid=(n_tiles, nf),
        in_specs=[
            pl.BlockSpec(memory_space=pl.ANY),
            pl.BlockSpec(memory_space=pl.ANY),
            pl.BlockSpec(memory_space=pl.ANY),
            pl.BlockSpec((1, tm, 1), lambda i, f, te, tv: (i, 0, 0)),
            pl.BlockSpec((1, D, fc), lambda i, f, te, tv: (te[i], 0, f)),
            pl.BlockSpec((1, D, fc), lambda i, f, te, tv: (te[i], 0, nf + f)),
            pl.BlockSpec((1, 1, fc), lambda i, f, te, tv: (te[i], 0, f)),
            pl.BlockSpec((1, 1, fc), lambda i, f, te, tv: (te[i], 0, nf + f)),
            pl.BlockSpec((1, dff, D), lambda i, f, te, tv: (te[i], 0, 0)),
            pl.BlockSpec((1, 1, D), lambda i, f, te, tv: (te[i], 0, 0)),
        ],
        out_specs=pl.BlockSpec(memory_space=pl.ANY),
        scratch_shapes=[
            pltpu.SMEM((2, tm), I32),
            pltpu.SMEM((tm,), I32),
            pltpu.VMEM((2, tm * SUBLANE, LANE), U32),
            pltpu.VMEM((tm, D), BF16),
            pltpu.VMEM((nf, tm, fc), BF16),
            pltpu.VMEM((tm * SUBLANE, LANE), U32),
            pltpu.SemaphoreType.DMA((2,)),
            pltpu.SemaphoreType.DMA(()),
            pltpu.SemaphoreType.DMA((2, tm // MOE_SUB)),
            pltpu.SemaphoreType.DMA(()),
        ],
    )
    return pl.pallas_call(
        functools.partial(_moe_kernel, tm=tm, nf=nf, n_tiles=n_tiles),
        grid_spec=grid_spec,
        out_shape=jax.ShapeDtypeStruct((n_out_rows * SUBLANE, LANE), U32),
        compiler_params=_params(("arbitrary", "arbitrary")),
    )(tile_exp, tile_valid, tok, dst, hn, wgt, w1, w1, b1, b1, w2, b2)


def _combine_kernel(h1_ref, y0_ref, y1_ref, y2_ref, y3_ref, g_ref, o_ref):
    h = h1_ref[...]
    tm = h.shape[0]
    for y_ref in (y0_ref, y1_ref, y2_ref, y3_ref):
        pairs = [_unpack_pairs(w) for w in _load_rows_tiled(y_ref, 0, tm)]
        h = h + jnp.concatenate([p[0] for p in pairs] + [p[1] for p in pairs], axis=1)
    o_ref[...] = _rms(h, g_ref[...])


def _combine(ys, h1, g, tm):
    T, D = h1.shape
    nb = T // tm
    slab = lambda k: pl.BlockSpec((tm * SUBLANE, LANE), lambda i, k=k: (k * nb + i, 0))
    return pl.pallas_call(
        _combine_kernel,
        grid=(nb,),
        in_specs=[pl.BlockSpec((tm, D), lambda i: (i, 0))] + [slab(k) for k in range(TOP_K)]
        + [pl.BlockSpec((1, D), lambda i: (0, 0))],
        out_specs=pl.BlockSpec((tm, D), lambda i: (i, 0)),
        out_shape=jax.ShapeDtypeStruct((T, D), F32),
        compiler_params=_params(("parallel",)),
    )(h1, ys, ys, ys, ys, g)


def _t5_bucket(dist):
    n = jnp.maximum(dist, 0)
    exact = N_BUCKETS // 2
    scaled = jnp.log(jnp.maximum(n, 1).astype(F32) / exact) / math.log(MAX_DISTANCE / exact)
    large = exact + (scaled * (N_BUCKETS - exact)).astype(I32)
    return jnp.where(n < exact, n, jnp.minimum(large, N_BUCKETS - 1)).astype(I32)


def _bias_tables(rel_bias, S):
    shifted = (rel_bias - rel_bias[N_BUCKETS - 1][None, :]).astype(F32)

    def lookup(dist):
        bucket = _t5_bucket(dist)
        out = jnp.zeros((NSA_HEADS,) + dist.shape, F32)
        for kb in range(N_BUCKETS - 1):
            out = jnp.where((bucket == kb)[None], shifted[kb].reshape((NSA_HEADS,) + (1,) * dist.ndim), out)
        return jnp.where((dist >= 0)[None], out, NEG)

    t = SEL_T
    r = jnp.arange(t)[:, None]
    c = jnp.arange(t)[None, :]
    dtiles = jnp.stack([lookup(r - c), lookup(t + r - c)], axis=1)
    w4 = jnp.where(r < c, 0.0, NEG).astype(F32)
    ncp = S // CMP_STRIDE
    dist_c = jnp.arange(S)[:, None] - (jnp.arange(ncp) * CMP_STRIDE + CMP_LEN - 1)[None, :]
    return dtiles, w4, lookup(dist_c)


def _overlap_t(S):
    ncp = S // CMP_STRIDE
    ns = S // SEL_BLOCK
    cs = jnp.arange(ncp)[None, :] * CMP_STRIDE
    ss = jnp.arange(ns)[:, None] * SEL_BLOCK
    ov = (cs < ss + SEL_BLOCK) & (cs + CMP_LEN > ss) & (jnp.arange(ncp)[None, :] < ncp - 1)
    return ov.astype(BF16)


def _route(top_idx, top_w, n_exp, tm):
    T = top_idx.shape[0]
    A = T * TOP_K
    e_flat = top_idx.reshape(A)
    w_flat = top_w.reshape(A)
    e_sorted, order = lax.sort_key_val(e_flat, jnp.arange(A, dtype=I32), is_stable=True)
    counts = jnp.sum((e_flat[:, None] == jnp.arange(n_exp, dtype=I32)[None, :]).astype(I32), axis=0)
    padded = (counts + tm - 1) // tm * tm
    start = jnp.cumsum(counts) - counts
    pend = jnp.cumsum(padded)
    pstart = pend - padded
    n_tiles = A // tm + n_exp
    tile_row0 = jnp.arange(n_tiles, dtype=I32) * tm
    tile_exp = jnp.minimum(jnp.sum((pend[None, :] <= tile_row0[:, None]).astype(I32), axis=1), n_exp - 1)
    tile_valid = jnp.clip(pstart[tile_exp] + counts[tile_exp] - tile_row0, 0, tm).astype(I32)
    off = tile_row0[:, None] - pstart[tile_exp][:, None] + jnp.arange(tm, dtype=I32)[None, :]
    live = jnp.arange(tm, dtype=I32)[None, :] < tile_valid[:, None]
    src = order[jnp.clip(start[tile_exp][:, None] + off, 0, A - 1)]
    buf_tok = jnp.where(live, src // TOP_K, 0)
    buf_w = jnp.where(live, w_flat[src], 0.0)
    buf_dst = jnp.where(live, (src % TOP_K) * T + src // TOP_K, TOP_K * T + jnp.arange(tm, dtype=I32)[None, :])
    return buf_tok, buf_dst, buf_w[:, :, None], tile_exp, tile_valid


def kernel(x, norm_mix_g, w_in, gla_wa2, gla_ba, gla_norm_g, cmp_pe_k, cmp_wk1, cmp_wk2, cmp_pe_v,
           cmp_wv1, cmp_wv2, nsa_norm_g, rel_bias, w_out, norm_ffn_g, router_w, router_b, exp_w1,
           exp_b1, exp_w2, exp_b2, norm_final_g):
    B, S, D = x.shape
    T = B * S
    G = NSA_KV
    n_exp = router_w.shape[-1]
    assert w_in.shape[0] == 1 and S % 1024 == 0 and S // SEL_BLOCK <= LANE and n_exp <= LANE
    x2 = x.reshape(T, D)

    w = w_in[0]
    a0 = COL_NQ
    n0 = a0 + GLA_RANK
    g0 = n0 + (N_MAIN - COL_NQ)
    w_main = jnp.concatenate([
        w[:, :COL_GK] * (GLA_DK ** -0.5), w[:, COL_GK:a0],
        w[:, n0:n0 + NSA_WIDTH] * (NSA_DH ** -0.5), w[:, n0 + NSA_WIDTH:g0]], axis=1).astype(BF16)
    w_nar = jnp.concatenate([w[:, a0:n0], w[:, g0:], jnp.zeros((D, LANE - GLA_RANK - 3 * NSA_HEADS), F32)],
                            axis=1).astype(BF16)
    wa2p = jnp.concatenate([gla_wa2[0], jnp.zeros((LANE - GLA_RANK, GLA_HEADS * GLA_DK), F32)], axis=0).astype(BF16)

    tm1 = min(1024, T)
    proj, log_a, nar = _inproj(x2, norm_mix_g, w_main, w_nar, wa2p, gla_ba, tm1, N_MAIN // 4)

    e_w1, e_w2 = exp_w1[0], exp_w2[0]
    o_gla = _gla(proj, log_a, gla_norm_g, B, S, min(256, S))

    ncp = S // CMP_STRIDE
    half = CMP_STRIDE * NSA_DH

    def stride_groups(col):
        t4 = proj[:, col:col + KV_W].reshape(B, ncp, CMP_STRIDE, G, NSA_DH)
        return t4.transpose(0, 3, 1, 2, 4).reshape(B * G, ncp, half)

    hb = jnp.stack([stride_groups(COL_KC), stride_groups(COL_VC)], axis=0)
    pe = jnp.stack([cmp_pe_k[0].reshape(2, half), cmp_pe_v[0].reshape(2, half)], axis=0)
    w1c = jnp.stack([cmp_wk1[0], cmp_wv1[0]], axis=0).astype(BF16)
    w2c = jnp.stack([cmp_wk2[0], cmp_wv2[0]], axis=0).astype(BF16)
    cmp_kv = _compress(hb, pe, w1c, w2c)

    dtiles, w4, bias_c = _bias_tables(rel_bias, S)
    o_cmp, sel = _cmpattn(proj, cmp_kv, bias_c, _overlap_t(S), B, S)
    o_sel, o_win, w1b, w2b = _selwin(proj, sel, dtiles, w4, e_w1.reshape(-1, e_w1.shape[-1]),
                                     e_w2.reshape(-1, e_w2.shape[-1]), B, S)
    w1b, w2b = w1b.reshape(e_w1.shape), w2b.reshape(e_w2.shape)

    rw = jnp.concatenate([router_w[0], jnp.zeros((D, LANE - n_exp), F32)], axis=1)
    rwh = rw.astype(BF16)
    rwl = (rw - rwh.astype(F32)).astype(BF16)
    rb = jnp.concatenate([router_b[0], jnp.full((LANE - n_exp,), NEG, F32)])[None, :]
    h1, hn, top_i, top_w = _outproj(x2, o_gla, o_cmp, o_sel, o_win, nar, nsa_norm_g, w_out[0].astype(BF16),
                                    norm_ffn_g, rwh, rwl, rb, min(512, T))

    tm6 = min(1024, T)
    tok, dst, wgt, tile_exp, tile_valid = _route(top_i[:, :TOP_K], top_w[:, :TOP_K], n_exp, tm6)
    ys = _moe(tile_exp, tile_valid, tok, dst, hn, wgt, w1b, exp_b1[0][:, None, :],
              w2b, exp_b2[0][:, None, :], TOP_K * T + tm6, tm6, 512)
    out = _combine(ys, h1, norm_final_g[None, :], min(512, T))
    return out.reshape(B, S, D)
```

```python
import functools
import math

import numpy as np
import jax
import jax.numpy as jnp
from jax import lax
from jax.experimental import pallas as pl
from jax.experimental.pallas import tpu as pltpu

F32 = jnp.float32
BF16 = jnp.bfloat16
I32 = jnp.int32
U32 = jnp.uint32

GLA_HEADS = 4
GLA_DK = 128
GLA_DV = 256
GLA_RANK = 16
GLA_TAU = 16.0
NSA_HEADS = 8
NSA_KV = 2
NSA_REP = NSA_HEADS // NSA_KV
NSA_DH = 128
CMP_LEN = 32
CMP_STRIDE = 16
SEL_BLOCK = 64
SEL_TOPN = 16
WINDOW = 512
N_BUCKETS = 32
MAX_DISTANCE = 128
TOP_K = 4
SWIGLU_ALPHA = 1.702
SWIGLU_LIMIT = 7.0
RMS_EPS = 1e-5
NEG = -1e30
FORCE_SCORE = 1e4
LOG2E = 1.4426950408889634

GLA_WIDTH = GLA_HEADS * GLA_DV
NSA_WIDTH = NSA_HEADS * NSA_DH
KV_W = NSA_KV * NSA_DH

COL_GQ = 0
COL_GK = COL_GQ + GLA_HEADS * GLA_DK
COL_GV = COL_GK + GLA_HEADS * GLA_DK
COL_GG = COL_GV + GLA_WIDTH
COL_NQ = COL_GG + GLA_WIDTH
COL_KC = COL_NQ + NSA_WIDTH
COL_VC = COL_KC + KV_W
COL_KS = COL_VC + KV_W
COL_VS = COL_KS + KV_W
COL_KW = COL_VS + KV_W
COL_VW = COL_KW + KV_W
N_MAIN = COL_VW + KV_W
LANE = 128
ATT_T = 256
SEL_T = 256
GLA_C = 64
GLA_SUB = 16
MOE_SUB = 512
DMA_UNROLL = 32
GATE_COL0 = GLA_RANK

VMEM_LIMIT = 56 * 1024 * 1024

NT = (((1,), (1,)), ((), ()))
TN = (((0,), (0,)), ((), ()))


def _dot(a, b):
    return jnp.dot(a, b, preferred_element_type=F32)


def _dg(a, b, dims):
    return lax.dot_general(a, b, dims, preferred_element_type=F32)


def _sigmoid(x):
    return 1.0 / (1.0 + jnp.exp(-x))


def _log_sigmoid(z):
    return -(jnp.maximum(-z, 0.0) + jnp.log(1.0 + jnp.exp(-jnp.abs(z))))


def _rms(x, g):
    return x * lax.rsqrt(jnp.mean(x * x, axis=-1, keepdims=True) + RMS_EPS) * g


def _split3(x):
    hi = x.astype(BF16)
    r1 = x - hi.astype(F32)
    mid = r1.astype(BF16)
    lo = (r1 - mid.astype(F32)).astype(BF16)
    return hi, mid, lo


def _pack_pairs(lo, hi):
    lo_bits = lax.bitcast_convert_type(lo.astype(BF16).astype(F32), U32) >> 16
    hi_bits = lax.bitcast_convert_type(hi.astype(BF16).astype(F32), U32) & jnp.uint32(0xFFFF0000)
    return lo_bits | hi_bits


def _unpack_pairs(w):
    lo = lax.bitcast_convert_type(w << 16, F32)
    hi = lax.bitcast_convert_type(w & jnp.uint32(0xFFFF0000), F32)
    return lo, hi


SUBLANE = 8


def _store_rows_tiled(ref, base, n, packed):
    for j in range(SUBLANE):
        ref[pl.ds(base * SUBLANE + j, n, stride=SUBLANE), :] = packed[:, j * LANE:(j + 1) * LANE]


def _load_rows_tiled(ref, base, n, lead=()):
    return [ref[lead + (pl.ds(base * SUBLANE + j, n, stride=SUBLANE), slice(None))] for j in range(SUBLANE)]


def _params(sem, vmem=VMEM_LIMIT):
    return pltpu.CompilerParams(dimension_semantics=sem, vmem_limit_bytes=vmem)


def _inproj_kernel(x_ref, g_ref, w_ref, wn_ref, wa2_ref, ba_ref, proj_ref, la_ref, nar_ref, xn_ref):
    @pl.when(pl.program_id(1) == 0)
    def _():
        xn = _rms(x_ref[...], g_ref[...]).astype(BF16)
        xn_ref[...] = xn
        nar = _dot(xn, wn_ref[...])
        nar_ref[...] = nar
        z = _dot(nar.astype(BF16), wa2_ref[...]) + ba_ref[...]
        la_ref[...] = _log_sigmoid(z) * (1.0 / GLA_TAU)

    proj_ref[...] = _dot(xn_ref[...], w_ref[...]).astype(BF16)


def _inproj(x2, g, w_main, w_nar, wa2p, ba, tm, tn):
    T, D = x2.shape
    nj = N_MAIN // tn
    return pl.pallas_call(
        _inproj_kernel,
        grid=(T // tm, nj),
        in_specs=[
            pl.BlockSpec((tm, D), lambda i, j: (i, 0)),
            pl.BlockSpec((1, D), lambda i, j: (0, 0)),
            pl.BlockSpec((D, tn), lambda i, j: (0, j)),
            pl.BlockSpec((D, LANE), lambda i, j: (0, 0)),
            pl.BlockSpec((LANE, GLA_HEADS * GLA_DK), lambda i, j: (0, 0)),
            pl.BlockSpec((1, GLA_HEADS * GLA_DK), lambda i, j: (0, 0)),
        ],
        out_specs=[
            pl.BlockSpec((tm, tn), lambda i, j: (i, j)),
            pl.BlockSpec((tm, GLA_HEADS * GLA_DK), lambda i, j: (i, 0)),
            pl.BlockSpec((tm, LANE), lambda i, j: (i, 0)),
        ],
        out_shape=[
            jax.ShapeDtypeStruct((T, N_MAIN), BF16),
            jax.ShapeDtypeStruct((T, GLA_HEADS * GLA_DK), F32),
            jax.ShapeDtypeStruct((T, LANE), F32),
        ],
        scratch_shapes=[pltpu.VMEM((tm, D), BF16)],
        compiler_params=_params(("parallel", "arbitrary")),
    )(x2, g, w_main, w_nar, wa2p, ba)


def _gla_kernel(q_ref, k_ref, v_ref, g_ref, la_ref, ng_ref, o_ref, st_ref, *, cb):
    @pl.when(pl.program_id(1) == 0)
    def _():
        st_ref[...] = jnp.zeros_like(st_ref)

    C, SUB = GLA_C, GLA_SUB
    heads = range(GLA_HEADS)
    dk = lambda h: slice(h * GLA_DK, (h + 1) * GLA_DK)
    dv = lambda h: slice(h * GLA_DV, (h + 1) * GLA_DV)
    ri = lax.broadcasted_iota(I32, (C, C), 0)
    ci = lax.broadcasted_iota(I32, (C, C), 1)
    tri = jnp.where(ri >= ci, 1.0, 0.0).astype(BF16)
    nsub = C // SUB
    sub_of_row = lax.broadcasted_iota(I32, (C, GLA_DK), 0) // SUB
    for n in range(cb // C):
        rows = slice(n * C, (n + 1) * C)
        b = []
        for h in heads:
            hi, mid, lo = _split3(la_ref[rows, dk(h)])
            b.append(_dot(tri, hi) + _dot(tri, mid) + _dot(tri, lo))
        q = [q_ref[rows, dk(h)].astype(F32) for h in heads]
        k = [k_ref[rows, dk(h)].astype(F32) for h in heads]
        v = [v_ref[rows, dv(h)] for h in heads]
        st = [st_ref[h] for h in heads]
        o = [_dg((q[h] * jnp.exp(b[h])).astype(BF16), st[h].astype(BF16), NT) for h in heads]
        intra = []
        for h in heads:
            refs = [jnp.zeros((1, GLA_DK), F32)] + [b[h][s * SUB - 1:s * SUB, :] for s in range(1, nsub)]
            ref_row = jnp.concatenate([jnp.broadcast_to(r, (SUB, GLA_DK)) for r in refs], axis=0)
            qsc = q[h] * jnp.exp(b[h] - ref_row)
            q_cat = jnp.concatenate([jnp.where(sub_of_row == s, qsc, 0.0) for s in range(nsub)], axis=1)
            k_cat = jnp.concatenate([jnp.where(sub_of_row <= s, k[h] * jnp.exp(refs[s] - b[h]), 0.0)
                                     for s in range(nsub)], axis=1)
            sc = jnp.where(ci <= ri, _dg(q_cat.astype(BF16), k_cat.astype(BF16), NT), 0.0)
            intra.append(_dot(sc.astype(BF16), v[h]))
        for h in heads:
            bl = b[h][C - 1:C, :]
            kdec = (k[h] * jnp.exp(bl - b[h])).astype(BF16)
            st_ref[h] = st[h] * jnp.exp(bl) + _dg(v[h], kdec, TN)
        for h in heads:
            oh = o[h] + intra[h]
            gg = g_ref[rows, dv(h)].astype(F32)
            o_ref[rows, dv(h)] = (_rms(oh, ng_ref[...]) * (gg * _sigmoid(gg))).astype(BF16)


def _cast_slab(w2d, nsteps):
    rows, width = w2d.shape
    assert rows % nsteps == 0 and (rows // nsteps) % 16 == 0
    return rows // nsteps, width


def _gla(proj, la, ng, B, S, cb):
    T = B * S
    ncb = S // cb
    dkw = GLA_HEADS * GLA_DK
    row = lambda b, c: b * ncb + c
    return pl.pallas_call(
        functools.partial(_gla_kernel, cb=cb),
        grid=(B, ncb),
        in_specs=[
            pl.BlockSpec((cb, dkw), lambda b, c: (row(b, c), COL_GQ // dkw)),
            pl.BlockSpec((cb, dkw), lambda b, c: (row(b, c), COL_GK // dkw)),
            pl.BlockSpec((cb, GLA_WIDTH), lambda b, c: (row(b, c), COL_GV // GLA_WIDTH)),
            pl.BlockSpec((cb, GLA_WIDTH), lambda b, c: (row(b, c), COL_GG // GLA_WIDTH)),
            pl.BlockSpec((cb, dkw), lambda b, c: (row(b, c), 0)),
            pl.BlockSpec((1, GLA_DV), lambda b, c: (0, 0)),
        ],
        out_specs=pl.BlockSpec((cb, GLA_WIDTH), lambda b, c: (row(b, c), 0)),
        out_shape=jax.ShapeDtypeStruct((T, GLA_WIDTH), BF16),
        scratch_shapes=[pltpu.VMEM((GLA_HEADS, GLA_DV, GLA_DK), F32)],
        compiler_params=_params(("parallel", "arbitrary")),
    )(proj, proj, proj, proj, la, ng)


def _compress_kernel(hb_ref, pe_ref, w1_ref, w2_ref, o_ref, *, ncp):
    half = CMP_STRIDE * NSA_DH
    hb = hb_ref[0, 0].astype(F32)
    pe = pe_ref[0]
    top = (hb + pe[0:1]).astype(BF16)
    bot = (hb + pe[1:2]).astype(BF16)
    a = _dot(top, w1_ref[0, :half, :])
    bm = _dot(bot, w1_ref[0, half:, :])
    pre = a + pltpu.roll(bm, ncp - 1, 0)
    cdf = 0.5 * (1.0 + jnp.tanh(math.sqrt(2.0 / math.pi) * (pre + 0.044715 * (pre * pre * pre))))
    o_ref[0, 0] = _dot((pre * cdf).astype(BF16), w2_ref[0]).astype(BF16)


def _compress(hb, pe, w1, w2):
    _, BG, ncp, half = hb.shape
    return pl.pallas_call(
        functools.partial(_compress_kernel, ncp=ncp),
        grid=(2, BG),
        in_specs=[
            pl.BlockSpec((1, 1, ncp, half), lambda t, n: (t, n, 0, 0)),
            pl.BlockSpec((1, 2, half), lambda t, n: (t, 0, 0)),
            pl.BlockSpec((1, 2 * half, NSA_DH), lambda t, n: (t, 0, 0)),
            pl.BlockSpec((1, NSA_DH, NSA_DH), lambda t, n: (t, 0, 0)),
        ],
        out_specs=pl.BlockSpec((1, 1, ncp, NSA_DH), lambda t, n: (t, n, 0, 0)),
        out_shape=jax.ShapeDtypeStruct((2, BG, ncp, NSA_DH), BF16),
        compiler_params=_params(("parallel", "parallel")),
    )(hb, pe, w1, w2)


def _cmpattn_kernel(q_ref, kc_ref, vc_ref, bias_ref, ovt_ref, o_ref, sel_ref, *, ns, ncp):
    tq = ATT_T
    i = pl.program_id(1)
    kc = kc_ref[0, 0]
    vc = vc_ref[0, 0]
    pos = i * tq + lax.broadcasted_iota(I32, (tq, 1), 0)
    row_valid = jnp.where(pos >= CMP_LEN - 1, 1.0, 0.0)
    psum = jnp.zeros((tq, ncp), F32)
    outs = []
    for r in range(NSA_REP):
        q = q_ref[:, r * NSA_DH:(r + 1) * NSA_DH]
        s = _dg(q, kc, NT) + bias_ref[r]
        e = jnp.exp2(s - jnp.max(s, axis=-1, keepdims=True))
        p = e / jnp.sum(e, axis=-1, keepdims=True) * row_valid
        psum = psum + p
        outs.append(_dot(p.astype(BF16), vc))
    o_ref[...] = jnp.concatenate(outs, axis=1).astype(BF16)

    ph = psum.astype(BF16)
    pl_ = (psum - ph.astype(F32)).astype(BF16)
    imp = _dg(ovt_ref[...], ph, NT) + _dg(ovt_ref[...], pl_, NT)
    n_i = lax.broadcasted_iota(I32, (ns, tq), 0)
    cur = (i * tq + lax.broadcasted_iota(I32, (ns, tq), 1)) // SEL_BLOCK
    forced = (n_i == 0) | (n_i == cur) | (n_i == cur - 1)
    score = jnp.where(forced, FORCE_SCORE, jnp.where(n_i <= cur, imp, -1.0))
    cnt = jnp.zeros((ns, tq), F32)
    for m in range(ns):
        sm = score[m:m + 1, :]
        ge = jnp.where(sm >= score, 1.0, 0.0)
        gt = jnp.where(sm > score, 1.0, 0.0)
        cnt = cnt + jnp.where(n_i > m, ge, gt)
    sel_t = jnp.where(cnt < min(SEL_TOPN, ns), 1.0, 0.0)
    if ns < LANE:
        sel_t = jnp.concatenate([sel_t, jnp.zeros((LANE - ns, tq), F32)], axis=0)
    sel_ref[0] = sel_t.T.astype(BF16)


def _cmpattn(proj, cmp_kv, bias_c, ovt, B, S):
    T = B * S
    tq = ATT_T
    nq = S // tq
    ns = S // SEL_BLOCK
    ncp = S // CMP_STRIDE
    G = NSA_KV
    qcol = COL_NQ // (NSA_REP * NSA_DH)
    return pl.pallas_call(
        functools.partial(_cmpattn_kernel, ns=ns, ncp=ncp),
        grid=(G, nq, B),
        in_specs=[
            pl.BlockSpec((tq, NSA_REP * NSA_DH), lambda g, i, b: (b * nq + i, qcol + g)),
            pl.BlockSpec((1, 1, ncp, NSA_DH), lambda g, i, b: (0, b * G + g, 0, 0)),
            pl.BlockSpec((1, 1, ncp, NSA_DH), lambda g, i, b: (1, b * G + g, 0, 0)),
            pl.BlockSpec((NSA_REP, tq, ncp), lambda g, i, b: (g, i, 0)),
            pl.BlockSpec((ns, ncp), lambda g, i, b: (0, 0)),
        ],
        out_specs=[
            pl.BlockSpec((tq, NSA_REP * NSA_DH), lambda g, i, b: (b * nq + i, g)),
            pl.BlockSpec((1, tq, LANE), lambda g, i, b: (b * G + g, i, 0)),
        ],
        out_shape=[
            jax.ShapeDtypeStruct((T, NSA_WIDTH), BF16),
            jax.ShapeDtypeStruct((B * G, S, LANE), BF16),
        ],
        compiler_params=_params(("parallel", "parallel", "parallel")),
    )(proj, cmp_kv, cmp_kv, bias_c, ovt)


def _flash_update(s, v, m_ref, al_ref):
    reps = s.shape[1] // LANE
    m_old = m_ref[...]
    part = s[:, :LANE]
    for c in range(1, reps):
        part = jnp.maximum(part, s[:, c * LANE:(c + 1) * LANE])
    m_new = jnp.maximum(m_old, jnp.max(part, axis=-1, keepdims=True))
    alpha = jnp.exp2(m_old - m_new)
    p = jnp.exp2(s - jnp.concatenate([m_new] * reps, axis=1))
    v_ext = jnp.concatenate([v, jnp.ones(v.shape, BF16)], axis=1)
    al_ref[...] = jnp.concatenate([alpha, alpha], axis=1) * al_ref[...] + _dot(p.astype(BF16), v_ext)
    m_ref[...] = m_new


def _selwin_kernel(q_ref, ks_ref, vs_ref, kw_ref, vw_ref, sel_ref, d_ref, w4_ref, wf1_ref, wf2_ref,
                   osel_ref, owin_ref, wb1_ref, wb2_ref, m_ref, al_ref, s_ref):
    wb1_ref[...] = wf1_ref[...].astype(BF16)
    wb2_ref[...] = wf2_ref[...].astype(BF16)
    t = SEL_T
    R = NSA_REP
    i = pl.program_id(2)
    q = jnp.concatenate([q_ref[:, r * NSA_DH:(r + 1) * NSA_DH] for r in range(R)], axis=0)
    unsel = (1.0 - sel_ref[0].astype(F32)).astype(BF16)
    q_sel = jnp.concatenate([q, jnp.concatenate([unsel] * R, axis=0)], axis=1)
    c_blk = lax.broadcasted_iota(I32, (t, LANE), 0) // SEL_BLOCK
    n_i = lax.broadcasted_iota(I32, (t, LANE), 1)
    blk_per_tile = t // SEL_BLOCK

    def init():
        m_ref[...] = jnp.full(m_ref.shape, NEG, F32)
        al_ref[...] = jnp.zeros(al_ref.shape, F32)

    def tile(ref, j):
        return ref[pl.ds(pl.multiple_of(j * t, t), t), :]

    def logits(k_ref, j, add):
        s = _dg(q, tile(k_ref, j), NT).reshape(R, t, t)
        return (s + add).reshape(R * t, t)

    def sel_logits(j):
        own_block = jnp.where(n_i == blk_per_tile * j + c_blk, NEG, 0.0).astype(BF16)
        return _dg(q_sel, jnp.concatenate([tile(ks_ref, j), own_block], axis=1), NT)

    def write(o_ref, al):
        o = al[:, :NSA_DH] / al[:, NSA_DH:]
        o_ref[...] = jnp.concatenate([o[r * t:(r + 1) * t] for r in range(R)], axis=1).astype(BF16)

    init()

    jn = jnp.maximum(i - 1, 0)
    s_ref[...] = sel_logits(0)

    def far(j, c):
        s = s_ref[...]
        s_ref[...] = sel_logits(j + 1)
        _flash_update(s, tile(vs_ref, j), m_ref, al_ref)
        return c

    def with_bias(s, add):
        return (s.reshape(R, t, t) + add).reshape(R * t, t)

    lax.fori_loop(0, jn, far, 0)
    off = jnp.where(i == 0, NEG, 0.0)
    s_near = s_ref[...]
    s_diag = sel_logits(i)
    _flash_update(with_bias(s_near, d_ref[:, 1] + off), tile(vs_ref, jn), m_ref, al_ref)
    _flash_update(with_bias(s_diag, d_ref[:, 0]), tile(vs_ref, i), m_ref, al_ref)
    write(osel_ref, al_ref[...])

    init()
    pre = WINDOW // t
    for o in range(pre, -1, -1):
        jw = i - o
        jc = jnp.maximum(jw, 0)
        off = jnp.where(jw < 0, NEG, 0.0)
        if o == 0:
            add = d_ref[:, 0] + off
        elif o == 1:
            add = d_ref[:, 1] + off
        elif o == pre:
            add = (w4_ref[...] + off)[None]
        else:
            add = off
        _flash_update(logits(kw_ref, jc, add), tile(vw_ref, jc), m_ref, al_ref)
    write(owin_ref, al_ref[...])


def _selwin(proj, sel, dtiles, w4, wf1, wf2, B, S):
    T = B * S
    t = SEL_T
    nq = S // t
    G = NSA_KV
    qcol = COL_NQ // (NSA_REP * NSA_DH)
    kv = lambda col: pl.BlockSpec((S, NSA_DH), lambda b, g, i, col=col: (b, col // NSA_DH + g))
    slab = lambda wf: pl.BlockSpec(_cast_slab(wf, B * G * nq), lambda b, g, i: ((b * G + g) * nq + i, 0))
    return pl.pallas_call(
        _selwin_kernel,
        grid=(B, G, nq),
        in_specs=[
            pl.BlockSpec((t, NSA_REP * NSA_DH), lambda b, g, i: (b * nq + i, qcol + g)),
            kv(COL_KS), kv(COL_VS), kv(COL_KW), kv(COL_VW),
            pl.BlockSpec((1, t, LANE), lambda b, g, i: (b * G + g, i, 0)),
            pl.BlockSpec((NSA_REP, 2, t, t), lambda b, g, i: (g, 0, 0, 0)),
            pl.BlockSpec((t, t), lambda b, g, i: (0, 0)),
            slab(wf1), slab(wf2),
        ],
        out_specs=[
            pl.BlockSpec((t, NSA_REP * NSA_DH), lambda b, g, i: (b * nq + i, g)),
            pl.BlockSpec((t, NSA_REP * NSA_DH), lambda b, g, i: (b * nq + i, g)),
            slab(wf1), slab(wf2),
        ],
        out_shape=[
            jax.ShapeDtypeStruct((T, NSA_WIDTH), BF16),
            jax.ShapeDtypeStruct((T, NSA_WIDTH), BF16),
            jax.ShapeDtypeStruct(wf1.shape, BF16),
            jax.ShapeDtypeStruct(wf2.shape, BF16),
        ],
        scratch_shapes=[
            pltpu.VMEM((NSA_REP * t, LANE), F32),
            pltpu.VMEM((NSA_REP * t, 2 * NSA_DH), F32),
            pltpu.VMEM((NSA_REP * t, t), F32),
        ],
        compiler_params=_params(("parallel", "parallel", "parallel")),
    )(proj, proj, proj, proj, proj, sel, dtiles, w4, wf1, wf2)


def _outproj_kernel(x_ref, og_ref, oc_ref, os_ref, ow_ref, nar_ref, ng_ref, wo_ref, fg_ref,
                    rwh_ref, rwl_ref, rb_ref, h1_ref, hn_ref, ti_ref, tw_ref):
    gates = _sigmoid(nar_ref[...])
    parts = []
    for h in range(NSA_HEADS):
        cs = slice(h * NSA_DH, (h + 1) * NSA_DH)
        g0 = gates[:, GATE_COL0 + h:GATE_COL0 + h + 1]
        g1 = gates[:, GATE_COL0 + NSA_HEADS + h:GATE_COL0 + NSA_HEADS + h + 1]
        g2 = gates[:, GATE_COL0 + 2 * NSA_HEADS + h:GATE_COL0 + 2 * NSA_HEADS + h + 1]
        parts.append(g0 * oc_ref[:, cs].astype(F32) + g1 * os_ref[:, cs].astype(F32)
                     + g2 * ow_ref[:, cs].astype(F32))
    on = _rms(jnp.concatenate(parts, axis=1), ng_ref[...]).astype(BF16)
    h1 = x_ref[...] + _dot(og_ref[...], wo_ref[:GLA_WIDTH, :]) + _dot(on, wo_ref[GLA_WIDTH:, :])
    h1_ref[...] = h1
    hn = _rms(h1, fg_ref[...])
    half = hn.shape[1] // 2
    _store_rows_tiled(hn_ref, 0, hn.shape[0], _pack_pairs(hn[:, :half], hn[:, half:]))

    hh = hn.astype(BF16)
    hl = (hn - hh.astype(F32)).astype(BF16)
    both = _dot(hh, jnp.concatenate([rwh_ref[...], rwl_ref[...]], axis=1))
    lg = both[:, :LANE] + both[:, LANE:] + _dot(hl, rwh_ref[...]) + rb_ref[...]
    lane = lax.broadcasted_iota(I32, lg.shape, 1)
    vals, idxs = [], []
    for _ in range(TOP_K):
        mx = jnp.max(lg, axis=-1, keepdims=True)
        ix = jnp.min(jnp.where(lg == mx, lane, LANE), axis=-1, keepdims=True)
        vals.append(mx)
        idxs.append(ix)
        lg = jnp.where(lane == ix, -3e38, lg)
    es = [jnp.exp(v - vals[0]) for v in vals]
    den = es[0] + es[1] + es[2] + es[3]
    ti = jnp.zeros(lg.shape, I32)
    tw = jnp.zeros(lg.shape, F32)
    for kk in range(TOP_K):
        ti = jnp.where(lane == kk, idxs[kk], ti)
        tw = jnp.where(lane == kk, es[kk] / den, tw)
    ti_ref[...] = ti
    tw_ref[...] = tw


def _outproj(x2, og, oc, osel, ow, nar, ng, wo, fg, rwh, rwl, rb, tm):
    T, D = x2.shape
    row = lambda w: pl.BlockSpec((tm, w), lambda i: (i, 0))
    full = lambda a: pl.BlockSpec(a.shape, lambda i: (0,) * a.ndim)
    return pl.pallas_call(
        _outproj_kernel,
        grid=(T // tm,),
        in_specs=[row(D), row(GLA_WIDTH), row(NSA_WIDTH), row(NSA_WIDTH), row(NSA_WIDTH), row(LANE),
                  full(ng), full(wo), full(fg), full(rwh), full(rwl), full(rb)],
        out_specs=[row(D), pl.BlockSpec((tm * SUBLANE, LANE), lambda i: (i, 0)), row(LANE), row(LANE)],
        out_shape=[
            jax.ShapeDtypeStruct((T, D), F32),
            jax.ShapeDtypeStruct((T * SUBLANE, LANE), U32),
            jax.ShapeDtypeStruct((T, LANE), I32),
            jax.ShapeDtypeStruct((T, LANE), F32),
        ],
        compiler_params=_params(("parallel",)),
    )(x2, og, oc, osel, ow, nar, ng, wo, fg, rwh, rwl, rb)


def _moe_kernel(te_ref, tv_ref, tok_hbm, dst_hbm, hn_hbm, wgt_ref, w1g_ref, w1l_ref, b1g_ref, b1l_ref,
                w2_ref, b2_ref, out_hbm, tok_smem, dst_smem, xbuf, xb_ref, act_ref, ybuf,
                sem_tok, sem_dst, sem_in, sem_out, *, tm, nf, n_tiles):
    i = pl.program_id(0)
    f = pl.program_id(1)
    slot = i % 2
    valid = tv_ref[i]
    sub = MOE_SUB
    nsub = tm // sub
    half = SUBLANE * LANE
    sl8 = SUBLANE

    def tile_of(r):
        return pl.ds(pl.multiple_of(r * sl8, sl8), sl8)

    def tok_copy(t, sl):
        return pltpu.make_async_copy(tok_hbm.at[t], tok_smem.at[sl], sem_tok.at[sl])

    def dst_copy():
        return pltpu.make_async_copy(dst_hbm.at[i], dst_smem, sem_dst)

    def issue_gather(t, sl):
        for s in range(nsub):
            @pl.when(s * sub < tv_ref[t])
            def _():
                def body(r, c):
                    pltpu.make_async_copy(hn_hbm.at[tile_of(tok_smem[sl, r]), :],
                                          xbuf.at[sl, tile_of(r), :], sem_in.at[sl, s]).start()
                    return c

                lax.fori_loop(s * sub, (s + 1) * sub, body, 0, unroll=DMA_UNROLL)

    def wait_gather(sl, s):
        pltpu.make_async_copy(hn_hbm.at[pl.ds(0, sub * sl8), :], xbuf.at[sl, pl.ds(s * sub * sl8, sub * sl8), :],
                              sem_in.at[sl, s]).wait()

    def wait_scatter(count):
        for s in range(nsub):
            @pl.when(s * sub < count)
            def _():
                pltpu.make_async_copy(ybuf.at[pl.ds(s * sub * sl8, sub * sl8), :],
                                      out_hbm.at[pl.ds(0, sub * sl8), :], sem_out).wait()

    @pl.when(f == 0)
    def _():
        @pl.when(i == 0)
        def _():
            cp = tok_copy(0, 0)
            cp.start()
            cp.wait()
            issue_gather(0, 0)

        @pl.when(i + 1 < n_tiles)
        def _():
            tok_copy(i + 1, 1 - slot).start()

        dst_copy().start()
        for s in range(nsub):
            @pl.when(s * sub < valid)
            def _():
                wait_gather(slot, s)
                for j, w in enumerate(_load_rows_tiled(xbuf, s * sub, sub, lead=(slot,))):
                    lo, hi = _unpack_pairs(w)
                    xb_ref[s * sub:(s + 1) * sub, j * LANE:(j + 1) * LANE] = lo.astype(BF16)
                    xb_ref[s * sub:(s + 1) * sub, half + j * LANE:half + (j + 1) * LANE] = hi.astype(BF16)

        @pl.when(i + 1 < n_tiles)
        def _():
            tok_copy(i + 1, 1 - slot).wait()
            issue_gather(i + 1, 1 - slot)

    for s in range(nsub):
        @pl.when(s * sub < valid)
        def _():
            rows = slice(s * sub, (s + 1) * sub)
            xb = xb_ref[rows, :]
            glu = jnp.minimum(_dot(xb, w1g_ref[0]) + b1g_ref[0], SWIGLU_LIMIT)
            lin = jnp.clip(_dot(xb, w1l_ref[0]) + b1l_ref[0], -SWIGLU_LIMIT, SWIGLU_LIMIT)
            act_ref[f, rows, :] = (glu * _sigmoid(SWIGLU_ALPHA * glu) * (lin + 1.0)).astype(BF16)

    @pl.when(f == nf - 1)
    def _():
        @pl.when(i > 0)
        def _():
            wait_scatter(tv_ref[jnp.maximum(i - 1, 0)])

        dst_copy().wait()
        for s in range(nsub):
            @pl.when(s * sub < valid)
            def _():
                rows = slice(s * sub, (s + 1) * sub)
                fc = act_ref.shape[2]
                y = _dot(act_ref[0, rows, :], w2_ref[0, :fc, :])
                for c in range(1, nf):
                    y = y + _dot(act_ref[c, rows, :], w2_ref[0, c * fc:(c + 1) * fc, :])
                y = (y + b2_ref[0]) * wgt_ref[0, rows, :]
                _store_rows_tiled(ybuf, s * sub, sub, _pack_pairs(y[:, :half], y[:, half:]))

                def body(r, c):
                    pltpu.make_async_copy(ybuf.at[tile_of(r), :], out_hbm.at[tile_of(dst_smem[r]), :],
                                          sem_out).start()
                    return c

                lax.fori_loop(s * sub, (s + 1) * sub, body, 0, unroll=DMA_UNROLL)

        @pl.when(i == n_tiles - 1)
        def _():
            wait_scatter(valid)
            ybuf[...] = jnp.zeros_like(ybuf)
            dump = pltpu.make_async_copy(ybuf, out_hbm.at[pl.ds(out_hbm.shape[0] - tm * sl8, tm * sl8), :], sem_out)
            dump.start()
            dump.wait()


def _moe(tile_exp, tile_valid, tok, dst, hn, wgt, w1, b1, w2, b2, n_out_rows, tm, fc):
    n_tiles = tok.shape[0]
    D = w1.shape[1]
    assert D == 2 * SUBLANE * LANE
    dff = w2.shape[1]
    nf = dff // fc
    grid_spec = pltpu.PrefetchScalarGridSpec(
        num_scalar_prefetch=2,
        grid=(n_tiles, nf),
        in_specs=[
            pl.BlockSpec(memory_space=pl.ANY),
            pl.BlockSpec(memory_space=pl.ANY),
            pl.BlockSpec(memory_space=pl.ANY),
            pl.BlockSpec((1, tm, 1), lambda i, f, te, tv: (i, 0, 0)),
            pl.BlockSpec((1, D, fc), lambda i, f, te, tv: (te[i], 0, f)),
            pl.BlockSpec((1, D, fc), lambda i, f, te, tv: (te[i], 0, nf + f)),
            pl.BlockSpec((1, 1, fc), lambda i, f, te, tv: (te[i], 0, f)),
            pl.BlockSpec((1, 1, fc), lambda i, f, te, tv: (te[i], 0, nf + f)),
            pl.BlockSpec((1, dff, D), lambda i, f, te, tv: (te[i], 0, 0)),
            pl.BlockSpec((1, 1, D), lambda i, f, te, tv: (te[i], 0, 0)),
        ],
        out_specs=pl.BlockSpec(memory_space=pl.ANY),
        scratch_shapes=[
            pltpu.SMEM((2, tm), I32),
            pltpu.SMEM((tm,), I32),
            pltpu.VMEM((2, tm * SUBLANE, LANE), U32),
            pltpu.VMEM((tm, D), BF16),
            pltpu.VMEM((nf, tm, fc), BF16),
            pltpu.VMEM((tm * SUBLANE, LANE), U32),
            pltpu.SemaphoreType.DMA((2,)),
            pltpu.SemaphoreType.DMA(()),
            pltpu.SemaphoreType.DMA((2, tm // MOE_SUB)),
            pltpu.SemaphoreType.DMA(()),
        ],
    )
    return pl.pallas_call(
        functools.partial(_moe_kernel, tm=tm, nf=nf, n_tiles=n_tiles),
        grid_spec=grid_spec,
        out_shape=jax.ShapeDtypeStruct((n_out_rows * SUBLANE, LANE), U32),
        compiler_params=_params(("arbitrary", "arbitrary")),
    )(tile_exp, tile_valid, tok, dst, hn, wgt, w1, w1, b1, b1, w2, b2)


def _combine_kernel(h1_ref, y0_ref, y1_ref, y2_ref, y3_ref, g_ref, o_ref):
    h = h1_ref[...]
    tm = h.shape[0]
    for y_ref in (y0_ref, y1_ref, y2_ref, y3_ref):
        pairs = [_unpack_pairs(w) for w in _load_rows_tiled(y_ref, 0, tm)]
        h = h + jnp.concatenate([p[0] for p in pairs] + [p[1] for p in pairs], axis=1)
    o_ref[...] = _rms(h, g_ref[...])


def _combine(ys, h1, g, tm):
    T, D = h1.shape
    nb = T // tm
    slab = lambda k: pl.BlockSpec((tm * SUBLANE, LANE), lambda i, k=k: (k * nb + i, 0))
    return pl.pallas_call(
        _combine_kernel,
        grid=(nb,),
        in_specs=[pl.BlockSpec((tm, D), lambda i: (i, 0))] + [slab(k) for k in range(TOP_K)]
        + [pl.BlockSpec((1, D), lambda i: (0, 0))],
        out_specs=pl.BlockSpec((tm, D), lambda i: (i, 0)),
        out_shape=jax.ShapeDtypeStruct((T, D), F32),
        compiler_params=_params(("parallel",)),
    )(h1, ys, ys, ys, ys, g)


def _t5_bucket(dist):
    n = jnp.maximum(dist, 0)
    exact = N_BUCKETS // 2
    scaled = jnp.log(jnp.maximum(n, 1).astype(F32) / exact) / math.log(MAX_DISTANCE / exact)
    large = exact + (scaled * (N_BUCKETS - exact)).astype(I32)
    return jnp.where(n < exact, n, jnp.minimum(large, N_BUCKETS - 1)).astype(I32)


def _bias_tables(rel_bias, S):
    shifted = ((rel_bias - rel_bias[N_BUCKETS - 1][None, :]) * LOG2E).astype(F32)

    def lookup(dist):
        bucket = _t5_bucket(dist)
        out = jnp.zeros((NSA_HEADS,) + dist.shape, F32)
        for kb in range(N_BUCKETS - 1):
            out = jnp.where((bucket == kb)[None], shifted[kb].reshape((NSA_HEADS,) + (1,) * dist.ndim), out)
        return jnp.where((dist >= 0)[None], out, NEG)

    t = SEL_T
    r = jnp.arange(t)[:, None]
    c = jnp.arange(t)[None, :]
    dtiles = jnp.stack([lookup(r - c), lookup(t + r - c)], axis=1)
    w4 = jnp.where(r < c, 0.0, NEG).astype(F32)
    ncp = S // CMP_STRIDE
    dist_c = jnp.arange(S)[:, None] - (jnp.arange(ncp) * CMP_STRIDE + CMP_LEN - 1)[None, :]
    return dtiles, w4, lookup(dist_c)


def _overlap_t(S):
    ncp = S // CMP_STRIDE
    ns = S // SEL_BLOCK
    cs = jnp.arange(ncp)[None, :] * CMP_STRIDE
    ss = jnp.arange(ns)[:, None] * SEL_BLOCK
    ov = (cs < ss + SEL_BLOCK) & (cs + CMP_LEN > ss) & (jnp.arange(ncp)[None, :] < ncp - 1)
    return ov.astype(BF16)


def _route(top_idx, top_w, n_exp, tm):
    T = top_idx.shape[0]
    A = T * TOP_K
    e_flat = top_idx.reshape(A)
    w_flat = top_w.reshape(A)
    e_sorted, order = lax.sort_key_val(e_flat, jnp.arange(A, dtype=I32), is_stable=True)
    counts = jnp.sum((e_flat[:, None] == jnp.arange(n_exp, dtype=I32)[None, :]).astype(I32), axis=0)
    padded = (counts + tm - 1) // tm * tm
    start = jnp.cumsum(counts) - counts
    pend = jnp.cumsum(padded)
    pstart = pend - padded
    n_tiles = A // tm + n_exp
    tile_row0 = jnp.arange(n_tiles, dtype=I32) * tm
    tile_exp = jnp.minimum(jnp.sum((pend[None, :] <= tile_row0[:, None]).astype(I32), axis=1), n_exp - 1)
    tile_valid = jnp.clip(pstart[tile_exp] + counts[tile_exp] - tile_row0, 0, tm).astype(I32)
    off = tile_row0[:, None] - pstart[tile_exp][:, None] + jnp.arange(tm, dtype=I32)[None, :]
    live = jnp.arange(tm, dtype=I32)[None, :] < tile_valid[:, None]
    src = order[jnp.clip(start[tile_exp][:, None] + off, 0, A - 1)]
    buf_tok = jnp.where(live, src // TOP_K, 0)
    buf_w = jnp.where(live, w_flat[src], 0.0)
    buf_dst = jnp.where(live, (src % TOP_K) * T + src // TOP_K, TOP_K * T + jnp.arange(tm, dtype=I32)[None, :])
    return buf_tok, buf_dst, buf_w[:, :, None], tile_exp, tile_valid


def kernel(x, norm_mix_g, w_in, gla_wa2, gla_ba, gla_norm_g, cmp_pe_k, cmp_wk1, cmp_wk2, cmp_pe_v,
           cmp_wv1, cmp_wv2, nsa_norm_g, rel_bias, w_out, norm_ffn_g, router_w, router_b, exp_w1,
           exp_b1, exp_w2, exp_b2, norm_final_g):
    B, S, D = x.shape
    T = B * S
    G = NSA_KV
    n_exp = router_w.shape[-1]
    assert w_in.shape[0] == 1 and S % 1024 == 0 and S // SEL_BLOCK <= LANE and n_exp <= LANE
    x2 = x.reshape(T, D)

    w = w_in[0]
    a0 = COL_NQ
    n0 = a0 + GLA_RANK
    g0 = n0 + (N_MAIN - COL_NQ)
    w_main = jnp.concatenate([
        w[:, :COL_GK] * (GLA_DK ** -0.5), w[:, COL_GK:a0],
        w[:, n0:n0 + NSA_WIDTH] * (NSA_DH ** -0.5 * LOG2E), w[:, n0 + NSA_WIDTH:g0]], axis=1).astype(BF16)
    w_nar = jnp.concatenate([w[:, a0:n0], w[:, g0:], jnp.zeros((D, LANE - GLA_RANK - 3 * NSA_HEADS), F32)],
                            axis=1).astype(BF16)
    wa2p = jnp.concatenate([gla_wa2[0], jnp.zeros((LANE - GLA_RANK, GLA_HEADS * GLA_DK), F32)], axis=0).astype(BF16)

    tm1 = min(1024, T)
    proj, log_a, nar = _inproj(x2, norm_mix_g, w_main, w_nar, wa2p, gla_ba, tm1, N_MAIN // 4)

    e_w1, e_w2 = exp_w1[0], exp_w2[0]
    o_gla = _gla(proj, log_a, gla_norm_g, B, S, min(256, S))

    ncp = S // CMP_STRIDE
    half = CMP_STRIDE * NSA_DH

    def stride_groups(col):
        t4 = proj[:, col:col + KV_W].reshape(B, ncp, CMP_STRIDE, G, NSA_DH)
        return t4.transpose(0, 3, 1, 2, 4).reshape(B * G, ncp, half)

    hb = jnp.stack([stride_groups(COL_KC), stride_groups(COL_VC)], axis=0)
    pe = jnp.stack([cmp_pe_k[0].reshape(2, half), cmp_pe_v[0].reshape(2, half)], axis=0)
    w1c = jnp.stack([cmp_wk1[0], cmp_wv1[0]], axis=0).astype(BF16)
    w2c = jnp.stack([cmp_wk2[0], cmp_wv2[0]], axis=0).astype(BF16)
    cmp_kv = _compress(hb, pe, w1c, w2c)

    dtiles, w4, bias_c = _bias_tables(rel_bias, S)
    o_cmp, sel = _cmpattn(proj, cmp_kv, bias_c, _overlap_t(S), B, S)
    o_sel, o_win, w1b, w2b = _selwin(proj, sel, dtiles, w4, e_w1.reshape(-1, e_w1.shape[-1]),
                                     e_w2.reshape(-1, e_w2.shape[-1]), B, S)
    w1b, w2b = w1b.reshape(e_w1.shape), w2b.reshape(e_w2.shape)

    rw = jnp.concatenate([router_w[0], jnp.zeros((D, LANE - n_exp), F32)], axis=1)
    rwh = rw.astype(BF16)
    rwl = (rw - rwh.astype(F32)).astype(BF16)
    rb = jnp.concatenate([router_b[0], jnp.full((LANE - n_exp,), NEG, F32)])[None, :]
    h1, hn, top_i, top_w = _outproj(x2, o_gla, o_cmp, o_sel, o_win, nar, nsa_norm_g, w_out[0].astype(BF16),
                                    norm_ffn_g, rwh, rwl, rb, min(512, T))

    tm6 = min(1024, T)
    tok, dst, wgt, tile_exp, tile_valid = _route(top_i[:, :TOP_K], top_w[:, :TOP_K], n_exp, tm6)
    ys = _moe(tile_exp, tile_valid, tok, dst, hn, wgt, w1b, exp_b1[0][:, None, :],
              w2b, exp_b2[0][:, None, :], TOP_K * T + tm6, tm6, 512)
    out = _combine(ys, h1, norm_final_g[None, :], min(512, T))
    return out.reshape(B, S, D)
```

```python
import functools
import math

import numpy as np
import jax
import jax.numpy as jnp
from jax import lax
from jax.experimental import pallas as pl
from jax.experimental.pallas import tpu as pltpu

F32 = jnp.float32
BF16 = jnp.bfloat16
I32 = jnp.int32
U32 = jnp.uint32

GLA_HEADS = 4
GLA_DK = 128
GLA_DV = 256
GLA_RANK = 16
GLA_TAU = 16.0
NSA_HEADS = 8
NSA_KV = 2
NSA_REP = NSA_HEADS // NSA_KV
NSA_DH = 128
CMP_LEN = 32
CMP_STRIDE = 16
SEL_BLOCK = 64
SEL_TOPN = 16
WINDOW = 512
N_BUCKETS = 32
MAX_DISTANCE = 128
TOP_K = 4
SWIGLU_ALPHA = 1.702
SWIGLU_LIMIT = 7.0
RMS_EPS = 1e-5
NEG = -1e30
FORCE_SCORE = 1e4
LOG2E = 1.4426950408889634

GLA_WIDTH = GLA_HEADS * GLA_DV
NSA_WIDTH = NSA_HEADS * NSA_DH
KV_W = NSA_KV * NSA_DH

COL_GQ = 0
COL_GK = COL_GQ + GLA_HEADS * GLA_DK
COL_GV = COL_GK + GLA_HEADS * GLA_DK
COL_GG = COL_GV + GLA_WIDTH
COL_NQ = COL_GG + GLA_WIDTH
COL_KC = COL_NQ + NSA_WIDTH
COL_VC = COL_KC + KV_W
COL_KS = COL_VC + KV_W
COL_VS = COL_KS + KV_W
COL_KW = COL_VS + KV_W
COL_VW = COL_KW + KV_W
N_MAIN = COL_VW + KV_W
LANE = 128
ATT_T = 256
SEL_T = 256
GLA_C = 64
GLA_SUB = 16
MOE_SUB = 512
DMA_UNROLL = 32
GATE_COL0 = GLA_RANK

VMEM_LIMIT = 56 * 1024 * 1024

NT = (((1,), (1,)), ((), ()))
TN = (((0,), (0,)), ((), ()))


def _dot(a, b):
    return jnp.dot(a, b, preferred_element_type=F32)


def _dg(a, b, dims):
    return lax.dot_general(a, b, dims, preferred_element_type=F32)


def _sigmoid(x):
    return 1.0 / (1.0 + jnp.exp(-x))


def _log_sigmoid(z):
    return -(jnp.maximum(-z, 0.0) + jnp.log(1.0 + jnp.exp(-jnp.abs(z))))


def _rms(x, g):
    return x * lax.rsqrt(jnp.mean(x * x, axis=-1, keepdims=True) + RMS_EPS) * g


def _split3(x):
    hi = x.astype(BF16)
    r1 = x - hi.astype(F32)
    mid = r1.astype(BF16)
    lo = (r1 - mid.astype(F32)).astype(BF16)
    return hi, mid, lo


def _pack_pairs(lo, hi):
    lo_bits = lax.bitcast_convert_type(lo.astype(BF16).astype(F32), U32) >> 16
    hi_bits = lax.bitcast_convert_type(hi.astype(BF16).astype(F32), U32) & jnp.uint32(0xFFFF0000)
    return lo_bits | hi_bits


def _unpack_pairs(w):
    lo = lax.bitcast_convert_type(w << 16, F32)
    hi = lax.bitcast_convert_type(w & jnp.uint32(0xFFFF0000), F32)
    return lo, hi


SUBLANE = 8


def _store_rows_tiled(ref, base, n, packed):
    for j in range(SUBLANE):
        ref[pl.ds(base * SUBLANE + j, n, stride=SUBLANE), :] = packed[:, j * LANE:(j + 1) * LANE]


def _load_rows_tiled(ref, base, n, lead=()):
    return [ref[lead + (pl.ds(base * SUBLANE + j, n, stride=SUBLANE), slice(None))] for j in range(SUBLANE)]


def _params(sem, vmem=VMEM_LIMIT):
    return pltpu.CompilerParams(dimension_semantics=sem, vmem_limit_bytes=vmem)


def _inproj_kernel(x_ref, g_ref, w_ref, wn_ref, wa2_ref, ba_ref, proj_ref, la_ref, nar_ref, xn_ref):
    @pl.when(pl.program_id(1) == 0)
    def _():
        xn = _rms(x_ref[...], g_ref[...]).astype(BF16)
        xn_ref[...] = xn
        nar = _dot(xn, wn_ref[...])
        nar_ref[...] = nar
        z = _dot(nar.astype(BF16), wa2_ref[...]) + ba_ref[...]
        la_ref[...] = _log_sigmoid(z) * (1.0 / GLA_TAU)

    proj_ref[...] = _dot(xn_ref[...], w_ref[...]).astype(BF16)


def _inproj(x2, g, w_main, w_nar, wa2p, ba, tm, tn):
    T, D = x2.shape
    nj = N_MAIN // tn
    return pl.pallas_call(
        _inproj_kernel,
        grid=(T // tm, nj),
        in_specs=[
            pl.BlockSpec((tm, D), lambda i, j: (i, 0)),
            pl.BlockSpec((1, D), lambda i, j: (0, 0)),
            pl.BlockSpec((D, tn), lambda i, j: (0, j)),
            pl.BlockSpec((D, LANE), lambda i, j: (0, 0)),
            pl.BlockSpec((LANE, GLA_HEADS * GLA_DK), lambda i, j: (0, 0)),
            pl.BlockSpec((1, GLA_HEADS * GLA_DK), lambda i, j: (0, 0)),
        ],
        out_specs=[
            pl.BlockSpec((tm, tn), lambda i, j: (i, j)),
            pl.BlockSpec((tm, GLA_HEADS * GLA_DK), lambda i, j: (i, 0)),
            pl.BlockSpec((tm, LANE), lambda i, j: (i, 0)),
        ],
        out_shape=[
            jax.ShapeDtypeStruct((T, N_MAIN), BF16),
            jax.ShapeDtypeStruct((T, GLA_HEADS * GLA_DK), F32),
            jax.ShapeDtypeStruct((T, LANE), F32),
        ],
        scratch_shapes=[pltpu.VMEM((tm, D), BF16)],
        compiler_params=_params(("parallel", "arbitrary")),
    )(x2, g, w_main, w_nar, wa2p, ba)


def _gla_kernel(q_ref, k_ref, v_ref, g_ref, la_ref, ng_ref, o_ref, st_ref, *, cb):
    @pl.when(pl.program_id(1) == 0)
    def _():
        st_ref[...] = jnp.zeros_like(st_ref)

    C, SUB = GLA_C, GLA_SUB
    heads = range(GLA_HEADS)
    dk = lambda h: slice(h * GLA_DK, (h + 1) * GLA_DK)
    dv = lambda h: slice(h * GLA_DV, (h + 1) * GLA_DV)
    ri = lax.broadcasted_iota(I32, (C, C), 0)
    ci = lax.broadcasted_iota(I32, (C, C), 1)
    tri = jnp.where(ri >= ci, 1.0, 0.0).astype(BF16)
    nsub = C // SUB
    sub_of_row = lax.broadcasted_iota(I32, (C, GLA_DK), 0) // SUB
    for n in range(cb // C):
        rows = slice(n * C, (n + 1) * C)
        b = []
        for h in heads:
            hi, mid, lo = _split3(la_ref[rows, dk(h)])
            b.append(_dot(tri, hi) + _dot(tri, mid) + _dot(tri, lo))
        q = [q_ref[rows, dk(h)].astype(F32) for h in heads]
        k = [k_ref[rows, dk(h)].astype(F32) for h in heads]
        v = [v_ref[rows, dv(h)] for h in heads]
        st = [st_ref[h] for h in heads]
        o = [_dg((q[h] * jnp.exp(b[h])).astype(BF16), st[h].astype(BF16), NT) for h in heads]
        intra = []
        for h in heads:
            refs = [jnp.zeros((1, GLA_DK), F32)] + [b[h][s * SUB - 1:s * SUB, :] for s in range(1, nsub)]
            ref_row = jnp.concatenate([jnp.broadcast_to(r, (SUB, GLA_DK)) for r in refs], axis=0)
            qsc = q[h] * jnp.exp(b[h] - ref_row)
            q_cat = jnp.concatenate([jnp.where(sub_of_row == s, qsc, 0.0) for s in range(nsub)], axis=1)
            k_cat = jnp.concatenate([jnp.where(sub_of_row <= s, k[h] * jnp.exp(refs[s] - b[h]), 0.0)
                                     for s in range(nsub)], axis=1)
            sc = jnp.where(ci <= ri, _dg(q_cat.astype(BF16), k_cat.astype(BF16), NT), 0.0)
            intra.append(_dot(sc.astype(BF16), v[h]))
        for h in heads:
            bl = b[h][C - 1:C, :]
            kdec = (k[h] * jnp.exp(bl - b[h])).astype(BF16)
            st_ref[h] = st[h] * jnp.exp(bl) + _dg(v[h], kdec, TN)
        for h in heads:
            oh = o[h] + intra[h]
            gg = g_ref[rows, dv(h)].astype(F32)
            o_ref[rows, dv(h)] = (_rms(oh, ng_ref[...]) * (gg * _sigmoid(gg))).astype(BF16)


def _cast_slab(w2d, nsteps):
    rows, width = w2d.shape
    assert rows % nsteps == 0 and (rows // nsteps) % 16 == 0
    return rows // nsteps, width


def _gla(proj, la, ng, B, S, cb):
    T = B * S
    ncb = S // cb
    dkw = GLA_HEADS * GLA_DK
    row = lambda b, c: b * ncb + c
    return pl.pallas_call(
        functools.partial(_gla_kernel, cb=cb),
        grid=(B, ncb),
        in_specs=[
            pl.BlockSpec((cb, dkw), lambda b, c: (row(b, c), COL_GQ // dkw)),
            pl.BlockSpec((cb, dkw), lambda b, c: (row(b, c), COL_GK // dkw)),
            pl.BlockSpec((cb, GLA_WIDTH), lambda b, c: (row(b, c), COL_GV // GLA_WIDTH)),
            pl.BlockSpec((cb, GLA_WIDTH), lambda b, c: (row(b, c), COL_GG // GLA_WIDTH)),
            pl.BlockSpec((cb, dkw), lambda b, c: (row(b, c), 0)),
            pl.BlockSpec((1, GLA_DV), lambda b, c: (0, 0)),
        ],
        out_specs=pl.BlockSpec((cb, GLA_WIDTH), lambda b, c: (row(b, c), 0)),
        out_shape=jax.ShapeDtypeStruct((T, GLA_WIDTH), BF16),
        scratch_shapes=[pltpu.VMEM((GLA_HEADS, GLA_DV, GLA_DK), F32)],
        compiler_params=_params(("parallel", "arbitrary")),
    )(proj, proj, proj, proj, la, ng)


def _compress_kernel(hb_ref, pe_ref, w1_ref, w2_ref, o_ref, *, ncp):
    half = CMP_STRIDE * NSA_DH
    hb = hb_ref[0, 0].astype(F32)
    pe = pe_ref[0]
    top = (hb + pe[0:1]).astype(BF16)
    bot = (hb + pe[1:2]).astype(BF16)
    a = _dot(top, w1_ref[0, :half, :])
    bm = _dot(bot, w1_ref[0, half:, :])
    pre = a + pltpu.roll(bm, ncp - 1, 0)
    cdf = 0.5 * (1.0 + jnp.tanh(math.sqrt(2.0 / math.pi) * (pre + 0.044715 * (pre * pre * pre))))
    o_ref[0, 0] = _dot((pre * cdf).astype(BF16), w2_ref[0]).astype(BF16)


def _compress(hb, pe, w1, w2):
    _, BG, ncp, half = hb.shape
    return pl.pallas_call(
        functools.partial(_compress_kernel, ncp=ncp),
        grid=(2, BG),
        in_specs=[
            pl.BlockSpec((1, 1, ncp, half), lambda t, n: (t, n, 0, 0)),
            pl.BlockSpec((1, 2, half), lambda t, n: (t, 0, 0)),
            pl.BlockSpec((1, 2 * half, NSA_DH), lambda t, n: (t, 0, 0)),
            pl.BlockSpec((1, NSA_DH, NSA_DH), lambda t, n: (t, 0, 0)),
        ],
        out_specs=pl.BlockSpec((1, 1, ncp, NSA_DH), lambda t, n: (t, n, 0, 0)),
        out_shape=jax.ShapeDtypeStruct((2, BG, ncp, NSA_DH), BF16),
        compiler_params=_params(("parallel", "parallel")),
    )(hb, pe, w1, w2)


def _cmpattn_kernel(q_ref, kc_ref, vc_ref, bias_ref, ovt_ref, o_ref, sel_ref, *, ns, ncp):
    tq = ATT_T
    i = pl.program_id(1)
    kc = kc_ref[0, 0]
    vc = vc_ref[0, 0]
    pos = i * tq + lax.broadcasted_iota(I32, (tq, 1), 0)
    row_valid = jnp.where(pos >= CMP_LEN - 1, 1.0, 0.0)
    psum = jnp.zeros((tq, ncp), F32)
    outs = []
    for r in range(NSA_REP):
        q = q_ref[:, r * NSA_DH:(r + 1) * NSA_DH]
        s = _dg(q, kc, NT) + bias_ref[r]
        e = jnp.exp2(s - jnp.max(s, axis=-1, keepdims=True))
        p = e / jnp.sum(e, axis=-1, keepdims=True) * row_valid
        psum = psum + p
        outs.append(_dot(p.astype(BF16), vc))
    o_ref[...] = jnp.concatenate(outs, axis=1).astype(BF16)

    ph = psum.astype(BF16)
    pl_ = (psum - ph.astype(F32)).astype(BF16)
    imp = _dg(ovt_ref[...], ph, NT) + _dg(ovt_ref[...], pl_, NT)
    n_i = lax.broadcasted_iota(I32, (ns, tq), 0)
    cur = (i * tq + lax.broadcasted_iota(I32, (ns, tq), 1)) // SEL_BLOCK
    forced = (n_i == 0) | (n_i == cur) | (n_i == cur - 1)
    score = jnp.where(forced, FORCE_SCORE, jnp.where(n_i <= cur, imp, -1.0))
    cnt = jnp.zeros((ns, tq), F32)
    for m in range(ns):
        sm = score[m:m + 1, :]
        ge = jnp.where(sm >= score, 1.0, 0.0)
        gt = jnp.where(sm > score, 1.0, 0.0)
        cnt = cnt + jnp.where(n_i > m, ge, gt)
    sel_t = jnp.where(cnt < min(SEL_TOPN, ns), 1.0, 0.0)
    if ns < LANE:
        sel_t = jnp.concatenate([sel_t, jnp.zeros((LANE - ns, tq), F32)], axis=0)
    sel_ref[0] = sel_t.T.astype(BF16)


def _cmpattn(proj, cmp_kv, bias_c, ovt, B, S):
    T = B * S
    tq = ATT_T
    nq = S // tq
    ns = S // SEL_BLOCK
    ncp = S // CMP_STRIDE
    G = NSA_KV
    qcol = COL_NQ // (NSA_REP * NSA_DH)
    return pl.pallas_call(
        functools.partial(_cmpattn_kernel, ns=ns, ncp=ncp),
        grid=(G, nq, B),
        in_specs=[
            pl.BlockSpec((tq, NSA_REP * NSA_DH), lambda g, i, b: (b * nq + i, qcol + g)),
            pl.BlockSpec((1, 1, ncp, NSA_DH), lambda g, i, b: (0, b * G + g, 0, 0)),
            pl.BlockSpec((1, 1, ncp, NSA_DH), lambda g, i, b: (1, b * G + g, 0, 0)),
            pl.BlockSpec((NSA_REP, tq, ncp), lambda g, i, b: (g, i, 0)),
            pl.BlockSpec((ns, ncp), lambda g, i, b: (0, 0)),
        ],
        out_specs=[
            pl.BlockSpec((tq, NSA_REP * NSA_DH), lambda g, i, b: (b * nq + i, g)),
            pl.BlockSpec((1, tq, LANE), lambda g, i, b: (b * G + g, i, 0)),
        ],
        out_shape=[
            jax.ShapeDtypeStruct((T, NSA_WIDTH), BF16),
            jax.ShapeDtypeStruct((B * G, S, LANE), BF16),
        ],
        compiler_params=_params(("parallel", "parallel", "parallel")),
    )(proj, cmp_kv, cmp_kv, bias_c, ovt)


def _flash_update(s, v, m_ref, al_ref):
    reps = s.shape[1] // LANE
    m_old = m_ref[...]
    part = s[:, :LANE]
    for c in range(1, reps):
        part = jnp.maximum(part, s[:, c * LANE:(c + 1) * LANE])
    m_new = jnp.maximum(m_old, jnp.max(part, axis=-1, keepdims=True))
    alpha = jnp.exp2(m_old - m_new)
    p = jnp.exp2(s - jnp.concatenate([m_new] * reps, axis=1))
    v_ext = jnp.concatenate([v, jnp.ones(v.shape, BF16)], axis=1)
    al_ref[...] = jnp.concatenate([alpha, alpha], axis=1) * al_ref[...] + _dot(p.astype(BF16), v_ext)
    m_ref[...] = m_new


def _selwin_kernel(q_ref, ks_ref, vs_ref, kw_ref, vw_ref, sel_ref, d_ref, w4_ref, wf1_ref, wf2_ref,
                   osel_ref, owin_ref, wb1_ref, wb2_ref, m_ref, al_ref, s_ref):
    wb1_ref[...] = wf1_ref[...].astype(BF16)
    wb2_ref[...] = wf2_ref[...].astype(BF16)
    t = SEL_T
    R = NSA_REP
    i = pl.program_id(2)
    q = jnp.concatenate([q_ref[:, r * NSA_DH:(r + 1) * NSA_DH] for r in range(R)], axis=0)
    unsel = (1.0 - sel_ref[0].astype(F32)).astype(BF16)
    q_sel = jnp.concatenate([q, jnp.concatenate([unsel] * R, axis=0)], axis=1)
    c_blk = lax.broadcasted_iota(I32, (t, LANE), 0) // SEL_BLOCK
    n_i = lax.broadcasted_iota(I32, (t, LANE), 1)
    blk_per_tile = t // SEL_BLOCK

    def init():
        m_ref[...] = jnp.full(m_ref.shape, NEG, F32)
        al_ref[...] = jnp.zeros(al_ref.shape, F32)

    def tile(ref, j):
        return ref[pl.ds(pl.multiple_of(j * t, t), t), :]

    def logits(k_ref, j, add):
        s = _dg(q, tile(k_ref, j), NT).reshape(R, t, t)
        return (s + add).reshape(R * t, t)

    def sel_logits(j):
        own_block = jnp.where(n_i == blk_per_tile * j + c_blk, NEG, 0.0).astype(BF16)
        return _dg(q_sel, jnp.concatenate([tile(ks_ref, j), own_block], axis=1), NT)

    def write(o_ref, al):
        o = al[:, :NSA_DH] / al[:, NSA_DH:]
        o_ref[...] = jnp.concatenate([o[r * t:(r + 1) * t] for r in range(R)], axis=1).astype(BF16)

    init()

    jn = jnp.maximum(i - 1, 0)
    s_ref[...] = sel_logits(0)

    def far(j, c):
        s = s_ref[...]
        s_ref[...] = sel_logits(j + 1)
        _flash_update(s, tile(vs_ref, j), m_ref, al_ref)
        return c

    def with_bias(s, add):
        return (s.reshape(R, t, t) + add).reshape(R * t, t)

    lax.fori_loop(0, jn, far, 0)
    off = jnp.where(i == 0, NEG, 0.0)
    s_near = s_ref[...]
    s_diag = sel_logits(i)
    _flash_update(with_bias(s_near, d_ref[:, 1] + off), tile(vs_ref, jn), m_ref, al_ref)
    _flash_update(with_bias(s_diag, d_ref[:, 0]), tile(vs_ref, i), m_ref, al_ref)
    write(osel_ref, al_ref[...])

    init()
    pre = WINDOW // t
    for o in range(pre, -1, -1):
        jw = i - o
        jc = jnp.maximum(jw, 0)
        off = jnp.where(jw < 0, NEG, 0.0)
        if o == 0:
            add = d_ref[:, 0] + off
        elif o == 1:
            add = d_ref[:, 1] + off
        elif o == pre:
            add = (w4_ref[...] + off)[None]
        else:
            add = off
        _flash_update(logits(kw_ref, jc, add), tile(vw_ref, jc), m_ref, al_ref)
    write(owin_ref, al_ref[...])


def _selwin(proj, sel, dtiles, w4, wf1, wf2, B, S):
    T = B * S
    t = SEL_T
    nq = S // t
    G = NSA_KV
    qcol = COL_NQ // (NSA_REP * NSA_DH)
    kv = lambda col: pl.BlockSpec((S, NSA_DH), lambda b, g, i, col=col: (b, col // NSA_DH + g))
    slab = lambda wf: pl.BlockSpec(_cast_slab(wf, B * G * nq), lambda b, g, i: ((b * G + g) * nq + i, 0))
    return pl.pallas_call(
        _selwin_kernel,
        grid=(B, G, nq),
        in_specs=[
            pl.BlockSpec((t, NSA_REP * NSA_DH), lambda b, g, i: (b * nq + i, qcol + g)),
            kv(COL_KS), kv(COL_VS), kv(COL_KW), kv(COL_VW),
            pl.BlockSpec((1, t, LANE), lambda b, g, i: (b * G + g, i, 0)),
            pl.BlockSpec((NSA_REP, 2, t, t), lambda b, g, i: (g, 0, 0, 0)),
            pl.BlockSpec((t, t), lambda b, g, i: (0, 0)),
            slab(wf1), slab(wf2),
        ],
        out_specs=[
            pl.BlockSpec((t, NSA_REP * NSA_DH), lambda b, g, i: (b * nq + i, g)),
            pl.BlockSpec((t, NSA_REP * NSA_DH), lambda b, g, i: (b * nq + i, g)),
            slab(wf1), slab(wf2),
        ],
        out_shape=[
            jax.ShapeDtypeStruct((T, NSA_WIDTH), BF16),
            jax.ShapeDtypeStruct((T, NSA_WIDTH), BF16),
            jax.ShapeDtypeStruct(wf1.shape, BF16),
            jax.ShapeDtypeStruct(wf2.shape, BF16),
        ],
        scratch_shapes=[
            pltpu.VMEM((NSA_REP * t, LANE), F32),
            pltpu.VMEM((NSA_REP * t, 2 * NSA_DH), F32),
            pltpu.VMEM((NSA_REP * t, t), F32),
        ],
        compiler_params=_params(("parallel", "parallel", "parallel")),
    )(proj, proj, proj, proj, proj, sel, dtiles, w4, wf1, wf2)


def _outproj_kernel(x_ref, og_ref, oc_ref, os_ref, ow_ref, nar_ref, ng_ref, wo_ref, fg_ref,
                    rwh_ref, rwl_ref, rb_ref, h1_ref, hn_ref, ti_ref, tw_ref):
    gates = _sigmoid(nar_ref[...])
    parts = []
    for h in range(NSA_HEADS):
        cs = slice(h * NSA_DH, (h + 1) * NSA_DH)
        g0 = gates[:, GATE_COL0 + h:GATE_COL0 + h + 1]
        g1 = gates[:, GATE_COL0 + NSA_HEADS + h:GATE_COL0 + NSA_HEADS + h + 1]
        g2 = gates[:, GATE_COL0 + 2 * NSA_HEADS + h:GATE_COL0 + 2 * NSA_HEADS + h + 1]
        parts.append(g0 * oc_ref[:, cs].astype(F32) + g1 * os_ref[:, cs].astype(F32)
                     + g2 * ow_ref[:, cs].astype(F32))
    on = _rms(jnp.concatenate(parts, axis=1), ng_ref[...]).astype(BF16)
    h1 = x_ref[...] + _dot(og_ref[...], wo_ref[:GLA_WIDTH, :]) + _dot(on, wo_ref[GLA_WIDTH:, :])
    h1_ref[...] = h1
    hn = _rms(h1, fg_ref[...])
    half = hn.shape[1] // 2
    _store_rows_tiled(hn_ref, 0, hn.shape[0], _pack_pairs(hn[:, :half], hn[:, half:]))

    hh = hn.astype(BF16)
    hl = (hn - hh.astype(F32)).astype(BF16)
    both = _dot(hh, jnp.concatenate([rwh_ref[...], rwl_ref[...]], axis=1))
    lg = both[:, :LANE] + both[:, LANE:] + _dot(hl, rwh_ref[...]) + rb_ref[...]
    lane = lax.broadcasted_iota(I32, lg.shape, 1)
    vals, idxs = [], []
    for _ in range(TOP_K):
        mx = jnp.max(lg, axis=-1, keepdims=True)
        ix = jnp.min(jnp.where(lg == mx, lane, LANE), axis=-1, keepdims=True)
        vals.append(mx)
        idxs.append(ix)
        lg = jnp.where(lane == ix, -3e38, lg)
    es = [jnp.exp(v - vals[0]) for v in vals]
    den = es[0] + es[1] + es[2] + es[3]
    ti = jnp.zeros(lg.shape, I32)
    tw = jnp.zeros(lg.shape, F32)
    for kk in range(TOP_K):
        ti = jnp.where(lane == kk, idxs[kk], ti)
        tw = jnp.where(lane == kk, es[kk] / den, tw)
    ti_ref[...] = ti
    tw_ref[...] = tw


def _outproj(x2, og, oc, osel, ow, nar, ng, wo, fg, rwh, rwl, rb, tm):
    T, D = x2.shape
    row = lambda w: pl.BlockSpec((tm, w), lambda i: (i, 0))
    full = lambda a: pl.BlockSpec(a.shape, lambda i: (0,) * a.ndim)
    return pl.pallas_call(
        _outproj_kernel,
        grid=(T // tm,),
        in_specs=[row(D), row(GLA_WIDTH), row(NSA_WIDTH), row(NSA_WIDTH), row(NSA_WIDTH), row(LANE),
                  full(ng), full(wo), full(fg), full(rwh), full(rwl), full(rb)],
        out_specs=[row(D), pl.BlockSpec((tm * SUBLANE, LANE), lambda i: (i, 0)), row(LANE), row(LANE)],
        out_shape=[
            jax.ShapeDtypeStruct((T, D), F32),
            jax.ShapeDtypeStruct((T * SUBLANE, LANE), U32),
            jax.ShapeDtypeStruct((T, LANE), I32),
            jax.ShapeDtypeStruct((T, LANE), F32),
        ],
        compiler_params=_params(("parallel",)),
    )(x2, og, oc, osel, ow, nar, ng, wo, fg, rwh, rwl, rb)


def _moe_kernel(te_ref, tv_ref, tok_hbm, dst_hbm, hn_hbm, wgt_ref, w1g_ref, w1l_ref, b1g_ref, b1l_ref,
                w2_ref, b2_ref, out_hbm, tok_smem, dst_smem, xbuf, xb_ref, act_ref, ybuf,
                sem_tok, sem_dst, sem_in, sem_out, *, tm, nf, n_tiles):
    i = pl.program_id(0)
    f = pl.program_id(1)
    slot = i % 2
    valid = tv_ref[i]
    sub = MOE_SUB
    nsub = tm // sub
    half = SUBLANE * LANE
    sl8 = SUBLANE

    def tile_of(r):
        return pl.ds(pl.multiple_of(r * sl8, sl8), sl8)

    def tok_copy(t, sl):
        return pltpu.make_async_copy(tok_hbm.at[t], tok_smem.at[sl], sem_tok.at[sl])

    def dst_copy():
        return pltpu.make_async_copy(dst_hbm.at[i], dst_smem, sem_dst)

    def issue_gather(t, sl):
        for s in range(nsub):
            @pl.when(s * sub < tv_ref[t])
            def _():
                def body(r, c):
                    pltpu.make_async_copy(hn_hbm.at[tile_of(tok_smem[sl, r]), :],
                                          xbuf.at[sl, tile_of(r), :], sem_in.at[sl, s]).start()
                    return c

                lax.fori_loop(s * sub, (s + 1) * sub, body, 0, unroll=DMA_UNROLL)

    def wait_gather(sl, s):
        pltpu.make_async_copy(hn_hbm.at[pl.ds(0, sub * sl8), :], xbuf.at[sl, pl.ds(s * sub * sl8, sub * sl8), :],
                              sem_in.at[sl, s]).wait()

    def wait_scatter(count):
        for s in range(nsub):
            @pl.when(s * sub < count)
            def _():
                pltpu.make_async_copy(ybuf.at[pl.ds(s * sub * sl8, sub * sl8), :],
                                      out_hbm.at[pl.ds(0, sub * sl8), :], sem_out).wait()

    @pl.when(f == 0)
    def _():
        @pl.when(i == 0)
        def _():
            cp = tok_copy(0, 0)
            cp.start()
            cp.wait()
            issue_gather(0, 0)

        @pl.when(i + 1 < n_tiles)
        def _():
            tok_copy(i + 1, 1 - slot).start()

        dst_copy().start()
        for s in range(nsub):
            @pl.when(s * sub < valid)
            def _():
                wait_gather(slot, s)
                for j, w in enumerate(_load_rows_tiled(xbuf, s * sub, sub, lead=(slot,))):
                    lo, hi = _unpack_pairs(w)
                    xb_ref[s * sub:(s + 1) * sub, j * LANE:(j + 1) * LANE] = lo.astype(BF16)
                    xb_ref[s * sub:(s + 1) * sub, half + j * LANE:half + (j + 1) * LANE] = hi.astype(BF16)

        @pl.when(i + 1 < n_tiles)
        def _():
            tok_copy(i + 1, 1 - slot).wait()
            issue_gather(i + 1, 1 - slot)

    for s in range(nsub):
        @pl.when(s * sub < valid)
        def _():
            rows = slice(s * sub, (s + 1) * sub)
            xb = xb_ref[rows, :]
            glu = jnp.minimum(_dot(xb, w1g_ref[0]) + b1g_ref[0], SWIGLU_LIMIT)
            lin = jnp.clip(_dot(xb, w1l_ref[0]) + b1l_ref[0], -SWIGLU_LIMIT, SWIGLU_LIMIT)
            act_ref[f, rows, :] = (glu * _sigmoid(SWIGLU_ALPHA * glu) * (lin + 1.0)).astype(BF16)

    @pl.when(f == nf - 1)
    def _():
        @pl.when(i > 0)
        def _():
            wait_scatter(tv_ref[jnp.maximum(i - 1, 0)])

        dst_copy().wait()
        for s in range(nsub):
            @pl.when(s * sub < valid)
            def _():
                rows = slice(s * sub, (s + 1) * sub)
                fc = act_ref.shape[2]
                y = _dot(act_ref[0, rows, :], w2_ref[0, :fc, :])
                for c in range(1, nf):
                    y = y + _dot(act_ref[c, rows, :], w2_ref[0, c * fc:(c + 1) * fc, :])
                y = (y + b2_ref[0]) * wgt_ref[0, rows, :]
                _store_rows_tiled(ybuf, s * sub, sub, _pack_pairs(y[:, :half], y[:, half:]))

                def body(r, c):
                    pltpu.make_async_copy(ybuf.at[tile_of(r), :], out_hbm.at[tile_of(dst_smem[r]), :],
                                          sem_out).start()
                    return c

                lax.fori_loop(s * sub, (s + 1) * sub, body, 0, unroll=DMA_UNROLL)

        @pl.when(i == n_tiles - 1)
        def _():
            wait_scatter(valid)
            ybuf[...] = jnp.zeros_like(ybuf)
            dump = pltpu.make_async_copy(ybuf, out_hbm.at[pl.ds(out_hbm.shape[0] - tm * sl8, tm * sl8), :], sem_out)
            dump.start()
            dump.wait()


def _moe(tile_exp, tile_valid, tok, dst, hn, wgt, w1, b1, w2, b2, n_out_rows, tm, fc):
    n_tiles = tok.shape[0]
    D = w1.shape[1]
    assert D == 2 * SUBLANE * LANE
    dff = w2.shape[1]
    nf = dff // fc
    grid_spec = pltpu.PrefetchScalarGridSpec(
        num_scalar_prefetch=2,
        grid=(n_tiles, nf),
        in_specs=[
            pl.BlockSpec(memory_space=pl.ANY),
            pl.BlockSpec(memory_space=pl.ANY),
            pl.BlockSpec(memory_space=pl.ANY),
            pl.BlockSpec((1, tm, 1), lambda i, f, te, tv: (i, 0, 0)),
            pl.BlockSpec((1, D, fc), lambda i, f, te, tv: (te[i], 0, f)),
            pl.BlockSpec((1, D, fc), lambda i, f, te, tv: (te[i], 0, nf + f)),
            pl.BlockSpec((1, 1, fc), lambda i, f, te, tv: (te[i], 0, f)),
            pl.BlockSpec((1, 1, fc), lambda i, f, te, tv: (te[i], 0, nf + f)),
            pl.BlockSpec((1, dff, D), lambda i, f, te, tv: (te[i], 0, 0)),
            pl.BlockSpec((1, 1, D), lambda i, f, te, tv: (te[i], 0, 0)),
        ],
        out_specs=pl.BlockSpec(memory_space=pl.ANY),
        scratch_shapes=[
            pltpu.SMEM((2, tm), I32),
            pltpu.SMEM((tm,), I32),
            pltpu.VMEM((2, tm * SUBLANE, LANE), U32),
            pltpu.VMEM((tm, D), BF16),
            pltpu.VMEM((nf, tm, fc), BF16),
            pltpu.VMEM((tm * SUBLANE, LANE), U32),
            pltpu.SemaphoreType.DMA((2,)),
            pltpu.SemaphoreType.DMA(()),
            pltpu.SemaphoreType.DMA((2, tm // MOE_SUB)),
            pltpu.SemaphoreType.DMA(()),
        ],
    )
    return pl.pallas_call(
        functools.partial(_moe_kernel, tm=tm, nf=nf, n_tiles=n_tiles),
        grid_spec=grid_spec,
        out_shape=jax.ShapeDtypeStruct((n_out_rows * SUBLANE, LANE), U32),
        compiler_params=_params(("arbitrary", "arbitrary")),
    )(tile_exp, tile_valid, tok, dst, hn, wgt, w1, w1, b1, b1, w2, b2)


def _combine_kernel(h1_ref, y0_ref, y1_ref, y2_ref, y3_ref, g_ref, o_ref):
    h = h1_ref[...]
    tm = h.shape[0]
    for y_ref in (y0_ref, y1_ref, y2_ref, y3_ref):
        pairs = [_unpack_pairs(w) for w in _load_rows_tiled(y_ref, 0, tm)]
        h = h + jnp.concatenate([p[0] for p in pairs] + [p[1] for p in pairs], axis=1)
    o_ref[...] = _rms(h, g_ref[...])


def _combine(ys, h1, g, tm):
    T, D = h1.shape
    nb = T // tm
    slab = lambda k: pl.BlockSpec((tm * SUBLANE, LANE), lambda i, k=k: (k * nb + i, 0))
    return pl.pallas_call(
        _combine_kernel,
        grid=(nb,),
        in_specs=[pl.BlockSpec((tm, D), lambda i: (i, 0))] + [slab(k) for k in range(TOP_K)]
        + [pl.BlockSpec((1, D), lambda i: (0, 0))],
        out_specs=pl.BlockSpec((tm, D), lambda i: (i, 0)),
        out_shape=jax.ShapeDtypeStruct((T, D), F32),
        compiler_params=_params(("parallel",)),
    )(h1, ys, ys, ys, ys, g)


def _t5_bucket(dist):
    n = jnp.maximum(dist, 0)
    exact = N_BUCKETS // 2
    scaled = jnp.log(jnp.maximum(n, 1).astype(F32) / exact) / math.log(MAX_DISTANCE / exact)
    large = exact + (scaled * (N_BUCKETS - exact)).astype(I32)
    return jnp.where(n < exact, n, jnp.minimum(large, N_BUCKETS - 1)).astype(I32)


def _bias_tables(rel_bias, S):
    shifted = ((rel_bias - rel_bias[N_BUCKETS - 1][None, :]) * LOG2E).astype(F32)

    def lookup(dist):
        bucket = _t5_bucket(dist)
        out = jnp.zeros((NSA_HEADS,) + dist.shape, F32)
        for kb in range(N_BUCKETS - 1):
            out = jnp.where((bucket == kb)[None], shifted[kb].reshape((NSA_HEADS,) + (1,) * dist.ndim), out)
        return jnp.where((dist >= 0)[None], out, NEG)

    t = SEL_T
    r = jnp.arange(t)[:, None]
    c = jnp.arange(t)[None, :]
    dtiles = jnp.stack([lookup(r - c), lookup(t + r - c)], axis=1)
    w4 = jnp.where(r < c, 0.0, NEG).astype(F32)
    ncp = S // CMP_STRIDE
    tq = ATT_T
    nq = S // tq
    shift = tq // CMP_STRIDE
    lead = shift * (nq - 1)
    cols = jnp.arange(ncp + lead)[None, :] - lead
    pattern = lookup(jnp.arange(tq)[:, None] - CMP_STRIDE * cols - (CMP_LEN - 1))
    bias_c = jnp.concatenate([pattern[:, :, lead - shift * i:lead - shift * i + ncp] for i in range(nq)], axis=1)
    return dtiles, w4, bias_c


def _overlap_t(S):
    ncp = S // CMP_STRIDE
    ns = S // SEL_BLOCK
    cs = jnp.arange(ncp)[None, :] * CMP_STRIDE
    ss = jnp.arange(ns)[:, None] * SEL_BLOCK
    ov = (cs < ss + SEL_BLOCK) & (cs + CMP_LEN > ss) & (jnp.arange(ncp)[None, :] < ncp - 1)
    return ov.astype(BF16)


def _route(top_idx, top_w, n_exp, tm):
    T = top_idx.shape[0]
    A = T * TOP_K
    e_flat = top_idx.reshape(A)
    w_flat = top_w.reshape(A)
    _, order, w_sorted = lax.sort((e_flat, jnp.arange(A, dtype=I32), w_flat), num_keys=1, is_stable=True)
    counts = jnp.sum((e_flat[:, None] == jnp.arange(n_exp, dtype=I32)[None, :]).astype(I32), axis=0)
    padded = (counts + tm - 1) // tm * tm
    start = jnp.cumsum(counts) - counts
    pend = jnp.cumsum(padded)
    pstart = pend - padded
    n_tiles = A // tm + n_exp
    tile_row0 = jnp.arange(n_tiles, dtype=I32) * tm
    tile_exp = jnp.minimum(jnp.sum((pend[None, :] <= tile_row0[:, None]).astype(I32), axis=1), n_exp - 1)
    tile_valid = jnp.clip(pstart[tile_exp] + counts[tile_exp] - tile_row0, 0, tm).astype(I32)
    run0 = jnp.clip(start[tile_exp] + tile_row0 - pstart[tile_exp], 0, A)
    take = lambda a: jax.vmap(lambda s0: lax.dynamic_slice(a, (s0,), (tm,)))(run0)
    live = jnp.arange(tm, dtype=I32)[None, :] < tile_valid[:, None]
    src = take(jnp.concatenate([order, jnp.zeros((tm,), I32)]))
    buf_tok = jnp.where(live, src // TOP_K, 0)
    buf_w = jnp.where(live, take(jnp.concatenate([w_sorted, jnp.zeros((tm,), F32)])), 0.0)
    buf_dst = jnp.where(live, (src % TOP_K) * T + src // TOP_K, TOP_K * T + jnp.arange(tm, dtype=I32)[None, :])
    return buf_tok, buf_dst, buf_w[:, :, None], tile_exp, tile_valid


def kernel(x, norm_mix_g, w_in, gla_wa2, gla_ba, gla_norm_g, cmp_pe_k, cmp_wk1, cmp_wk2, cmp_pe_v,
           cmp_wv1, cmp_wv2, nsa_norm_g, rel_bias, w_out, norm_ffn_g, router_w, router_b, exp_w1,
           exp_b1, exp_w2, exp_b2, norm_final_g):
    B, S, D = x.shape
    T = B * S
    G = NSA_KV
    n_exp = router_w.shape[-1]
    assert w_in.shape[0] == 1 and S % 1024 == 0 and S // SEL_BLOCK <= LANE and n_exp <= LANE
    x2 = x.reshape(T, D)

    w = w_in[0]
    a0 = COL_NQ
    n0 = a0 + GLA_RANK
    g0 = n0 + (N_MAIN - COL_NQ)
    w_main = jnp.concatenate([
        w[:, :COL_GK] * (GLA_DK ** -0.5), w[:, COL_GK:a0],
        w[:, n0:n0 + NSA_WIDTH] * (NSA_DH ** -0.5 * LOG2E), w[:, n0 + NSA_WIDTH:g0]], axis=1).astype(BF16)
    w_nar = jnp.concatenate([w[:, a0:n0], w[:, g0:], jnp.zeros((D, LANE - GLA_RANK - 3 * NSA_HEADS), F32)],
                            axis=1).astype(BF16)
    wa2p = jnp.concatenate([gla_wa2[0], jnp.zeros((LANE - GLA_RANK, GLA_HEADS * GLA_DK), F32)], axis=0).astype(BF16)

    tm1 = min(1024, T)
    proj, log_a, nar = _inproj(x2, norm_mix_g, w_main, w_nar, wa2p, gla_ba, tm1, N_MAIN // 4)

    e_w1, e_w2 = exp_w1[0], exp_w2[0]
    o_gla = _gla(proj, log_a, gla_norm_g, B, S, min(256, S))

    ncp = S // CMP_STRIDE
    half = CMP_STRIDE * NSA_DH

    def stride_groups(col):
        t4 = proj[:, col:col + KV_W].reshape(B, ncp, CMP_STRIDE, G, NSA_DH)
        return t4.transpose(0, 3, 1, 2, 4).reshape(B * G, ncp, half)

    hb = jnp.stack([stride_groups(COL_KC), stride_groups(COL_VC)], axis=0)
    pe = jnp.stack([cmp_pe_k[0].reshape(2, half), cmp_pe_v[0].reshape(2, half)], axis=0)
    w1c = jnp.stack([cmp_wk1[0], cmp_wv1[0]], axis=0).astype(BF16)
    w2c = jnp.stack([cmp_wk2[0], cmp_wv2[0]], axis=0).astype(BF16)
    cmp_kv = _compress(hb, pe, w1c, w2c)

    dtiles, w4, bias_c = _bias_tables(rel_bias, S)
    o_cmp, sel = _cmpattn(proj, cmp_kv, bias_c, _overlap_t(S), B, S)
    o_sel, o_win, w1b, w2b = _selwin(proj, sel, dtiles, w4, e_w1.reshape(-1, e_w1.shape[-1]),
                                     e_w2.reshape(-1, e_w2.shape[-1]), B, S)
    w1b, w2b = w1b.reshape(e_w1.shape), w2b.reshape(e_w2.shape)

    rw = jnp.concatenate([router_w[0], jnp.zeros((D, LANE - n_exp), F32)], axis=1)
    rwh = rw.astype(BF16)
    rwl = (rw - rwh.astype(F32)).astype(BF16)
    rb = jnp.concatenate([router_b[0], jnp.full((LANE - n_exp,), NEG, F32)])[None, :]
    h1, hn, top_i, top_w = _outproj(x2, o_gla, o_cmp, o_sel, o_win, nar, nsa_norm_g, w_out[0].astype(BF16),
                                    norm_ffn_g, rwh, rwl, rb, min(512, T))

    tm6 = min(1024, T)
    tok, dst, wgt, tile_exp, tile_valid = _route(top_i[:, :TOP_K], top_w[:, :TOP_K], n_exp, tm6)
    ys = _moe(tile_exp, tile_valid, tok, dst, hn, wgt, w1b, exp_b1[0][:, None, :],
              w2b, exp_b2[0][:, None, :], TOP_K * T + tm6, tm6, 512)
    out = _combine(ys, h1, norm_final_g[None, :], min(512, T))
    return out.reshape(B, S, D)
```

```python
import functools
import math

import numpy as np
import jax
import jax.numpy as jnp
from jax import lax
from jax.experimental import pallas as pl
from jax.experimental.pallas import tpu as pltpu

F32 = jnp.float32
BF16 = jnp.bfloat16
I32 = jnp.int32
U32 = jnp.uint32

GLA_HEADS = 4
GLA_DK = 128
GLA_DV = 256
GLA_RANK = 16
GLA_TAU = 16.0
NSA_HEADS = 8
NSA_KV = 2
NSA_REP = NSA_HEADS // NSA_KV
NSA_DH = 128
CMP_LEN = 32
CMP_STRIDE = 16
SEL_BLOCK = 64
SEL_TOPN = 16
WINDOW = 512
N_BUCKETS = 32
MAX_DISTANCE = 128
TOP_K = 4
SWIGLU_ALPHA = 1.702
SWIGLU_LIMIT = 7.0
RMS_EPS = 1e-5
NEG = -1e30
FORCE_SCORE = 1e4
LOG2E = 1.4426950408889634

GLA_WIDTH = GLA_HEADS * GLA_DV
NSA_WIDTH = NSA_HEADS * NSA_DH
KV_W = NSA_KV * NSA_DH

COL_GQ = 0
COL_GK = COL_GQ + GLA_HEADS * GLA_DK
COL_GV = COL_GK + GLA_HEADS * GLA_DK
COL_GG = COL_GV + GLA_WIDTH
COL_NQ = COL_GG + GLA_WIDTH
COL_KC = COL_NQ + NSA_WIDTH
COL_VC = COL_KC + KV_W
COL_KS = COL_VC + KV_W
COL_VS = COL_KS + KV_W
COL_KW = COL_VS + KV_W
COL_VW = COL_KW + KV_W
N_MAIN = COL_VW + KV_W
LANE = 128
ATT_T = 256
SEL_T = 256
GLA_C = 64
GLA_SUB = 16
MOE_SUB = 512
DMA_UNROLL = 32
GATE_COL0 = GLA_RANK

VMEM_LIMIT = 56 * 1024 * 1024

NT = (((1,), (1,)), ((), ()))
TN = (((0,), (0,)), ((), ()))


def _dot(a, b):
    return jnp.dot(a, b, preferred_element_type=F32)


def _dg(a, b, dims):
    return lax.dot_general(a, b, dims, preferred_element_type=F32)


def _sigmoid(x):
    return 1.0 / (1.0 + jnp.exp(-x))


def _log_sigmoid(z):
    return -(jnp.maximum(-z, 0.0) + jnp.log(1.0 + jnp.exp(-jnp.abs(z))))


def _rms(x, g):
    return x * lax.rsqrt(jnp.mean(x * x, axis=-1, keepdims=True) + RMS_EPS) * g


def _split3(x):
    hi = x.astype(BF16)
    r1 = x - hi.astype(F32)
    mid = r1.astype(BF16)
    lo = (r1 - mid.astype(F32)).astype(BF16)
    return hi, mid, lo


def _pack_pairs(lo, hi):
    lo_bits = lax.bitcast_convert_type(lo.astype(BF16).astype(F32), U32) >> 16
    hi_bits = lax.bitcast_convert_type(hi.astype(BF16).astype(F32), U32) & jnp.uint32(0xFFFF0000)
    return lo_bits | hi_bits


def _unpack_pairs(w):
    lo = lax.bitcast_convert_type(w << 16, F32)
    hi = lax.bitcast_convert_type(w & jnp.uint32(0xFFFF0000), F32)
    return lo, hi


SUBLANE = 8


def _store_rows_tiled(ref, base, n, packed):
    for j in range(SUBLANE):
        ref[pl.ds(base * SUBLANE + j, n, stride=SUBLANE), :] = packed[:, j * LANE:(j + 1) * LANE]


def _load_rows_tiled(ref, base, n, lead=()):
    return [ref[lead + (pl.ds(base * SUBLANE + j, n, stride=SUBLANE), slice(None))] for j in range(SUBLANE)]


def _params(sem, vmem=VMEM_LIMIT):
    return pltpu.CompilerParams(dimension_semantics=sem, vmem_limit_bytes=vmem)


def _inproj_kernel(x_ref, g_ref, w_ref, wn_ref, wa2_ref, ba_ref, proj_ref, la_ref, nar_ref, xn_ref):
    @pl.when(pl.program_id(1) == 0)
    def _():
        xn = _rms(x_ref[...], g_ref[...]).astype(BF16)
        xn_ref[...] = xn
        nar = _dot(xn, wn_ref[...])
        nar_ref[...] = nar
        z = _dot(nar.astype(BF16), wa2_ref[...]) + ba_ref[...]
        la_ref[...] = _log_sigmoid(z) * (1.0 / GLA_TAU)

    proj_ref[...] = _dot(xn_ref[...], w_ref[...]).astype(BF16)


def _inproj(x2, g, w_main, w_nar, wa2p, ba, tm, tn):
    T, D = x2.shape
    nj = N_MAIN // tn
    return pl.pallas_call(
        _inproj_kernel,
        grid=(T // tm, nj),
        in_specs=[
            pl.BlockSpec((tm, D), lambda i, j: (i, 0)),
            pl.BlockSpec((1, D), lambda i, j: (0, 0)),
            pl.BlockSpec((D, tn), lambda i, j: (0, j)),
            pl.BlockSpec((D, LANE), lambda i, j: (0, 0)),
            pl.BlockSpec((LANE, GLA_HEADS * GLA_DK), lambda i, j: (0, 0)),
            pl.BlockSpec((1, GLA_HEADS * GLA_DK), lambda i, j: (0, 0)),
        ],
        out_specs=[
            pl.BlockSpec((tm, tn), lambda i, j: (i, j)),
            pl.BlockSpec((tm, GLA_HEADS * GLA_DK), lambda i, j: (i, 0)),
            pl.BlockSpec((tm, LANE), lambda i, j: (i, 0)),
        ],
        out_shape=[
            jax.ShapeDtypeStruct((T, N_MAIN), BF16),
            jax.ShapeDtypeStruct((T, GLA_HEADS * GLA_DK), F32),
            jax.ShapeDtypeStruct((T, LANE), F32),
        ],
        scratch_shapes=[pltpu.VMEM((tm, D), BF16)],
        compiler_params=_params(("parallel", "arbitrary")),
    )(x2, g, w_main, w_nar, wa2p, ba)


def _gla_kernel(q_ref, k_ref, v_ref, g_ref, la_ref, ng_ref, o_ref, st_ref, *, cb):
    @pl.when(pl.program_id(1) == 0)
    def _():
        st_ref[...] = jnp.zeros_like(st_ref)

    C, SUB = GLA_C, GLA_SUB
    heads = range(GLA_HEADS)
    dk = lambda h: slice(h * GLA_DK, (h + 1) * GLA_DK)
    dv = lambda h: slice(h * GLA_DV, (h + 1) * GLA_DV)
    ri = lax.broadcasted_iota(I32, (C, C), 0)
    ci = lax.broadcasted_iota(I32, (C, C), 1)
    tri = jnp.where(ri >= ci, 1.0, 0.0).astype(BF16)
    nsub = C // SUB
    sub_of_row = lax.broadcasted_iota(I32, (C, GLA_DK), 0) // SUB
    for n in range(cb // C):
        rows = slice(n * C, (n + 1) * C)
        b = []
        for h in heads:
            hi, mid, lo = _split3(la_ref[rows, dk(h)])
            b.append(_dot(tri, hi) + _dot(tri, mid) + _dot(tri, lo))
        q = [q_ref[rows, dk(h)].astype(F32) for h in heads]
        k = [k_ref[rows, dk(h)].astype(F32) for h in heads]
        v = [v_ref[rows, dv(h)] for h in heads]
        st = [st_ref[h] for h in heads]
        o = [_dg((q[h] * jnp.exp(b[h])).astype(BF16), st[h].astype(BF16), NT) for h in heads]
        intra = []
        for h in heads:
            refs = [jnp.zeros((1, GLA_DK), F32)] + [b[h][s * SUB - 1:s * SUB, :] for s in range(1, nsub)]
            ref_row = jnp.concatenate([jnp.broadcast_to(r, (SUB, GLA_DK)) for r in refs], axis=0)
            qsc = q[h] * jnp.exp(b[h] - ref_row)
            q_cat = jnp.concatenate([jnp.where(sub_of_row == s, qsc, 0.0) for s in range(nsub)], axis=1)
            k_cat = jnp.concatenate([jnp.where(sub_of_row <= s, k[h] * jnp.exp(refs[s] - b[h]), 0.0)
                                     for s in range(nsub)], axis=1)
            sc = jnp.where(ci <= ri, _dg(q_cat.astype(BF16), k_cat.astype(BF16), NT), 0.0)
            intra.append(_dot(sc.astype(BF16), v[h]))
        for h in heads:
            bl = b[h][C - 1:C, :]
            kdec = (k[h] * jnp.exp(bl - b[h])).astype(BF16)
            st_ref[h] = st[h] * jnp.exp(bl) + _dg(v[h], kdec, TN)
        for h in heads:
            oh = o[h] + intra[h]
            gg = g_ref[rows, dv(h)].astype(F32)
            o_ref[rows, dv(h)] = (_rms(oh, ng_ref[...]) * (gg * _sigmoid(gg))).astype(BF16)


def _cast_slab(w2d, nsteps):
    rows, width = w2d.shape
    assert rows % nsteps == 0 and (rows // nsteps) % 16 == 0
    return rows // nsteps, width


def _gla(proj, la, ng, B, S, cb):
    T = B * S
    ncb = S // cb
    dkw = GLA_HEADS * GLA_DK
    row = lambda b, c: b * ncb + c
    return pl.pallas_call(
        functools.partial(_gla_kernel, cb=cb),
        grid=(B, ncb),
        in_specs=[
            pl.BlockSpec((cb, dkw), lambda b, c: (row(b, c), COL_GQ // dkw)),
            pl.BlockSpec((cb, dkw), lambda b, c: (row(b, c), COL_GK // dkw)),
            pl.BlockSpec((cb, GLA_WIDTH), lambda b, c: (row(b, c), COL_GV // GLA_WIDTH)),
            pl.BlockSpec((cb, GLA_WIDTH), lambda b, c: (row(b, c), COL_GG // GLA_WIDTH)),
            pl.BlockSpec((cb, dkw), lambda b, c: (row(b, c), 0)),
            pl.BlockSpec((1, GLA_DV), lambda b, c: (0, 0)),
        ],
        out_specs=pl.BlockSpec((cb, GLA_WIDTH), lambda b, c: (row(b, c), 0)),
        out_shape=jax.ShapeDtypeStruct((T, GLA_WIDTH), BF16),
        scratch_shapes=[pltpu.VMEM((GLA_HEADS, GLA_DV, GLA_DK), F32)],
        compiler_params=_params(("parallel", "arbitrary")),
    )(proj, proj, proj, proj, la, ng)


def _compress_kernel(hb_ref, pe_ref, w1_ref, w2_ref, o_ref, *, ncp):
    half = CMP_STRIDE * NSA_DH
    hb = hb_ref[0, 0].astype(F32)
    pe = pe_ref[0]
    top = (hb + pe[0:1]).astype(BF16)
    bot = (hb + pe[1:2]).astype(BF16)
    a = _dot(top, w1_ref[0, :half, :])
    bm = _dot(bot, w1_ref[0, half:, :])
    pre = a + pltpu.roll(bm, ncp - 1, 0)
    cdf = 0.5 * (1.0 + jnp.tanh(math.sqrt(2.0 / math.pi) * (pre + 0.044715 * (pre * pre * pre))))
    o_ref[0, 0] = _dot((pre * cdf).astype(BF16), w2_ref[0]).astype(BF16)


def _compress(hb, pe, w1, w2):
    _, BG, ncp, half = hb.shape
    return pl.pallas_call(
        functools.partial(_compress_kernel, ncp=ncp),
        grid=(2, BG),
        in_specs=[
            pl.BlockSpec((1, 1, ncp, half), lambda t, n: (t, n, 0, 0)),
            pl.BlockSpec((1, 2, half), lambda t, n: (t, 0, 0)),
            pl.BlockSpec((1, 2 * half, NSA_DH), lambda t, n: (t, 0, 0)),
            pl.BlockSpec((1, NSA_DH, NSA_DH), lambda t, n: (t, 0, 0)),
        ],
        out_specs=pl.BlockSpec((1, 1, ncp, NSA_DH), lambda t, n: (t, n, 0, 0)),
        out_shape=jax.ShapeDtypeStruct((2, BG, ncp, NSA_DH), BF16),
        compiler_params=_params(("parallel", "parallel")),
    )(hb, pe, w1, w2)


def _cmpattn_kernel(q_ref, kc_ref, vc_ref, bias_ref, ovt_ref, o_ref, sel_ref, *, ns, ncp):
    tq = ATT_T
    i = pl.program_id(1)
    kc = kc_ref[0, 0]
    vc = vc_ref[0, 0]
    pos = i * tq + lax.broadcasted_iota(I32, (tq, 1), 0)
    row_valid = jnp.where(pos >= CMP_LEN - 1, 1.0, 0.0)
    psum = jnp.zeros((tq, ncp), F32)
    outs = []
    for r in range(NSA_REP):
        q = q_ref[:, r * NSA_DH:(r + 1) * NSA_DH]
        s = _dg(q, kc, NT) + bias_ref[r]
        e = jnp.exp2(s - jnp.max(s, axis=-1, keepdims=True))
        p = e / jnp.sum(e, axis=-1, keepdims=True) * row_valid
        psum = psum + p
        outs.append(_dot(p.astype(BF16), vc))
    o_ref[...] = jnp.concatenate(outs, axis=1).astype(BF16)

    ph = psum.astype(BF16)
    pl_ = (psum - ph.astype(F32)).astype(BF16)
    imp = _dg(ovt_ref[...], ph, NT) + _dg(ovt_ref[...], pl_, NT)
    n_i = lax.broadcasted_iota(I32, (ns, tq), 0)
    cur = (i * tq + lax.broadcasted_iota(I32, (ns, tq), 1)) // SEL_BLOCK
    forced = (n_i == 0) | (n_i == cur) | (n_i == cur - 1)
    score = jnp.where(forced, FORCE_SCORE, jnp.where(n_i <= cur, imp, -1.0))
    cnt = jnp.zeros((ns, tq), F32)
    for m in range(ns):
        sm = score[m:m + 1, :]
        ge = jnp.where(sm >= score, 1.0, 0.0)
        gt = jnp.where(sm > score, 1.0, 0.0)
        cnt = cnt + jnp.where(n_i > m, ge, gt)
    sel_t = jnp.where(cnt < min(SEL_TOPN, ns), 1.0, 0.0)
    if ns < LANE:
        sel_t = jnp.concatenate([sel_t, jnp.zeros((LANE - ns, tq), F32)], axis=0)
    sel_ref[0] = sel_t.T.astype(BF16)


def _cmpattn(proj, cmp_kv, bias_c, ovt, B, S):
    T = B * S
    tq = ATT_T
    nq = S // tq
    ns = S // SEL_BLOCK
    ncp = S // CMP_STRIDE
    G = NSA_KV
    qcol = COL_NQ // (NSA_REP * NSA_DH)
    return pl.pallas_call(
        functools.partial(_cmpattn_kernel, ns=ns, ncp=ncp),
        grid=(G, nq, B),
        in_specs=[
            pl.BlockSpec((tq, NSA_REP * NSA_DH), lambda g, i, b: (b * nq + i, qcol + g)),
            pl.BlockSpec((1, 1, ncp, NSA_DH), lambda g, i, b: (0, b * G + g, 0, 0)),
            pl.BlockSpec((1, 1, ncp, NSA_DH), lambda g, i, b: (1, b * G + g, 0, 0)),
            pl.BlockSpec((NSA_REP, tq, ncp), lambda g, i, b: (g, i, 0)),
            pl.BlockSpec((ns, ncp), lambda g, i, b: (0, 0)),
        ],
        out_specs=[
            pl.BlockSpec((tq, NSA_REP * NSA_DH), lambda g, i, b: (b * nq + i, g)),
            pl.BlockSpec((1, tq, LANE), lambda g, i, b: (b * G + g, i, 0)),
        ],
        out_shape=[
            jax.ShapeDtypeStruct((T, NSA_WIDTH), BF16),
            jax.ShapeDtypeStruct((B * G, S, LANE), BF16),
        ],
        compiler_params=_params(("parallel", "parallel", "parallel")),
    )(proj, cmp_kv, cmp_kv, bias_c, ovt)


def _flash_update(s, v, m_ref, al_ref):
    reps = s.shape[1] // LANE
    m_old = m_ref[...]
    part = s[:, :LANE]
    for c in range(1, reps):
        part = jnp.maximum(part, s[:, c * LANE:(c + 1) * LANE])
    m_new = jnp.maximum(m_old, jnp.max(part, axis=-1, keepdims=True))
    alpha = jnp.exp2(m_old - m_new)
    p = jnp.exp2(s - jnp.concatenate([m_new] * reps, axis=1))
    v_ext = jnp.concatenate([v, jnp.ones(v.shape, BF16)], axis=1)
    al_ref[...] = jnp.concatenate([alpha, alpha], axis=1) * al_ref[...] + _dot(p.astype(BF16), v_ext)
    m_ref[...] = m_new


def _selwin_kernel(q_ref, ks_ref, vs_ref, kw_ref, vw_ref, sel_ref, d_ref, w4_ref, wf1_ref, wf2_ref,
                   osel_ref, owin_ref, wb1_ref, wb2_ref, m_ref, al_ref, s_ref):
    wb1_ref[...] = wf1_ref[...].astype(BF16)
    wb2_ref[...] = wf2_ref[...].astype(BF16)
    t = SEL_T
    R = NSA_REP
    i = pl.program_id(2)
    q = jnp.concatenate([q_ref[:, r * NSA_DH:(r + 1) * NSA_DH] for r in range(R)], axis=0)
    unsel = (1.0 - sel_ref[0].astype(F32)).astype(BF16)
    q_sel = jnp.concatenate([q, jnp.concatenate([unsel] * R, axis=0)], axis=1)
    c_blk = lax.broadcasted_iota(I32, (t, LANE), 0) // SEL_BLOCK
    n_i = lax.broadcasted_iota(I32, (t, LANE), 1)
    blk_per_tile = t // SEL_BLOCK

    def init():
        m_ref[...] = jnp.full(m_ref.shape, NEG, F32)
        al_ref[...] = jnp.zeros(al_ref.shape, F32)

    def tile(ref, j):
        return ref[pl.ds(pl.multiple_of(j * t, t), t), :]

    def logits(k_ref, j, add):
        s = _dg(q, tile(k_ref, j), NT).reshape(R, t, t)
        return (s + add).reshape(R * t, t)

    def sel_logits(j):
        own_block = jnp.where(n_i == blk_per_tile * j + c_blk, NEG, 0.0).astype(BF16)
        return _dg(q_sel, jnp.concatenate([tile(ks_ref, j), own_block], axis=1), NT)

    def write(o_ref, al):
        o = al[:, :NSA_DH] / al[:, NSA_DH:]
        o_ref[...] = jnp.concatenate([o[r * t:(r + 1) * t] for r in range(R)], axis=1).astype(BF16)

    init()

    jn = jnp.maximum(i - 1, 0)
    s_ref[...] = sel_logits(0)

    def far(j, c):
        s = s_ref[...]
        s_ref[...] = sel_logits(j + 1)
        _flash_update(s, tile(vs_ref, j), m_ref, al_ref)
        return c

    def with_bias(s, add):
        return (s.reshape(R, t, t) + add).reshape(R * t, t)

    lax.fori_loop(0, jn, far, 0)
    off = jnp.where(i == 0, NEG, 0.0)
    s_near = s_ref[...]
    s_diag = sel_logits(i)
    _flash_update(with_bias(s_near, d_ref[:, 1] + off), tile(vs_ref, jn), m_ref, al_ref)
    _flash_update(with_bias(s_diag, d_ref[:, 0]), tile(vs_ref, i), m_ref, al_ref)
    write(osel_ref, al_ref[...])

    init()
    pre = WINDOW // t
    for o in range(pre, -1, -1):
        jw = i - o
        jc = jnp.maximum(jw, 0)
        off = jnp.where(jw < 0, NEG, 0.0)
        if o == 0:
            add = d_ref[:, 0] + off
        elif o == 1:
            add = d_ref[:, 1] + off
        elif o == pre:
            add = (w4_ref[...] + off)[None]
        else:
            add = off
        _flash_update(logits(kw_ref, jc, add), tile(vw_ref, jc), m_ref, al_ref)
    write(owin_ref, al_ref[...])


def _selwin(proj, sel, dtiles, w4, wf1, wf2, B, S):
    T = B * S
    t = SEL_T
    nq = S // t
    G = NSA_KV
    qcol = COL_NQ // (NSA_REP * NSA_DH)
    kv = lambda col: pl.BlockSpec((S, NSA_DH), lambda b, g, i, col=col: (b, col // NSA_DH + g))
    slab = lambda wf: pl.BlockSpec(_cast_slab(wf, B * G * nq), lambda b, g, i: ((b * G + g) * nq + i, 0))
    return pl.pallas_call(
        _selwin_kernel,
        grid=(B, G, nq),
        in_specs=[
            pl.BlockSpec((t, NSA_REP * NSA_DH), lambda b, g, i: (b * nq + i, qcol + g)),
            kv(COL_KS), kv(COL_VS), kv(COL_KW), kv(COL_VW),
            pl.BlockSpec((1, t, LANE), lambda b, g, i: (b * G + g, i, 0)),
            pl.BlockSpec((NSA_REP, 2, t, t), lambda b, g, i: (g, 0, 0, 0)),
            pl.BlockSpec((t, t), lambda b, g, i: (0, 0)),
            slab(wf1), slab(wf2),
        ],
        out_specs=[
            pl.BlockSpec((t, NSA_REP * NSA_DH), lambda b, g, i: (b * nq + i, g)),
            pl.BlockSpec((t, NSA_REP * NSA_DH), lambda b, g, i: (b * nq + i, g)),
            slab(wf1), slab(wf2),
        ],
        out_shape=[
            jax.ShapeDtypeStruct((T, NSA_WIDTH), BF16),
            jax.ShapeDtypeStruct((T, NSA_WIDTH), BF16),
            jax.ShapeDtypeStruct(wf1.shape, BF16),
            jax.ShapeDtypeStruct(wf2.shape, BF16),
        ],
        scratch_shapes=[
            pltpu.VMEM((NSA_REP * t, LANE), F32),
            pltpu.VMEM((NSA_REP * t, 2 * NSA_DH), F32),
            pltpu.VMEM((NSA_REP * t, t), F32),
        ],
        compiler_params=_params(("parallel", "parallel", "parallel")),
    )(proj, proj, proj, proj, proj, sel, dtiles, w4, wf1, wf2)


def _outproj_kernel(x_ref, og_ref, oc_ref, os_ref, ow_ref, nar_ref, ng_ref, wo_ref, fg_ref,
                    rwh_ref, rwl_ref, rb_ref, h1_ref, hn_ref, ti_ref, tw_ref):
    gates = _sigmoid(nar_ref[...])
    parts = []
    for h in range(NSA_HEADS):
        cs = slice(h * NSA_DH, (h + 1) * NSA_DH)
        g0 = gates[:, GATE_COL0 + h:GATE_COL0 + h + 1]
        g1 = gates[:, GATE_COL0 + NSA_HEADS + h:GATE_COL0 + NSA_HEADS + h + 1]
        g2 = gates[:, GATE_COL0 + 2 * NSA_HEADS + h:GATE_COL0 + 2 * NSA_HEADS + h + 1]
        parts.append(g0 * oc_ref[:, cs].astype(F32) + g1 * os_ref[:, cs].astype(F32)
                     + g2 * ow_ref[:, cs].astype(F32))
    on = _rms(jnp.concatenate(parts, axis=1), ng_ref[...]).astype(BF16)
    h1 = x_ref[...] + _dot(og_ref[...], wo_ref[:GLA_WIDTH, :]) + _dot(on, wo_ref[GLA_WIDTH:, :])
    h1_ref[...] = h1
    hn = _rms(h1, fg_ref[...])
    half = hn.shape[1] // 2
    _store_rows_tiled(hn_ref, 0, hn.shape[0], _pack_pairs(hn[:, :half], hn[:, half:]))

    hh = hn.astype(BF16)
    hl = (hn - hh.astype(F32)).astype(BF16)
    both = _dot(hh, jnp.concatenate([rwh_ref[...], rwl_ref[...]], axis=1))
    lg = both[:, :LANE] + both[:, LANE:] + _dot(hl, rwh_ref[...]) + rb_ref[...]
    lane = lax.broadcasted_iota(I32, lg.shape, 1)
    vals, idxs = [], []
    for _ in range(TOP_K):
        mx = jnp.max(lg, axis=-1, keepdims=True)
        ix = jnp.min(jnp.where(lg == mx, lane, LANE), axis=-1, keepdims=True)
        vals.append(mx)
        idxs.append(ix)
        lg = jnp.where(lane == ix, -3e38, lg)
    es = [jnp.exp(v - vals[0]) for v in vals]
    den = es[0] + es[1] + es[2] + es[3]
    ti = jnp.zeros(lg.shape, I32)
    tw = jnp.zeros(lg.shape, F32)
    for kk in range(TOP_K):
        ti = jnp.where(lane == kk, idxs[kk], ti)
        tw = jnp.where(lane == kk, es[kk] / den, tw)
    ti_ref[...] = ti
    tw_ref[...] = tw


def _outproj(x2, og, oc, osel, ow, nar, ng, wo, fg, rwh, rwl, rb, tm):
    T, D = x2.shape
    row = lambda w: pl.BlockSpec((tm, w), lambda i: (i, 0))
    full = lambda a: pl.BlockSpec(a.shape, lambda i: (0,) * a.ndim)
    return pl.pallas_call(
        _outproj_kernel,
        grid=(T // tm,),
        in_specs=[row(D), row(GLA_WIDTH), row(NSA_WIDTH), row(NSA_WIDTH), row(NSA_WIDTH), row(LANE),
                  full(ng), full(wo), full(fg), full(rwh), full(rwl), full(rb)],
        out_specs=[row(D), pl.BlockSpec((tm * SUBLANE, LANE), lambda i: (i, 0)), row(LANE), row(LANE)],
        out_shape=[
            jax.ShapeDtypeStruct((T, D), F32),
            jax.ShapeDtypeStruct((T * SUBLANE, LANE), U32),
            jax.ShapeDtypeStruct((T, LANE), I32),
            jax.ShapeDtypeStruct((T, LANE), F32),
        ],
        compiler_params=_params(("parallel",)),
    )(x2, og, oc, osel, ow, nar, ng, wo, fg, rwh, rwl, rb)


def _moe_kernel(te_ref, tv_ref, tok_hbm, dst_hbm, hn_hbm, wgt_ref, w1g_ref, w1l_ref, b1g_ref, b1l_ref,
                w2_ref, b2_ref, out_hbm, tok_smem, dst_smem, xbuf, xb_ref, act_ref, ybuf,
                sem_tok, sem_dst, sem_in, sem_out, *, tm, nf, n_tiles):
    i = pl.program_id(0)
    f = pl.program_id(1)
    slot = i % 2
    valid = tv_ref[i]
    sub = MOE_SUB
    nsub = tm // sub
    half = SUBLANE * LANE
    sl8 = SUBLANE

    def tile_of(r):
        return pl.ds(pl.multiple_of(r * sl8, sl8), sl8)

    def tok_copy(t, sl):
        return pltpu.make_async_copy(tok_hbm.at[t], tok_smem.at[sl], sem_tok.at[sl])

    def dst_copy():
        return pltpu.make_async_copy(dst_hbm.at[i], dst_smem, sem_dst)

    def issue_gather(t, sl):
        for s in range(nsub):
            @pl.when(s * sub < tv_ref[t])
            def _():
                def body(r, c):
                    pltpu.make_async_copy(hn_hbm.at[tile_of(tok_smem[sl, r]), :],
                                          xbuf.at[sl, tile_of(r), :], sem_in.at[sl, s]).start()
                    return c

                lax.fori_loop(s * sub, (s + 1) * sub, body, 0, unroll=DMA_UNROLL)

    def wait_gather(sl, s):
        pltpu.make_async_copy(hn_hbm.at[pl.ds(0, sub * sl8), :], xbuf.at[sl, pl.ds(s * sub * sl8, sub * sl8), :],
                              sem_in.at[sl, s]).wait()

    def wait_scatter(count):
        for s in range(nsub):
            @pl.when(s * sub < count)
            def _():
                pltpu.make_async_copy(ybuf.at[pl.ds(s * sub * sl8, sub * sl8), :],
                                      out_hbm.at[pl.ds(0, sub * sl8), :], sem_out).wait()

    @pl.when(f == 0)
    def _():
        @pl.when(i == 0)
        def _():
            cp = tok_copy(0, 0)
            cp.start()
            cp.wait()
            issue_gather(0, 0)

        @pl.when(i + 1 < n_tiles)
        def _():
            tok_copy(i + 1, 1 - slot).start()

        dst_copy().start()
        for s in range(nsub):
            @pl.when(s * sub < valid)
            def _():
                wait_gather(slot, s)
                for j, w in enumerate(_load_rows_tiled(xbuf, s * sub, sub, lead=(slot,))):
                    lo, hi = _unpack_pairs(w)
                    xb_ref[s * sub:(s + 1) * sub, j * LANE:(j + 1) * LANE] = lo.astype(BF16)
                    xb_ref[s * sub:(s + 1) * sub, half + j * LANE:half + (j + 1) * LANE] = hi.astype(BF16)

        @pl.when(i + 1 < n_tiles)
        def _():
            tok_copy(i + 1, 1 - slot).wait()
            issue_gather(i + 1, 1 - slot)

    for s in range(nsub):
        @pl.when(s * sub < valid)
        def _():
            rows = slice(s * sub, (s + 1) * sub)
            xb = xb_ref[rows, :]
            glu = jnp.minimum(_dot(xb, w1g_ref[0]) + b1g_ref[0], SWIGLU_LIMIT)
            lin = jnp.clip(_dot(xb, w1l_ref[0]) + b1l_ref[0], -SWIGLU_LIMIT, SWIGLU_LIMIT)
            act_ref[f, rows, :] = (glu * _sigmoid(SWIGLU_ALPHA * glu) * (lin + 1.0)).astype(BF16)

    @pl.when(f == nf - 1)
    def _():
        @pl.when(i > 0)
        def _():
            wait_scatter(tv_ref[jnp.maximum(i - 1, 0)])

        dst_copy().wait()
        for s in range(nsub):
            @pl.when(s * sub < valid)
            def _():
                rows = slice(s * sub, (s + 1) * sub)
                fc = act_ref.shape[2]
                y = _dot(act_ref[0, rows, :], w2_ref[0, :fc, :])
                for c in range(1, nf):
                    y = y + _dot(act_ref[c, rows, :], w2_ref[0, c * fc:(c + 1) * fc, :])
                y = (y + b2_ref[0]) * wgt_ref[0, rows, :]
                _store_rows_tiled(ybuf, s * sub, sub, _pack_pairs(y[:, :half], y[:, half:]))

                def body(r, c):
                    pltpu.make_async_copy(ybuf.at[tile_of(r), :], out_hbm.at[tile_of(dst_smem[r]), :],
                                          sem_out).start()
                    return c

                lax.fori_loop(s * sub, (s + 1) * sub, body, 0, unroll=DMA_UNROLL)

        @pl.when(i == n_tiles - 1)
        def _():
            wait_scatter(valid)
            ybuf[...] = jnp.zeros_like(ybuf)
            dump = pltpu.make_async_copy(ybuf, out_hbm.at[pl.ds(out_hbm.shape[0] - tm * sl8, tm * sl8), :], sem_out)
            dump.start()
            dump.wait()


def _moe(tile_exp, tile_valid, tok, dst, hn, wgt, w1, b1, w2, b2, n_out_rows, tm, fc):
    n_tiles = tok.shape[0]
    D = w1.shape[1]
    assert D == 2 * SUBLANE * LANE
    dff = w2.shape[1]
    nf = dff // fc
    grid_spec = pltpu.PrefetchScalarGridSpec(
        num_scalar_prefetch=2,
        grid=(n_tiles, nf),
        in_specs=[
            pl.BlockSpec(memory_space=pl.ANY),
            pl.BlockSpec(memory_space=pl.ANY),
            pl.BlockSpec(memory_space=pl.ANY),
            pl.BlockSpec((1, tm, 1), lambda i, f, te, tv: (i, 0, 0)),
            pl.BlockSpec((1, D, fc), lambda i, f, te, tv: (te[i], 0, f)),
            pl.BlockSpec((1, D, fc), lambda i, f, te, tv: (te[i], 0, nf + f)),
            pl.BlockSpec((1, 1, fc), lambda i, f, te, tv: (te[i], 0, f)),
            pl.BlockSpec((1, 1, fc), lambda i, f, te, tv: (te[i], 0, nf + f)),
            pl.BlockSpec((1, dff, D), lambda i, f, te, tv: (te[i], 0, 0)),
            pl.BlockSpec((1, 1, D), lambda i, f, te, tv: (te[i], 0, 0)),
        ],
        out_specs=pl.BlockSpec(memory_space=pl.ANY),
        scratch_shapes=[
            pltpu.SMEM((2, tm), I32),
            pltpu.SMEM((tm,), I32),
            pltpu.VMEM((2, tm * SUBLANE, LANE), U32),
            pltpu.VMEM((tm, D), BF16),
            pltpu.VMEM((nf, tm, fc), BF16),
            pltpu.VMEM((tm * SUBLANE, LANE), U32),
            pltpu.SemaphoreType.DMA((2,)),
            pltpu.SemaphoreType.DMA(()),
            pltpu.SemaphoreType.DMA((2, tm // MOE_SUB)),
            pltpu.SemaphoreType.DMA(()),
        ],
    )
    return pl.pallas_call(
        functools.partial(_moe_kernel, tm=tm, nf=nf, n_tiles=n_tiles),
        grid_spec=grid_spec,
        out_shape=jax.ShapeDtypeStruct((n_out_rows * SUBLANE, LANE), U32),
        compiler_params=_params(("arbitrary", "arbitrary")),
    )(tile_exp, tile_valid, tok, dst, hn, wgt, w1, w1, b1, b1, w2, b2)


def _combine_kernel(h1_ref, y0_ref, y1_ref, y2_ref, y3_ref, g_ref, o_ref):
    h = h1_ref[...]
    tm = h.shape[0]
    for y_ref in (y0_ref, y1_ref, y2_ref, y3_ref):
        pairs = [_unpack_pairs(w) for w in _load_rows_tiled(y_ref, 0, tm)]
        h = h + jnp.concatenate([p[0] for p in pairs] + [p[1] for p in pairs], axis=1)
    o_ref[...] = _rms(h, g_ref[...])


def _combine(ys, h1, g, tm):
    T, D = h1.shape
    nb = T // tm
    slab = lambda k: pl.BlockSpec((tm * SUBLANE, LANE), lambda i, k=k: (k * nb + i, 0))
    return pl.pallas_call(
        _combine_kernel,
        grid=(nb,),
        in_specs=[pl.BlockSpec((tm, D), lambda i: (i, 0))] + [slab(k) for k in range(TOP_K)]
        + [pl.BlockSpec((1, D), lambda i: (0, 0))],
        out_specs=pl.BlockSpec((tm, D), lambda i: (i, 0)),
        out_shape=jax.ShapeDtypeStruct((T, D), F32),
        compiler_params=_params(("parallel",)),
    )(h1, ys, ys, ys, ys, g)


def _t5_bucket(dist):
    n = jnp.maximum(dist, 0)
    exact = N_BUCKETS // 2
    scaled = jnp.log(jnp.maximum(n, 1).astype(F32) / exact) / math.log(MAX_DISTANCE / exact)
    large = exact + (scaled * (N_BUCKETS - exact)).astype(I32)
    return jnp.where(n < exact, n, jnp.minimum(large, N_BUCKETS - 1)).astype(I32)


def _bias_tables(rel_bias, S):
    shifted = ((rel_bias - rel_bias[N_BUCKETS - 1][None, :]) * LOG2E).astype(F32)

    def lookup(dist):
        bucket = _t5_bucket(dist)
        out = jnp.zeros((NSA_HEADS,) + dist.shape, F32)
        for kb in range(N_BUCKETS - 1):
            out = jnp.where((bucket == kb)[None], shifted[kb].reshape((NSA_HEADS,) + (1,) * dist.ndim), out)
        return jnp.where((dist >= 0)[None], out, NEG)

    t = SEL_T
    r = jnp.arange(t)[:, None]
    c = jnp.arange(t)[None, :]
    dtiles = jnp.stack([lookup(r - c), lookup(t + r - c)], axis=1)
    w4 = jnp.where(r < c, 0.0, NEG).astype(F32)
    ncp = S // CMP_STRIDE
    tq = ATT_T
    nq = S // tq
    shift = tq // CMP_STRIDE
    lead = shift * (nq - 1)
    cols = jnp.arange(ncp + lead)[None, :] - lead
    pattern = lookup(jnp.arange(tq)[:, None] - CMP_STRIDE * cols - (CMP_LEN - 1))
    bias_c = jnp.concatenate([pattern[:, :, lead - shift * i:lead - shift * i + ncp] for i in range(nq)], axis=1)
    return dtiles, w4, bias_c


def _overlap_t(S):
    ncp = S // CMP_STRIDE
    ns = S // SEL_BLOCK
    cs = jnp.arange(ncp)[None, :] * CMP_STRIDE
    ss = jnp.arange(ns)[:, None] * SEL_BLOCK
    ov = (cs < ss + SEL_BLOCK) & (cs + CMP_LEN > ss) & (jnp.arange(ncp)[None, :] < ncp - 1)
    return ov.astype(BF16)


def _route(top_idx, top_w, n_exp, tm):
    T = top_idx.shape[0]
    A = T * TOP_K
    e_flat = top_idx.reshape(A)
    w_flat = top_w.reshape(A)
    _, order = lax.sort_key_val(e_flat, jnp.arange(A, dtype=I32), is_stable=True)
    counts = jnp.sum((e_flat[:, None] == jnp.arange(n_exp, dtype=I32)[None, :]).astype(I32), axis=0)
    padded = (counts + tm - 1) // tm * tm
    start = jnp.cumsum(counts) - counts
    pend = jnp.cumsum(padded)
    pstart = pend - padded
    n_tiles = A // tm + n_exp
    tile_row0 = jnp.arange(n_tiles, dtype=I32) * tm
    tile_exp = jnp.minimum(jnp.sum((pend[None, :] <= tile_row0[:, None]).astype(I32), axis=1), n_exp - 1)
    tile_valid = jnp.clip(pstart[tile_exp] + counts[tile_exp] - tile_row0, 0, tm).astype(I32)
    off = tile_row0[:, None] - pstart[tile_exp][:, None] + jnp.arange(tm, dtype=I32)[None, :]
    live = jnp.arange(tm, dtype=I32)[None, :] < tile_valid[:, None]
    src = order[jnp.clip(start[tile_exp][:, None] + off, 0, A - 1)]
    buf_tok = jnp.where(live, src // TOP_K, 0)
    buf_w = jnp.where(live, w_flat[src], 0.0)
    buf_dst = jnp.where(live, (src % TOP_K) * T + src // TOP_K, TOP_K * T + jnp.arange(tm, dtype=I32)[None, :])
    return buf_tok, buf_dst, buf_w[:, :, None], tile_exp, tile_valid


def kernel(x, norm_mix_g, w_in, gla_wa2, gla_ba, gla_norm_g, cmp_pe_k, cmp_wk1, cmp_wk2, cmp_pe_v,
           cmp_wv1, cmp_wv2, nsa_norm_g, rel_bias, w_out, norm_ffn_g, router_w, router_b, exp_w1,
           exp_b1, exp_w2, exp_b2, norm_final_g):
    B, S, D = x.shape
    T = B * S
    G = NSA_KV
    n_exp = router_w.shape[-1]
    assert w_in.shape[0] == 1 and S % 1024 == 0 and S // SEL_BLOCK <= LANE and n_exp <= LANE
    x2 = x.reshape(T, D)

    w = w_in[0]
    a0 = COL_NQ
    n0 = a0 + GLA_RANK
    g0 = n0 + (N_MAIN - COL_NQ)
    w_main = jnp.concatenate([
        w[:, :COL_GK] * (GLA_DK ** -0.5), w[:, COL_GK:a0],
        w[:, n0:n0 + NSA_WIDTH] * (NSA_DH ** -0.5 * LOG2E), w[:, n0 + NSA_WIDTH:g0]], axis=1).astype(BF16)
    w_nar = jnp.concatenate([w[:, a0:n0], w[:, g0:], jnp.zeros((D, LANE - GLA_RANK - 3 * NSA_HEADS), F32)],
                            axis=1).astype(BF16)
    wa2p = jnp.concatenate([gla_wa2[0], jnp.zeros((LANE - GLA_RANK, GLA_HEADS * GLA_DK), F32)], axis=0).astype(BF16)

    tm1 = min(1024, T)
    proj, log_a, nar = _inproj(x2, norm_mix_g, w_main, w_nar, wa2p, gla_ba, tm1, N_MAIN // 4)

    e_w1, e_w2 = exp_w1[0], exp_w2[0]
    o_gla = _gla(proj, log_a, gla_norm_g, B, S, min(256, S))

    ncp = S // CMP_STRIDE
    half = CMP_STRIDE * NSA_DH

    def stride_groups(col):
        t4 = proj[:, col:col + KV_W].reshape(B, ncp, CMP_STRIDE, G, NSA_DH)
        return t4.transpose(0, 3, 1, 2, 4).reshape(B * G, ncp, half)

    hb = jnp.stack([stride_groups(COL_KC), stride_groups(COL_VC)], axis=0)
    pe = jnp.stack([cmp_pe_k[0].reshape(2, half), cmp_pe_v[0].reshape(2, half)], axis=0)
    w1c = jnp.stack([cmp_wk1[0], cmp_wv1[0]], axis=0).astype(BF16)
    w2c = jnp.stack([cmp_wk2[0], cmp_wv2[0]], axis=0).astype(BF16)
    cmp_kv = _compress(hb, pe, w1c, w2c)

    dtiles, w4, bias_c = _bias_tables(rel_bias, S)
    o_cmp, sel = _cmpattn(proj, cmp_kv, bias_c, _overlap_t(S), B, S)
    o_sel, o_win, w1b, w2b = _selwin(proj, sel, dtiles, w4, e_w1.reshape(-1, e_w1.shape[-1]),
                                     e_w2.reshape(-1, e_w2.shape[-1]), B, S)
    w1b, w2b = w1b.reshape(e_w1.shape), w2b.reshape(e_w2.shape)

    rw = jnp.concatenate([router_w[0], jnp.zeros((D, LANE - n_exp), F32)], axis=1)
    rwh = rw.astype(BF16)
    rwl = (rw - rwh.astype(F32)).astype(BF16)
    rb = jnp.concatenate([router_b[0], jnp.full((LANE - n_exp,), NEG, F32)])[None, :]
    h1, hn, top_i, top_w = _outproj(x2, o_gla, o_cmp, o_sel, o_win, nar, nsa_norm_g, w_out[0].astype(BF16),
                                    norm_ffn_g, rwh, rwl, rb, min(512, T))

    tm6 = min(1024, T)
    tok, dst, wgt, tile_exp, tile_valid = _route(top_i[:, :TOP_K], top_w[:, :TOP_K], n_exp, tm6)
    ys = _moe(tile_exp, tile_valid, tok, dst, hn, wgt, w1b, exp_b1[0][:, None, :],
              w2b, exp_b2[0][:, None, :], TOP_K * T + tm6, tm6, 512)
    out = _combine(ys, h1, norm_final_g[None, :], min(512, T))
    return out.reshape(B, S, D)
```

```python
import functools
import math

import numpy as np
import jax
import jax.numpy as jnp
from jax import lax
from jax.experimental import pallas as pl
from jax.experimental.pallas import tpu as pltpu

F32 = jnp.float32
BF16 = jnp.bfloat16
I32 = jnp.int32
U32 = jnp.uint32

GLA_HEADS = 4
GLA_DK = 128
GLA_DV = 256
GLA_RANK = 16
GLA_TAU = 16.0
NSA_HEADS = 8
NSA_KV = 2
NSA_REP = NSA_HEADS // NSA_KV
NSA_DH = 128
CMP_LEN = 32
CMP_STRIDE = 16
SEL_BLOCK = 64
SEL_TOPN = 16
WINDOW = 512
N_BUCKETS = 32
MAX_DISTANCE = 128
TOP_K = 4
SWIGLU_ALPHA = 1.702
SWIGLU_LIMIT = 7.0
RMS_EPS = 1e-5
NEG = -1e30
FORCE_SCORE = 1e4
LOG2E = 1.4426950408889634

GLA_WIDTH = GLA_HEADS * GLA_DV
NSA_WIDTH = NSA_HEADS * NSA_DH
KV_W = NSA_KV * NSA_DH

COL_GQ = 0
COL_GK = COL_GQ + GLA_HEADS * GLA_DK
COL_GV = COL_GK + GLA_HEADS * GLA_DK
COL_GG = COL_GV + GLA_WIDTH
COL_NQ = COL_GG + GLA_WIDTH
COL_KC = COL_NQ + NSA_WIDTH
COL_VC = COL_KC + KV_W
COL_KS = COL_VC + KV_W
COL_VS = COL_KS + KV_W
COL_KW = COL_VS + KV_W
COL_VW = COL_KW + KV_W
N_MAIN = COL_VW + KV_W
LANE = 128
ATT_T = 512
SEL_T = 256
GLA_C = 64
GLA_SUB = 16
MOE_SUB = 512
DMA_UNROLL = 32
GATE_COL0 = GLA_RANK

VMEM_LIMIT = 56 * 1024 * 1024

NT = (((1,), (1,)), ((), ()))
TN = (((0,), (0,)), ((), ()))


def _dot(a, b):
    return jnp.dot(a, b, preferred_element_type=F32)


def _dg(a, b, dims):
    return lax.dot_general(a, b, dims, preferred_element_type=F32)


def _sigmoid(x):
    return 1.0 / (1.0 + jnp.exp(-x))


def _log_sigmoid(z):
    return -(jnp.maximum(-z, 0.0) + jnp.log(1.0 + jnp.exp(-jnp.abs(z))))


def _rms(x, g):
    return x * lax.rsqrt(jnp.mean(x * x, axis=-1, keepdims=True) + RMS_EPS) * g


def _split3(x):
    hi = x.astype(BF16)
    r1 = x - hi.astype(F32)
    mid = r1.astype(BF16)
    lo = (r1 - mid.astype(F32)).astype(BF16)
    return hi, mid, lo


def _pack_pairs(lo, hi):
    lo_bits = lax.bitcast_convert_type(lo.astype(BF16).astype(F32), U32) >> 16
    hi_bits = lax.bitcast_convert_type(hi.astype(BF16).astype(F32), U32) & jnp.uint32(0xFFFF0000)
    return lo_bits | hi_bits


def _unpack_pairs(w):
    lo = lax.bitcast_convert_type(w << 16, F32)
    hi = lax.bitcast_convert_type(w & jnp.uint32(0xFFFF0000), F32)
    return lo, hi


SUBLANE = 8


def _store_rows_tiled(ref, base, n, packed):
    for j in range(SUBLANE):
        ref[pl.ds(base * SUBLANE + j, n, stride=SUBLANE), :] = packed[:, j * LANE:(j + 1) * LANE]


def _load_rows_tiled(ref, base, n, lead=()):
    return [ref[lead + (pl.ds(base * SUBLANE + j, n, stride=SUBLANE), slice(None))] for j in range(SUBLANE)]


def _params(sem, vmem=VMEM_LIMIT):
    return pltpu.CompilerParams(dimension_semantics=sem, vmem_limit_bytes=vmem)


def _inproj_kernel(x_ref, g_ref, w_ref, wn_ref, wa2_ref, ba_ref, proj_ref, la_ref, nar_ref, xn_ref):
    @pl.when(pl.program_id(1) == 0)
    def _():
        xn = _rms(x_ref[...], g_ref[...]).astype(BF16)
        xn_ref[...] = xn
        nar = _dot(xn, wn_ref[...])
        nar_ref[...] = nar
        z = _dot(nar.astype(BF16), wa2_ref[...]) + ba_ref[...]
        la_ref[...] = _log_sigmoid(z) * (1.0 / GLA_TAU)

    proj_ref[...] = _dot(xn_ref[...], w_ref[...]).astype(BF16)


def _inproj(x2, g, w_main, w_nar, wa2p, ba, tm, tn):
    T, D = x2.shape
    nj = N_MAIN // tn
    return pl.pallas_call(
        _inproj_kernel,
        grid=(T // tm, nj),
        in_specs=[
            pl.BlockSpec((tm, D), lambda i, j: (i, 0)),
            pl.BlockSpec((1, D), lambda i, j: (0, 0)),
            pl.BlockSpec((D, tn), lambda i, j: (0, j)),
            pl.BlockSpec((D, LANE), lambda i, j: (0, 0)),
            pl.BlockSpec((LANE, GLA_HEADS * GLA_DK), lambda i, j: (0, 0)),
            pl.BlockSpec((1, GLA_HEADS * GLA_DK), lambda i, j: (0, 0)),
        ],
        out_specs=[
            pl.BlockSpec((tm, tn), lambda i, j: (i, j)),
            pl.BlockSpec((tm, GLA_HEADS * GLA_DK), lambda i, j: (i, 0)),
            pl.BlockSpec((tm, LANE), lambda i, j: (i, 0)),
        ],
        out_shape=[
            jax.ShapeDtypeStruct((T, N_MAIN), BF16),
            jax.ShapeDtypeStruct((T, GLA_HEADS * GLA_DK), F32),
            jax.ShapeDtypeStruct((T, LANE), F32),
        ],
        scratch_shapes=[pltpu.VMEM((tm, D), BF16)],
        compiler_params=_params(("parallel", "arbitrary")),
    )(x2, g, w_main, w_nar, wa2p, ba)


def _gla_kernel(q_ref, k_ref, v_ref, g_ref, la_ref, ng_ref, o_ref, st_ref, *, cb):
    @pl.when(pl.program_id(1) == 0)
    def _():
        st_ref[...] = jnp.zeros_like(st_ref)

    C, SUB = GLA_C, GLA_SUB
    heads = range(GLA_HEADS)
    dk = lambda h: slice(h * GLA_DK, (h + 1) * GLA_DK)
    dv = lambda h: slice(h * GLA_DV, (h + 1) * GLA_DV)
    ri = lax.broadcasted_iota(I32, (C, C), 0)
    ci = lax.broadcasted_iota(I32, (C, C), 1)
    tri = jnp.where(ri >= ci, 1.0, 0.0).astype(BF16)
    nsub = C // SUB
    sub_of_row = lax.broadcasted_iota(I32, (C, GLA_DK), 0) // SUB
    for n in range(cb // C):
        rows = slice(n * C, (n + 1) * C)
        b = []
        for h in heads:
            hi, mid, lo = _split3(la_ref[rows, dk(h)])
            b.append(_dot(tri, hi) + _dot(tri, mid) + _dot(tri, lo))
        q = [q_ref[rows, dk(h)].astype(F32) for h in heads]
        k = [k_ref[rows, dk(h)].astype(F32) for h in heads]
        v = [v_ref[rows, dv(h)] for h in heads]
        st = [st_ref[h] for h in heads]
        o = [_dg((q[h] * jnp.exp(b[h])).astype(BF16), st[h].astype(BF16), NT) for h in heads]
        intra = []
        for h in heads:
            refs = [jnp.zeros((1, GLA_DK), F32)] + [b[h][s * SUB - 1:s * SUB, :] for s in range(1, nsub)]
            ref_row = jnp.concatenate([jnp.broadcast_to(r, (SUB, GLA_DK)) for r in refs], axis=0)
            qsc = q[h] * jnp.exp(b[h] - ref_row)
            q_cat = jnp.concatenate([jnp.where(sub_of_row == s, qsc, 0.0) for s in range(nsub)], axis=1)
            k_cat = jnp.concatenate([jnp.where(sub_of_row <= s, k[h] * jnp.exp(refs[s] - b[h]), 0.0)
                                     for s in range(nsub)], axis=1)
            sc = jnp.where(ci <= ri, _dg(q_cat.astype(BF16), k_cat.astype(BF16), NT), 0.0)
            intra.append(_dot(sc.astype(BF16), v[h]))
        for h in heads:
            bl = b[h][C - 1:C, :]
            kdec = (k[h] * jnp.exp(bl - b[h])).astype(BF16)
            st_ref[h] = st[h] * jnp.exp(bl) + _dg(v[h], kdec, TN)
        for h in heads:
            oh = o[h] + intra[h]
            gg = g_ref[rows, dv(h)].astype(F32)
            o_ref[rows, dv(h)] = (_rms(oh, ng_ref[...]) * (gg * _sigmoid(gg))).astype(BF16)


def _cast_slab(w2d, nsteps):
    rows, width = w2d.shape
    assert rows % nsteps == 0 and (rows // nsteps) % 16 == 0
    return rows // nsteps, width


def _gla(proj, la, ng, B, S, cb):
    T = B * S
    ncb = S // cb
    dkw = GLA_HEADS * GLA_DK
    row = lambda b, c: b * ncb + c
    return pl.pallas_call(
        functools.partial(_gla_kernel, cb=cb),
        grid=(B, ncb),
        in_specs=[
            pl.BlockSpec((cb, dkw), lambda b, c: (row(b, c), COL_GQ // dkw)),
            pl.BlockSpec((cb, dkw), lambda b, c: (row(b, c), COL_GK // dkw)),
            pl.BlockSpec((cb, GLA_WIDTH), lambda b, c: (row(b, c), COL_GV // GLA_WIDTH)),
            pl.BlockSpec((cb, GLA_WIDTH), lambda b, c: (row(b, c), COL_GG // GLA_WIDTH)),
            pl.BlockSpec((cb, dkw), lambda b, c: (row(b, c), 0)),
            pl.BlockSpec((1, GLA_DV), lambda b, c: (0, 0)),
        ],
        out_specs=pl.BlockSpec((cb, GLA_WIDTH), lambda b, c: (row(b, c), 0)),
        out_shape=jax.ShapeDtypeStruct((T, GLA_WIDTH), BF16),
        scratch_shapes=[pltpu.VMEM((GLA_HEADS, GLA_DV, GLA_DK), F32)],
        compiler_params=_params(("parallel", "arbitrary")),
    )(proj, proj, proj, proj, la, ng)


def _compress_kernel(hb_ref, pe_ref, w1_ref, w2_ref, o_ref, *, ncp):
    half = CMP_STRIDE * NSA_DH
    hb = hb_ref[0, 0].astype(F32)
    pe = pe_ref[0]
    top = (hb + pe[0:1]).astype(BF16)
    bot = (hb + pe[1:2]).astype(BF16)
    a = _dot(top, w1_ref[0, :half, :])
    bm = _dot(bot, w1_ref[0, half:, :])
    pre = a + pltpu.roll(bm, ncp - 1, 0)
    cdf = 0.5 * (1.0 + jnp.tanh(math.sqrt(2.0 / math.pi) * (pre + 0.044715 * (pre * pre * pre))))
    o_ref[0, 0] = _dot((pre * cdf).astype(BF16), w2_ref[0]).astype(BF16)


def _compress(hb, pe, w1, w2):
    _, BG, ncp, half = hb.shape
    return pl.pallas_call(
        functools.partial(_compress_kernel, ncp=ncp),
        grid=(2, BG),
        in_specs=[
            pl.BlockSpec((1, 1, ncp, half), lambda t, n: (t, n, 0, 0)),
            pl.BlockSpec((1, 2, half), lambda t, n: (t, 0, 0)),
            pl.BlockSpec((1, 2 * half, NSA_DH), lambda t, n: (t, 0, 0)),
            pl.BlockSpec((1, NSA_DH, NSA_DH), lambda t, n: (t, 0, 0)),
        ],
        out_specs=pl.BlockSpec((1, 1, ncp, NSA_DH), lambda t, n: (t, n, 0, 0)),
        out_shape=jax.ShapeDtypeStruct((2, BG, ncp, NSA_DH), BF16),
        compiler_params=_params(("parallel", "parallel")),
    )(hb, pe, w1, w2)


def _cmpattn_kernel(q_ref, kc_ref, vc_ref, bias_ref, ovt_ref, o_ref, sel_ref, *, ns, ncp):
    tq = ATT_T
    i = pl.program_id(1)
    kc = kc_ref[0, 0]
    vc = vc_ref[0, 0]
    pos = i * tq + lax.broadcasted_iota(I32, (tq, 1), 0)
    row_valid = jnp.where(pos >= CMP_LEN - 1, 1.0, 0.0)
    psum = jnp.zeros((tq, ncp), F32)
    outs = []
    for r in range(NSA_REP):
        q = q_ref[:, r * NSA_DH:(r + 1) * NSA_DH]
        s = _dg(q, kc, NT) + bias_ref[r]
        e = jnp.exp2(s - jnp.max(s, axis=-1, keepdims=True))
        p = e / jnp.sum(e, axis=-1, keepdims=True) * row_valid
        psum = psum + p
        outs.append(_dot(p.astype(BF16), vc))
    o_ref[...] = jnp.concatenate(outs, axis=1).astype(BF16)

    ph = psum.astype(BF16)
    pl_ = (psum - ph.astype(F32)).astype(BF16)
    imp = _dg(ovt_ref[...], ph, NT) + _dg(ovt_ref[...], pl_, NT)
    n_i = lax.broadcasted_iota(I32, (ns, tq), 0)
    cur = (i * tq + lax.broadcasted_iota(I32, (ns, tq), 1)) // SEL_BLOCK
    forced = (n_i == 0) | (n_i == cur) | (n_i == cur - 1)
    score = jnp.where(forced, FORCE_SCORE, jnp.where(n_i <= cur, imp, -1.0))
    cnt = jnp.zeros((ns, tq), F32)
    for m in range(ns):
        sm = score[m:m + 1, :]
        ge = jnp.where(sm >= score, 1.0, 0.0)
        gt = jnp.where(sm > score, 1.0, 0.0)
        cnt = cnt + jnp.where(n_i > m, ge, gt)
    sel_t = jnp.where(cnt < min(SEL_TOPN, ns), 1.0, 0.0)
    if ns < LANE:
        sel_t = jnp.concatenate([sel_t, jnp.zeros((LANE - ns, tq), F32)], axis=0)
    sel_ref[0] = sel_t.T.astype(BF16)


def _cmpattn(proj, cmp_kv, bias_c, ovt, B, S):
    T = B * S
    tq = ATT_T
    nq = S // tq
    ns = S // SEL_BLOCK
    ncp = S // CMP_STRIDE
    G = NSA_KV
    qcol = COL_NQ // (NSA_REP * NSA_DH)
    return pl.pallas_call(
        functools.partial(_cmpattn_kernel, ns=ns, ncp=ncp),
        grid=(G, nq, B),
        in_specs=[
            pl.BlockSpec((tq, NSA_REP * NSA_DH), lambda g, i, b: (b * nq + i, qcol + g)),
            pl.BlockSpec((1, 1, ncp, NSA_DH), lambda g, i, b: (0, b * G + g, 0, 0)),
            pl.BlockSpec((1, 1, ncp, NSA_DH), lambda g, i, b: (1, b * G + g, 0, 0)),
            pl.BlockSpec((NSA_REP, tq, ncp), lambda g, i, b: (g, i, 0)),
            pl.BlockSpec((ns, ncp), lambda g, i, b: (0, 0)),
        ],
        out_specs=[
            pl.BlockSpec((tq, NSA_REP * NSA_DH), lambda g, i, b: (b * nq + i, g)),
            pl.BlockSpec((1, tq, LANE), lambda g, i, b: (b * G + g, i, 0)),
        ],
        out_shape=[
            jax.ShapeDtypeStruct((T, NSA_WIDTH), BF16),
            jax.ShapeDtypeStruct((B * G, S, LANE), BF16),
        ],
        compiler_params=_params(("parallel", "parallel", "parallel")),
    )(proj, cmp_kv, cmp_kv, bias_c, ovt)


def _flash_update(s, v, m_ref, al_ref):
    reps = s.shape[1] // LANE
    m_old = m_ref[...]
    part = s[:, :LANE]
    for c in range(1, reps):
        part = jnp.maximum(part, s[:, c * LANE:(c + 1) * LANE])
    m_new = jnp.maximum(m_old, jnp.max(part, axis=-1, keepdims=True))
    alpha = jnp.exp2(m_old - m_new)
    p = jnp.exp2(s - jnp.concatenate([m_new] * reps, axis=1))
    v_ext = jnp.concatenate([v, jnp.ones(v.shape, BF16)], axis=1)
    al_ref[...] = jnp.concatenate([alpha, alpha], axis=1) * al_ref[...] + _dot(p.astype(BF16), v_ext)
    m_ref[...] = m_new


def _selwin_kernel(q_ref, ks_ref, vs_ref, kw_ref, vw_ref, sel_ref, d_ref, w4_ref, wf1_ref, wf2_ref,
                   osel_ref, owin_ref, wb1_ref, wb2_ref, m_ref, al_ref, s_ref):
    wb1_ref[...] = wf1_ref[...].astype(BF16)
    wb2_ref[...] = wf2_ref[...].astype(BF16)
    t = SEL_T
    R = NSA_REP
    i = pl.program_id(2)
    q = jnp.concatenate([q_ref[:, r * NSA_DH:(r + 1) * NSA_DH] for r in range(R)], axis=0)
    unsel = (1.0 - sel_ref[0].astype(F32)).astype(BF16)
    q_sel = jnp.concatenate([q, jnp.concatenate([unsel] * R, axis=0)], axis=1)
    c_blk = lax.broadcasted_iota(I32, (t, LANE), 0) // SEL_BLOCK
    n_i = lax.broadcasted_iota(I32, (t, LANE), 1)
    blk_per_tile = t // SEL_BLOCK

    def init():
        m_ref[...] = jnp.full(m_ref.shape, NEG, F32)
        al_ref[...] = jnp.zeros(al_ref.shape, F32)

    def tile(ref, j):
        return ref[pl.ds(pl.multiple_of(j * t, t), t), :]

    def logits(k_ref, j, add):
        s = _dg(q, tile(k_ref, j), NT).reshape(R, t, t)
        return (s + add).reshape(R * t, t)

    def sel_logits(j):
        own_block = jnp.where(n_i == blk_per_tile * j + c_blk, NEG, 0.0).astype(BF16)
        return _dg(q_sel, jnp.concatenate([tile(ks_ref, j), own_block], axis=1), NT)

    def write(o_ref, al):
        o = al[:, :NSA_DH] / al[:, NSA_DH:]
        o_ref[...] = jnp.concatenate([o[r * t:(r + 1) * t] for r in range(R)], axis=1).astype(BF16)

    init()

    jn = jnp.maximum(i - 1, 0)
    s_ref[...] = sel_logits(0)

    def far(j, c):
        s = s_ref[...]
        s_ref[...] = sel_logits(j + 1)
        _flash_update(s, tile(vs_ref, j), m_ref, al_ref)
        return c

    def with_bias(s, add):
        return (s.reshape(R, t, t) + add).reshape(R * t, t)

    lax.fori_loop(0, jn, far, 0)
    off = jnp.where(i == 0, NEG, 0.0)
    s_near = s_ref[...]
    s_diag = sel_logits(i)
    _flash_update(with_bias(s_near, d_ref[:, 1] + off), tile(vs_ref, jn), m_ref, al_ref)
    _flash_update(with_bias(s_diag, d_ref[:, 0]), tile(vs_ref, i), m_ref, al_ref)
    write(osel_ref, al_ref[...])

    init()
    pre = WINDOW // t
    for o in range(pre, -1, -1):
        jw = i - o
        jc = jnp.maximum(jw, 0)
        off = jnp.where(jw < 0, NEG, 0.0)
        if o == 0:
            add = d_ref[:, 0] + off
        elif o == 1:
            add = d_ref[:, 1] + off
        elif o == pre:
            add = (w4_ref[...] + off)[None]
        else:
            add = off
        _flash_update(logits(kw_ref, jc, add), tile(vw_ref, jc), m_ref, al_ref)
    write(owin_ref, al_ref[...])


def _selwin(proj, sel, dtiles, w4, wf1, wf2, B, S):
    T = B * S
    t = SEL_T
    nq = S // t
    G = NSA_KV
    qcol = COL_NQ // (NSA_REP * NSA_DH)
    kv = lambda col: pl.BlockSpec((S, NSA_DH), lambda b, g, i, col=col: (b, col // NSA_DH + g))
    slab = lambda wf: pl.BlockSpec(_cast_slab(wf, B * G * nq), lambda b, g, i: ((b * G + g) * nq + i, 0))
    return pl.pallas_call(
        _selwin_kernel,
        grid=(B, G, nq),
        in_specs=[
            pl.BlockSpec((t, NSA_REP * NSA_DH), lambda b, g, i: (b * nq + i, qcol + g)),
            kv(COL_KS), kv(COL_VS), kv(COL_KW), kv(COL_VW),
            pl.BlockSpec((1, t, LANE), lambda b, g, i: (b * G + g, i, 0)),
            pl.BlockSpec((NSA_REP, 2, t, t), lambda b, g, i: (g, 0, 0, 0)),
            pl.BlockSpec((t, t), lambda b, g, i: (0, 0)),
            slab(wf1), slab(wf2),
        ],
        out_specs=[
            pl.BlockSpec((t, NSA_REP * NSA_DH), lambda b, g, i: (b * nq + i, g)),
            pl.BlockSpec((t, NSA_REP * NSA_DH), lambda b, g, i: (b * nq + i, g)),
            slab(wf1), slab(wf2),
        ],
        out_shape=[
            jax.ShapeDtypeStruct((T, NSA_WIDTH), BF16),
            jax.ShapeDtypeStruct((T, NSA_WIDTH), BF16),
            jax.ShapeDtypeStruct(wf1.shape, BF16),
            jax.ShapeDtypeStruct(wf2.shape, BF16),
        ],
        scratch_shapes=[
            pltpu.VMEM((NSA_REP * t, LANE), F32),
            pltpu.VMEM((NSA_REP * t, 2 * NSA_DH), F32),
            pltpu.VMEM((NSA_REP * t, t), F32),
        ],
        compiler_params=_params(("parallel", "parallel", "parallel")),
    )(proj, proj, proj, proj, proj, sel, dtiles, w4, wf1, wf2)


def _outproj_kernel(x_ref, og_ref, oc_ref, os_ref, ow_ref, nar_ref, ng_ref, wo_ref, fg_ref,
                    rwh_ref, rwl_ref, rb_ref, h1_ref, hn_ref, ti_ref, tw_ref):
    gates = _sigmoid(nar_ref[...])
    parts = []
    for h in range(NSA_HEADS):
        cs = slice(h * NSA_DH, (h + 1) * NSA_DH)
        g0 = gates[:, GATE_COL0 + h:GATE_COL0 + h + 1]
        g1 = gates[:, GATE_COL0 + NSA_HEADS + h:GATE_COL0 + NSA_HEADS + h + 1]
        g2 = gates[:, GATE_COL0 + 2 * NSA_HEADS + h:GATE_COL0 + 2 * NSA_HEADS + h + 1]
        parts.append(g0 * oc_ref[:, cs].astype(F32) + g1 * os_ref[:, cs].astype(F32)
                     + g2 * ow_ref[:, cs].astype(F32))
    on = _rms(jnp.concatenate(parts, axis=1), ng_ref[...]).astype(BF16)
    h1 = x_ref[...] + _dot(og_ref[...], wo_ref[:GLA_WIDTH, :]) + _dot(on, wo_ref[GLA_WIDTH:, :])
    h1_ref[...] = h1
    hn = _rms(h1, fg_ref[...])
    half = hn.shape[1] // 2
    _store_rows_tiled(hn_ref, 0, hn.shape[0], _pack_pairs(hn[:, :half], hn[:, half:]))

    hh = hn.astype(BF16)
    hl = (hn - hh.astype(F32)).astype(BF16)
    both = _dot(hh, jnp.concatenate([rwh_ref[...], rwl_ref[...]], axis=1))
    lg = both[:, :LANE] + both[:, LANE:] + _dot(hl, rwh_ref[...]) + rb_ref[...]
    lane = lax.broadcasted_iota(I32, lg.shape, 1)
    vals, idxs = [], []
    for _ in range(TOP_K):
        mx = jnp.max(lg, axis=-1, keepdims=True)
        ix = jnp.min(jnp.where(lg == mx, lane, LANE), axis=-1, keepdims=True)
        vals.append(mx)
        idxs.append(ix)
        lg = jnp.where(lane == ix, -3e38, lg)
    es = [jnp.exp(v - vals[0]) for v in vals]
    den = es[0] + es[1] + es[2] + es[3]
    ti = jnp.zeros(lg.shape, I32)
    tw = jnp.zeros(lg.shape, F32)
    for kk in range(TOP_K):
        ti = jnp.where(lane == kk, idxs[kk], ti)
        tw = jnp.where(lane == kk, es[kk] / den, tw)
    ti_ref[...] = ti
    tw_ref[...] = tw


def _outproj(x2, og, oc, osel, ow, nar, ng, wo, fg, rwh, rwl, rb, tm):
    T, D = x2.shape
    row = lambda w: pl.BlockSpec((tm, w), lambda i: (i, 0))
    full = lambda a: pl.BlockSpec(a.shape, lambda i: (0,) * a.ndim)
    return pl.pallas_call(
        _outproj_kernel,
        grid=(T // tm,),
        in_specs=[row(D), row(GLA_WIDTH), row(NSA_WIDTH), row(NSA_WIDTH), row(NSA_WIDTH), row(LANE),
                  full(ng), full(wo), full(fg), full(rwh), full(rwl), full(rb)],
        out_specs=[row(D), pl.BlockSpec((tm * SUBLANE, LANE), lambda i: (i, 0)), row(LANE), row(LANE)],
        out_shape=[
            jax.ShapeDtypeStruct((T, D), F32),
            jax.ShapeDtypeStruct((T * SUBLANE, LANE), U32),
            jax.ShapeDtypeStruct((T, LANE), I32),
            jax.ShapeDtypeStruct((T, LANE), F32),
        ],
        compiler_params=_params(("parallel",)),
    )(x2, og, oc, osel, ow, nar, ng, wo, fg, rwh, rwl, rb)


def _moe_kernel(te_ref, tv_ref, tok_hbm, dst_hbm, hn_hbm, wgt_ref, w1g_ref, w1l_ref, b1g_ref, b1l_ref,
                w2_ref, b2_ref, out_hbm, tok_smem, dst_smem, xbuf, xb_ref, act_ref, ybuf,
                sem_tok, sem_dst, sem_in, sem_out, *, tm, nf, n_tiles):
    i = pl.program_id(0)
    f = pl.program_id(1)
    slot = i % 2
    valid = tv_ref[i]
    sub = MOE_SUB
    nsub = tm // sub
    half = SUBLANE * LANE
    sl8 = SUBLANE

    def tile_of(r):
        return pl.ds(pl.multiple_of(r * sl8, sl8), sl8)

    def tok_copy(t, sl):
        return pltpu.make_async_copy(tok_hbm.at[t], tok_smem.at[sl], sem_tok.at[sl])

    def dst_copy():
        return pltpu.make_async_copy(dst_hbm.at[i], dst_smem, sem_dst)

    def issue_gather(t, sl):
        for s in range(nsub):
            @pl.when(s * sub < tv_ref[t])
            def _():
                def body(r, c):
                    pltpu.make_async_copy(hn_hbm.at[tile_of(tok_smem[sl, r]), :],
                                          xbuf.at[sl, tile_of(r), :], sem_in.at[sl, s]).start()
                    return c

                lax.fori_loop(s * sub, (s + 1) * sub, body, 0, unroll=DMA_UNROLL)

    def wait_gather(sl, s):
        pltpu.make_async_copy(hn_hbm.at[pl.ds(0, sub * sl8), :], xbuf.at[sl, pl.ds(s * sub * sl8, sub * sl8), :],
                              sem_in.at[sl, s]).wait()

    def wait_scatter(count):
        for s in range(nsub):
            @pl.when(s * sub < count)
            def _():
                pltpu.make_async_copy(ybuf.at[pl.ds(s * sub * sl8, sub * sl8), :],
                                      out_hbm.at[pl.ds(0, sub * sl8), :], sem_out).wait()

    @pl.when(f == 0)
    def _():
        @pl.when(i == 0)
        def _():
            cp = tok_copy(0, 0)
            cp.start()
            cp.wait()
            issue_gather(0, 0)

        @pl.when(i + 1 < n_tiles)
        def _():
            tok_copy(i + 1, 1 - slot).start()

        dst_copy().start()
        for s in range(nsub):
            @pl.when(s * sub < valid)
            def _():
                wait_gather(slot, s)
                for j, w in enumerate(_load_rows_tiled(xbuf, s * sub, sub, lead=(slot,))):
                    lo, hi = _unpack_pairs(w)
                    xb_ref[s * sub:(s + 1) * sub, j * LANE:(j + 1) * LANE] = lo.astype(BF16)
                    xb_ref[s * sub:(s + 1) * sub, half + j * LANE:half + (j + 1) * LANE] = hi.astype(BF16)

        @pl.when(i + 1 < n_tiles)
        def _():
            tok_copy(i + 1, 1 - slot).wait()
            issue_gather(i + 1, 1 - slot)

    for s in range(nsub):
        @pl.when(s * sub < valid)
        def _():
            rows = slice(s * sub, (s + 1) * sub)
            xb = xb_ref[rows, :]
            glu = jnp.minimum(_dot(xb, w1g_ref[0]) + b1g_ref[0], SWIGLU_LIMIT)
            lin = jnp.clip(_dot(xb, w1l_ref[0]) + b1l_ref[0], -SWIGLU_LIMIT, SWIGLU_LIMIT)
            act_ref[f, rows, :] = (glu * _sigmoid(SWIGLU_ALPHA * glu) * (lin + 1.0)).astype(BF16)

    @pl.when(f == nf - 1)
    def _():
        @pl.when(i > 0)
        def _():
            wait_scatter(tv_ref[jnp.maximum(i - 1, 0)])

        dst_copy().wait()
        for s in range(nsub):
            @pl.when(s * sub < valid)
            def _():
                rows = slice(s * sub, (s + 1) * sub)
                fc = act_ref.shape[2]
                y = _dot(act_ref[0, rows, :], w2_ref[0, :fc, :])
                for c in range(1, nf):
                    y = y + _dot(act_ref[c, rows, :], w2_ref[0, c * fc:(c + 1) * fc, :])
                y = (y + b2_ref[0]) * wgt_ref[0, rows, :]
                _store_rows_tiled(ybuf, s * sub, sub, _pack_pairs(y[:, :half], y[:, half:]))

                def body(r, c):
                    pltpu.make_async_copy(ybuf.at[tile_of(r), :], out_hbm.at[tile_of(dst_smem[r]), :],
                                          sem_out).start()
                    return c

                lax.fori_loop(s * sub, (s + 1) * sub, body, 0, unroll=DMA_UNROLL)

        @pl.when(i == n_tiles - 1)
        def _():
            wait_scatter(valid)
            ybuf[...] = jnp.zeros_like(ybuf)
            dump = pltpu.make_async_copy(ybuf, out_hbm.at[pl.ds(out_hbm.shape[0] - tm * sl8, tm * sl8), :], sem_out)
            dump.start()
            dump.wait()


def _moe(tile_exp, tile_valid, tok, dst, hn, wgt, w1, b1, w2, b2, n_out_rows, tm, fc):
    n_tiles = tok.shape[0]
    D = w1.shape[1]
    assert D == 2 * SUBLANE * LANE
    dff = w2.shape[1]
    nf = dff // fc
    grid_spec = pltpu.PrefetchScalarGridSpec(
        num_scalar_prefetch=2,
        grid=(n_tiles, nf),
        in_specs=[
            pl.BlockSpec(memory_space=pl.ANY),
            pl.BlockSpec(memory_space=pl.ANY),
            pl.BlockSpec(memory_space=pl.ANY),
            pl.BlockSpec((1, tm, 1), lambda i, f, te, tv: (i, 0, 0)),
            pl.BlockSpec((1, D, fc), lambda i, f, te, tv: (te[i], 0, f)),
            pl.BlockSpec((1, D, fc), lambda i, f, te, tv: (te[i], 0, nf + f)),
            pl.BlockSpec((1, 1, fc), lambda i, f, te, tv: (te[i], 0, f)),
            pl.BlockSpec((1, 1, fc), lambda i, f, te, tv: (te[i], 0, nf + f)),
            pl.BlockSpec((1, dff, D), lambda i, f, te, tv: (te[i], 0, 0)),
            pl.BlockSpec((1, 1, D), lambda i, f, te, tv: (te[i], 0, 0)),
        ],
        out_specs=pl.BlockSpec(memory_space=pl.ANY),
        scratch_shapes=[
            pltpu.SMEM((2, tm), I32),
            pltpu.SMEM((tm,), I32),
            pltpu.VMEM((2, tm * SUBLANE, LANE), U32),
            pltpu.VMEM((tm, D), BF16),
            pltpu.VMEM((nf, tm, fc), BF16),
            pltpu.VMEM((tm * SUBLANE, LANE), U32),
            pltpu.SemaphoreType.DMA((2,)),
            pltpu.SemaphoreType.DMA(()),
            pltpu.SemaphoreType.DMA((2, tm // MOE_SUB)),
            pltpu.SemaphoreType.DMA(()),
        ],
    )
    return pl.pallas_call(
        functools.partial(_moe_kernel, tm=tm, nf=nf, n_tiles=n_tiles),
        grid_spec=grid_spec,
        out_shape=jax.ShapeDtypeStruct((n_out_rows * SUBLANE, LANE), U32),
        compiler_params=_params(("arbitrary", "arbitrary")),
    )(tile_exp, tile_valid, tok, dst, hn, wgt, w1, w1, b1, b1, w2, b2)


def _combine_kernel(h1_ref, y0_ref, y1_ref, y2_ref, y3_ref, g_ref, o_ref):
    h = h1_ref[...]
    tm = h.shape[0]
    for y_ref in (y0_ref, y1_ref, y2_ref, y3_ref):
        pairs = [_unpack_pairs(w) for w in _load_rows_tiled(y_ref, 0, tm)]
        h = h + jnp.concatenate([p[0] for p in pairs] + [p[1] for p in pairs], axis=1)
    o_ref[...] = _rms(h, g_ref[...])


def _combine(ys, h1, g, tm):
    T, D = h1.shape
    nb = T // tm
    slab = lambda k: pl.BlockSpec((tm * SUBLANE, LANE), lambda i, k=k: (k * nb + i, 0))
    return pl.pallas_call(
        _combine_kernel,
        grid=(nb,),
        in_specs=[pl.BlockSpec((tm, D), lambda i: (i, 0))] + [slab(k) for k in range(TOP_K)]
        + [pl.BlockSpec((1, D), lambda i: (0, 0))],
        out_specs=pl.BlockSpec((tm, D), lambda i: (i, 0)),
        out_shape=jax.ShapeDtypeStruct((T, D), F32),
        compiler_params=_params(("parallel",)),
    )(h1, ys, ys, ys, ys, g)


def _t5_bucket(dist):
    n = jnp.maximum(dist, 0)
    exact = N_BUCKETS // 2
    scaled = jnp.log(jnp.maximum(n, 1).astype(F32) / exact) / math.log(MAX_DISTANCE / exact)
    large = exact + (scaled * (N_BUCKETS - exact)).astype(I32)
    return jnp.where(n < exact, n, jnp.minimum(large, N_BUCKETS - 1)).astype(I32)


def _bias_tables(rel_bias, S):
    shifted = ((rel_bias - rel_bias[N_BUCKETS - 1][None, :]) * LOG2E).astype(F32)

    def lookup(dist):
        bucket = _t5_bucket(dist)
        out = jnp.zeros((NSA_HEADS,) + dist.shape, F32)
        for kb in range(N_BUCKETS - 1):
            out = jnp.where((bucket == kb)[None], shifted[kb].reshape((NSA_HEADS,) + (1,) * dist.ndim), out)
        return jnp.where((dist >= 0)[None], out, NEG)

    t = SEL_T
    r = jnp.arange(t)[:, None]
    c = jnp.arange(t)[None, :]
    dtiles = jnp.stack([lookup(r - c), lookup(t + r - c)], axis=1)
    w4 = jnp.where(r < c, 0.0, NEG).astype(F32)
    ncp = S // CMP_STRIDE
    tq = ATT_T
    nq = S // tq
    shift = tq // CMP_STRIDE
    lead = shift * (nq - 1)
    cols = jnp.arange(ncp + lead)[None, :] - lead
    pattern = lookup(jnp.arange(tq)[:, None] - CMP_STRIDE * cols - (CMP_LEN - 1))
    bias_c = jnp.concatenate([pattern[:, :, lead - shift * i:lead - shift * i + ncp] for i in range(nq)], axis=1)
    return dtiles, w4, bias_c


def _overlap_t(S):
    ncp = S // CMP_STRIDE
    ns = S // SEL_BLOCK
    cs = jnp.arange(ncp)[None, :] * CMP_STRIDE
    ss = jnp.arange(ns)[:, None] * SEL_BLOCK
    ov = (cs < ss + SEL_BLOCK) & (cs + CMP_LEN > ss) & (jnp.arange(ncp)[None, :] < ncp - 1)
    return ov.astype(BF16)


def _route(top_idx, top_w, n_exp, tm):
    T = top_idx.shape[0]
    A = T * TOP_K
    e_flat = top_idx.reshape(A)
    w_flat = top_w.reshape(A)
    _, order = lax.sort_key_val(e_flat, jnp.arange(A, dtype=I32), is_stable=True)
    counts = jnp.sum((e_flat[:, None] == jnp.arange(n_exp, dtype=I32)[None, :]).astype(I32), axis=0)
    padded = (counts + tm - 1) // tm * tm
    start = jnp.cumsum(counts) - counts
    pend = jnp.cumsum(padded)
    pstart = pend - padded
    n_tiles = A // tm + n_exp
    tile_row0 = jnp.arange(n_tiles, dtype=I32) * tm
    tile_exp = jnp.minimum(jnp.sum((pend[None, :] <= tile_row0[:, None]).astype(I32), axis=1), n_exp - 1)
    tile_valid = jnp.clip(pstart[tile_exp] + counts[tile_exp] - tile_row0, 0, tm).astype(I32)
    off = tile_row0[:, None] - pstart[tile_exp][:, None] + jnp.arange(tm, dtype=I32)[None, :]
    live = jnp.arange(tm, dtype=I32)[None, :] < tile_valid[:, None]
    src = order[jnp.clip(start[tile_exp][:, None] + off, 0, A - 1)]
    buf_tok = jnp.where(live, src // TOP_K, 0)
    buf_w = jnp.where(live, w_flat[src], 0.0)
    buf_dst = jnp.where(live, (src % TOP_K) * T + src // TOP_K, TOP_K * T + jnp.arange(tm, dtype=I32)[None, :])
    return buf_tok, buf_dst, buf_w[:, :, None], tile_exp, tile_valid


def kernel(x, norm_mix_g, w_in, gla_wa2, gla_ba, gla_norm_g, cmp_pe_k, cmp_wk1, cmp_wk2, cmp_pe_v,
           cmp_wv1, cmp_wv2, nsa_norm_g, rel_bias, w_out, norm_ffn_g, router_w, router_b, exp_w1,
           exp_b1, exp_w2, exp_b2, norm_final_g):
    B, S, D = x.shape
    T = B * S
    G = NSA_KV
    n_exp = router_w.shape[-1]
    assert w_in.shape[0] == 1 and S % 1024 == 0 and S // SEL_BLOCK <= LANE and n_exp <= LANE
    x2 = x.reshape(T, D)

    w = w_in[0]
    a0 = COL_NQ
    n0 = a0 + GLA_RANK
    g0 = n0 + (N_MAIN - COL_NQ)
    w_main = jnp.concatenate([
        w[:, :COL_GK] * (GLA_DK ** -0.5), w[:, COL_GK:a0],
        w[:, n0:n0 + NSA_WIDTH] * (NSA_DH ** -0.5 * LOG2E), w[:, n0 + NSA_WIDTH:g0]], axis=1).astype(BF16)
    w_nar = jnp.concatenate([w[:, a0:n0], w[:, g0:], jnp.zeros((D, LANE - GLA_RANK - 3 * NSA_HEADS), F32)],
                            axis=1).astype(BF16)
    wa2p = jnp.concatenate([gla_wa2[0], jnp.zeros((LANE - GLA_RANK, GLA_HEADS * GLA_DK), F32)], axis=0).astype(BF16)

    tm1 = min(1024, T)
    proj, log_a, nar = _inproj(x2, norm_mix_g, w_main, w_nar, wa2p, gla_ba, tm1, N_MAIN // 4)

    e_w1, e_w2 = exp_w1[0], exp_w2[0]
    o_gla = _gla(proj, log_a, gla_norm_g, B, S, min(256, S))

    ncp = S // CMP_STRIDE
    half = CMP_STRIDE * NSA_DH

    def stride_groups(col):
        t4 = proj[:, col:col + KV_W].reshape(B, ncp, CMP_STRIDE, G, NSA_DH)
        return t4.transpose(0, 3, 1, 2, 4).reshape(B * G, ncp, half)

    hb = jnp.stack([stride_groups(COL_KC), stride_groups(COL_VC)], axis=0)
    pe = jnp.stack([cmp_pe_k[0].reshape(2, half), cmp_pe_v[0].reshape(2, half)], axis=0)
    w1c = jnp.stack([cmp_wk1[0], cmp_wv1[0]], axis=0).astype(BF16)
    w2c = jnp.stack([cmp_wk2[0], cmp_wv2[0]], axis=0).astype(BF16)
    cmp_kv = _compress(hb, pe, w1c, w2c)

    dtiles, w4, bias_c = _bias_tables(rel_bias, S)
    o_cmp, sel = _cmpattn(proj, cmp_kv, bias_c, _overlap_t(S), B, S)
    o_sel, o_win, w1b, w2b = _selwin(proj, sel, dtiles, w4, e_w1.reshape(-1, e_w1.shape[-1]),
                                     e_w2.reshape(-1, e_w2.shape[-1]), B, S)
    w1b, w2b = w1b.reshape(e_w1.shape), w2b.reshape(e_w2.shape)

    rw = jnp.concatenate([router_w[0], jnp.zeros((D, LANE - n_exp), F32)], axis=1)
    rwh = rw.astype(BF16)
    rwl = (rw - rwh.astype(F32)).astype(BF16)
    rb = jnp.concatenate([router_b[0], jnp.full((LANE - n_exp,), NEG, F32)])[None, :]
    h1, hn, top_i, top_w = _outproj(x2, o_gla, o_cmp, o_sel, o_win, nar, nsa_norm_g, w_out[0].astype(BF16),
                                    norm_ffn_g, rwh, rwl, rb, min(512, T))

    tm6 = min(1024, T)
    tok, dst, wgt, tile_exp, tile_valid = _route(top_i[:, :TOP_K], top_w[:, :TOP_K], n_exp, tm6)
    ys = _moe(tile_exp, tile_valid, tok, dst, hn, wgt, w1b, exp_b1[0][:, None, :],
              w2b, exp_b2[0][:, None, :], TOP_K * T + tm6, tm6, 512)
    out = _combine(ys, h1, norm_final_g[None, :], min(512, T))
    return out.reshape(B, S, D)
```

```python
import functools
import math

import numpy as np
import jax
import jax.numpy as jnp
from jax import lax
from jax.experimental import pallas as pl
from jax.experimental.pallas import tpu as pltpu

F32 = jnp.float32
BF16 = jnp.bfloat16
I32 = jnp.int32
U32 = jnp.uint32

GLA_HEADS = 4
GLA_DK = 128
GLA_DV = 256
GLA_RANK = 16
GLA_TAU = 16.0
NSA_HEADS = 8
NSA_KV = 2
NSA_REP = NSA_HEADS // NSA_KV
NSA_DH = 128
CMP_LEN = 32
CMP_STRIDE = 16
SEL_BLOCK = 64
SEL_TOPN = 16
WINDOW = 512
N_BUCKETS = 32
MAX_DISTANCE = 128
TOP_K = 4
SWIGLU_ALPHA = 1.702
SWIGLU_LIMIT = 7.0
RMS_EPS = 1e-5
NEG = -1e30
FORCE_SCORE = 1e4
LOG2E = 1.4426950408889634

GLA_WIDTH = GLA_HEADS * GLA_DV
NSA_WIDTH = NSA_HEADS * NSA_DH
KV_W = NSA_KV * NSA_DH

COL_GQ = 0
COL_GK = COL_GQ + GLA_HEADS * GLA_DK
COL_GV = COL_GK + GLA_HEADS * GLA_DK
COL_GG = COL_GV + GLA_WIDTH
COL_NQ = COL_GG + GLA_WIDTH
COL_KC = COL_NQ + NSA_WIDTH
COL_VC = COL_KC + KV_W
COL_KS = COL_VC + KV_W
COL_VS = COL_KS + KV_W
COL_KW = COL_VS + KV_W
COL_VW = COL_KW + KV_W
N_MAIN = COL_VW + KV_W
LANE = 128
ATT_T = 512
SEL_T = 256
GLA_C = 64
GLA_SUB = 16
MOE_SUB = 512
DMA_UNROLL = 32
GATE_COL0 = GLA_RANK

VMEM_LIMIT = 56 * 1024 * 1024

NT = (((1,), (1,)), ((), ()))
TN = (((0,), (0,)), ((), ()))


def _dot(a, b):
    return jnp.dot(a, b, preferred_element_type=F32)


def _dg(a, b, dims):
    return lax.dot_general(a, b, dims, preferred_element_type=F32)


def _sigmoid(x):
    return 1.0 / (1.0 + jnp.exp(-x))


def _log_sigmoid(z):
    return -(jnp.maximum(-z, 0.0) + jnp.log(1.0 + jnp.exp(-jnp.abs(z))))


def _rms(x, g):
    return x * lax.rsqrt(jnp.mean(x * x, axis=-1, keepdims=True) + RMS_EPS) * g


def _split3(x):
    hi = x.astype(BF16)
    r1 = x - hi.astype(F32)
    mid = r1.astype(BF16)
    lo = (r1 - mid.astype(F32)).astype(BF16)
    return hi, mid, lo


def _pack_pairs(lo, hi):
    lo_bits = lax.bitcast_convert_type(lo.astype(BF16).astype(F32), U32) >> 16
    hi_bits = lax.bitcast_convert_type(hi.astype(BF16).astype(F32), U32) & jnp.uint32(0xFFFF0000)
    return lo_bits | hi_bits


def _unpack_pairs(w):
    lo = lax.bitcast_convert_type(w << 16, F32)
    hi = lax.bitcast_convert_type(w & jnp.uint32(0xFFFF0000), F32)
    return lo, hi


SUBLANE = 8


def _store_rows_tiled(ref, base, n, packed):
    for j in range(SUBLANE):
        ref[pl.ds(base * SUBLANE + j, n, stride=SUBLANE), :] = packed[:, j * LANE:(j + 1) * LANE]


def _load_rows_tiled(ref, base, n, lead=()):
    return [ref[lead + (pl.ds(base * SUBLANE + j, n, stride=SUBLANE), slice(None))] for j in range(SUBLANE)]


def _params(sem, vmem=VMEM_LIMIT):
    return pltpu.CompilerParams(dimension_semantics=sem, vmem_limit_bytes=vmem)


def _inproj_kernel(x_ref, g_ref, w_ref, wn_ref, wa2_ref, ba_ref, proj_ref, la_ref, nar_ref, xn_ref):
    @pl.when(pl.program_id(1) == 0)
    def _():
        xn = _rms(x_ref[...], g_ref[...]).astype(BF16)
        xn_ref[...] = xn
        nar = _dot(xn, wn_ref[...])
        nar_ref[...] = nar
        z = _dot(nar.astype(BF16), wa2_ref[...]) + ba_ref[...]
        la_ref[...] = _log_sigmoid(z) * (1.0 / GLA_TAU)

    proj_ref[...] = _dot(xn_ref[...], w_ref[...]).astype(BF16)


def _inproj(x2, g, w_main, w_nar, wa2p, ba, tm, tn):
    T, D = x2.shape
    nj = N_MAIN // tn
    return pl.pallas_call(
        _inproj_kernel,
        grid=(T // tm, nj),
        in_specs=[
            pl.BlockSpec((tm, D), lambda i, j: (i, 0)),
            pl.BlockSpec((1, D), lambda i, j: (0, 0)),
            pl.BlockSpec((D, tn), lambda i, j: (0, j)),
            pl.BlockSpec((D, LANE), lambda i, j: (0, 0)),
            pl.BlockSpec((LANE, GLA_HEADS * GLA_DK), lambda i, j: (0, 0)),
            pl.BlockSpec((1, GLA_HEADS * GLA_DK), lambda i, j: (0, 0)),
        ],
        out_specs=[
            pl.BlockSpec((tm, tn), lambda i, j: (i, j)),
            pl.BlockSpec((tm, GLA_HEADS * GLA_DK), lambda i, j: (i, 0)),
            pl.BlockSpec((tm, LANE), lambda i, j: (i, 0)),
        ],
        out_shape=[
            jax.ShapeDtypeStruct((T, N_MAIN), BF16),
            jax.ShapeDtypeStruct((T, GLA_HEADS * GLA_DK), F32),
            jax.ShapeDtypeStruct((T, LANE), F32),
        ],
        scratch_shapes=[pltpu.VMEM((tm, D), BF16)],
        compiler_params=_params(("parallel", "arbitrary")),
    )(x2, g, w_main, w_nar, wa2p, ba)


def _gla_kernel(q_ref, k_ref, v_ref, g_ref, la_ref, ng_ref, o_ref, st_ref, *, cb):
    @pl.when(pl.program_id(1) == 0)
    def _():
        st_ref[...] = jnp.zeros_like(st_ref)

    C, SUB = GLA_C, GLA_SUB
    heads = range(GLA_HEADS)
    dk = lambda h: slice(h * GLA_DK, (h + 1) * GLA_DK)
    dv = lambda h: slice(h * GLA_DV, (h + 1) * GLA_DV)
    ri = lax.broadcasted_iota(I32, (C, C), 0)
    ci = lax.broadcasted_iota(I32, (C, C), 1)
    tri = jnp.where(ri >= ci, 1.0, 0.0).astype(BF16)
    nsub = C // SUB
    sub_of_row = lax.broadcasted_iota(I32, (C, GLA_DK), 0) // SUB
    for n in range(cb // C):
        rows = slice(n * C, (n + 1) * C)
        b = []
        for h in heads:
            hi, mid, lo = _split3(la_ref[rows, dk(h)])
            b.append(_dot(tri, hi) + _dot(tri, mid) + _dot(tri, lo))
        q = [q_ref[rows, dk(h)].astype(F32) for h in heads]
        k = [k_ref[rows, dk(h)].astype(F32) for h in heads]
        v = [v_ref[rows, dv(h)] for h in heads]
        st = [st_ref[h] for h in heads]
        o = [_dg((q[h] * jnp.exp(b[h])).astype(BF16), st[h].astype(BF16), NT) for h in heads]
        intra = []
        for h in heads:
            refs = [jnp.zeros((1, GLA_DK), F32)] + [b[h][s * SUB - 1:s * SUB, :] for s in range(1, nsub)]
            ref_row = jnp.concatenate([jnp.broadcast_to(r, (SUB, GLA_DK)) for r in refs], axis=0)
            qsc = q[h] * jnp.exp(b[h] - ref_row)
            q_cat = jnp.concatenate([jnp.where(sub_of_row == s, qsc, 0.0) for s in range(nsub)], axis=1)
            k_cat = jnp.concatenate([jnp.where(sub_of_row <= s, k[h] * jnp.exp(refs[s] - b[h]), 0.0)
                                     for s in range(nsub)], axis=1)
            sc = jnp.where(ci <= ri, _dg(q_cat.astype(BF16), k_cat.astype(BF16), NT), 0.0)
            intra.append(_dot(sc.astype(BF16), v[h]))
        for h in heads:
            bl = b[h][C - 1:C, :]
            kdec = (k[h] * jnp.exp(bl - b[h])).astype(BF16)
            st_ref[h] = st[h] * jnp.exp(bl) + _dg(v[h], kdec, TN)
        for h in heads:
            oh = o[h] + intra[h]
            gg = g_ref[rows, dv(h)].astype(F32)
            o_ref[rows, dv(h)] = (_rms(oh, ng_ref[...]) * (gg * _sigmoid(gg))).astype(BF16)


def _cast_slab(w2d, nsteps):
    rows, width = w2d.shape
    assert rows % nsteps == 0 and (rows // nsteps) % 16 == 0
    return rows // nsteps, width


def _gla(proj, la, ng, B, S, cb):
    T = B * S
    ncb = S // cb
    dkw = GLA_HEADS * GLA_DK
    row = lambda b, c: b * ncb + c
    return pl.pallas_call(
        functools.partial(_gla_kernel, cb=cb),
        grid=(B, ncb),
        in_specs=[
            pl.BlockSpec((cb, dkw), lambda b, c: (row(b, c), COL_GQ // dkw)),
            pl.BlockSpec((cb, dkw), lambda b, c: (row(b, c), COL_GK // dkw)),
            pl.BlockSpec((cb, GLA_WIDTH), lambda b, c: (row(b, c), COL_GV // GLA_WIDTH)),
            pl.BlockSpec((cb, GLA_WIDTH), lambda b, c: (row(b, c), COL_GG // GLA_WIDTH)),
            pl.BlockSpec((cb, dkw), lambda b, c: (row(b, c), 0)),
            pl.BlockSpec((1, GLA_DV), lambda b, c: (0, 0)),
        ],
        out_specs=pl.BlockSpec((cb, GLA_WIDTH), lambda b, c: (row(b, c), 0)),
        out_shape=jax.ShapeDtypeStruct((T, GLA_WIDTH), BF16),
        scratch_shapes=[pltpu.VMEM((GLA_HEADS, GLA_DV, GLA_DK), F32)],
        compiler_params=_params(("parallel", "arbitrary")),
    )(proj, proj, proj, proj, la, ng)


def _compress_kernel(hb_ref, pe_ref, w1_ref, w2_ref, o_ref, *, ncp):
    half = CMP_STRIDE * NSA_DH
    hb = hb_ref[0, 0].astype(F32)
    pe = pe_ref[0]
    top = (hb + pe[0:1]).astype(BF16)
    bot = (hb + pe[1:2]).astype(BF16)
    a = _dot(top, w1_ref[0, :half, :])
    bm = _dot(bot, w1_ref[0, half:, :])
    pre = a + pltpu.roll(bm, ncp - 1, 0)
    cdf = 0.5 * (1.0 + jnp.tanh(math.sqrt(2.0 / math.pi) * (pre + 0.044715 * (pre * pre * pre))))
    o_ref[0, 0] = _dot((pre * cdf).astype(BF16), w2_ref[0]).astype(BF16)


def _compress(hb, pe, w1, w2):
    _, BG, ncp, half = hb.shape
    return pl.pallas_call(
        functools.partial(_compress_kernel, ncp=ncp),
        grid=(2, BG),
        in_specs=[
            pl.BlockSpec((1, 1, ncp, half), lambda t, n: (t, n, 0, 0)),
            pl.BlockSpec((1, 2, half), lambda t, n: (t, 0, 0)),
            pl.BlockSpec((1, 2 * half, NSA_DH), lambda t, n: (t, 0, 0)),
            pl.BlockSpec((1, NSA_DH, NSA_DH), lambda t, n: (t, 0, 0)),
        ],
        out_specs=pl.BlockSpec((1, 1, ncp, NSA_DH), lambda t, n: (t, n, 0, 0)),
        out_shape=jax.ShapeDtypeStruct((2, BG, ncp, NSA_DH), BF16),
        compiler_params=_params(("parallel", "parallel")),
    )(hb, pe, w1, w2)


def _cmpattn_kernel(q_ref, kc_ref, vc_ref, bias_ref, ovt_ref, o_ref, sel_ref, *, ns, ncp):
    tq = ATT_T
    i = pl.program_id(1)
    kc = kc_ref[0, 0]
    vc = vc_ref[0, 0]
    pos = i * tq + lax.broadcasted_iota(I32, (tq, 1), 0)
    row_valid = jnp.where(pos >= CMP_LEN - 1, 1.0, 0.0)
    psum = jnp.zeros((tq, ncp), F32)
    outs = []
    for r in range(NSA_REP):
        q = q_ref[:, r * NSA_DH:(r + 1) * NSA_DH]
        s = _dg(q, kc, NT) + bias_ref[r]
        e = jnp.exp2(s - jnp.max(s, axis=-1, keepdims=True))
        p = e / jnp.sum(e, axis=-1, keepdims=True) * row_valid
        psum = psum + p
        outs.append(_dot(p.astype(BF16), vc))
    o_ref[...] = jnp.concatenate(outs, axis=1).astype(BF16)

    ph = psum.astype(BF16)
    pl_ = (psum - ph.astype(F32)).astype(BF16)
    imp = _dg(ovt_ref[...], ph, NT) + _dg(ovt_ref[...], pl_, NT)
    n_i = lax.broadcasted_iota(I32, (ns, tq), 0)
    cur = (i * tq + lax.broadcasted_iota(I32, (ns, tq), 1)) // SEL_BLOCK
    forced = (n_i == 0) | (n_i == cur) | (n_i == cur - 1)
    score = jnp.where(forced, FORCE_SCORE, jnp.where(n_i <= cur, imp, -1.0))
    cnt = jnp.zeros((ns, tq), F32)
    for m in range(ns):
        sm = score[m:m + 1, :]
        ge = jnp.where(sm >= score, 1.0, 0.0)
        gt = jnp.where(sm > score, 1.0, 0.0)
        cnt = cnt + jnp.where(n_i > m, ge, gt)
    sel_t = jnp.where(cnt < min(SEL_TOPN, ns), 1.0, 0.0)
    if ns < LANE:
        sel_t = jnp.concatenate([sel_t, jnp.zeros((LANE - ns, tq), F32)], axis=0)
    sel_ref[0] = sel_t.T.astype(BF16)


def _cmpattn(proj, cmp_kv, bias_c, ovt, B, S):
    T = B * S
    tq = ATT_T
    nq = S // tq
    ns = S // SEL_BLOCK
    ncp = S // CMP_STRIDE
    G = NSA_KV
    qcol = COL_NQ // (NSA_REP * NSA_DH)
    return pl.pallas_call(
        functools.partial(_cmpattn_kernel, ns=ns, ncp=ncp),
        grid=(G, nq, B),
        in_specs=[
            pl.BlockSpec((tq, NSA_REP * NSA_DH), lambda g, i, b: (b * nq + i, qcol + g)),
            pl.BlockSpec((1, 1, ncp, NSA_DH), lambda g, i, b: (0, b * G + g, 0, 0)),
            pl.BlockSpec((1, 1, ncp, NSA_DH), lambda g, i, b: (1, b * G + g, 0, 0)),
            pl.BlockSpec((NSA_REP, tq, ncp), lambda g, i, b: (g, i, 0)),
            pl.BlockSpec((ns, ncp), lambda g, i, b: (0, 0)),
        ],
        out_specs=[
            pl.BlockSpec((tq, NSA_REP * NSA_DH), lambda g, i, b: (b * nq + i, g)),
            pl.BlockSpec((1, tq, LANE), lambda g, i, b: (b * G + g, i, 0)),
        ],
        out_shape=[
            jax.ShapeDtypeStruct((T, NSA_WIDTH), BF16),
            jax.ShapeDtypeStruct((B * G, S, LANE), BF16),
        ],
        compiler_params=_params(("parallel", "parallel", "parallel")),
    )(proj, cmp_kv, cmp_kv, bias_c, ovt)


def _flash_update(s, v, m_ref, al_ref):
    reps = s.shape[1] // LANE
    m_old = m_ref[...]
    part = s[:, :LANE]
    for c in range(1, reps):
        part = jnp.maximum(part, s[:, c * LANE:(c + 1) * LANE])
    m_new = jnp.maximum(m_old, jnp.max(part, axis=-1, keepdims=True))
    alpha = jnp.exp2(m_old - m_new)
    p = jnp.exp2(s - jnp.concatenate([m_new] * reps, axis=1))
    v_ext = jnp.concatenate([v, jnp.ones(v.shape, BF16)], axis=1)
    al_ref[...] = jnp.concatenate([alpha, alpha], axis=1) * al_ref[...] + _dot(p.astype(BF16), v_ext)
    m_ref[...] = m_new


def _selwin_kernel(q_ref, ks_ref, vs_ref, kw_ref, vw_ref, sel_ref, d_ref, w4_ref, wf1_ref, wf2_ref,
                   osel_ref, owin_ref, wb1_ref, wb2_ref, m_ref, al_ref, s_ref):
    wb1_ref[...] = wf1_ref[...].astype(BF16)
    wb2_ref[...] = wf2_ref[...].astype(BF16)
    t = SEL_T
    R = NSA_REP
    i = pl.program_id(2)
    q = jnp.concatenate([q_ref[:, r * NSA_DH:(r + 1) * NSA_DH] for r in range(R)], axis=0)
    unsel = (1.0 - sel_ref[0].astype(F32)).astype(BF16)
    q_sel = jnp.concatenate([q, jnp.concatenate([unsel] * R, axis=0)], axis=1)
    c_blk = lax.broadcasted_iota(I32, (t, LANE), 0) // SEL_BLOCK
    n_i = lax.broadcasted_iota(I32, (t, LANE), 1)
    blk_per_tile = t // SEL_BLOCK

    def init():
        m_ref[...] = jnp.full(m_ref.shape, NEG, F32)
        al_ref[...] = jnp.zeros(al_ref.shape, F32)

    def tile(ref, j):
        return ref[pl.ds(pl.multiple_of(j * t, t), t), :]

    def logits(k_ref, j, add):
        s = _dg(q, tile(k_ref, j), NT).reshape(R, t, t)
        return (s + add).reshape(R * t, t)

    def sel_logits(j):
        own_block = jnp.where(n_i == blk_per_tile * j + c_blk, NEG, 0.0).astype(BF16)
        return _dg(q_sel, jnp.concatenate([tile(ks_ref, j), own_block], axis=1), NT)

    def write(o_ref, al):
        o = al[:, :NSA_DH] / al[:, NSA_DH:]
        o_ref[...] = jnp.concatenate([o[r * t:(r + 1) * t] for r in range(R)], axis=1).astype(BF16)

    init()

    jn = jnp.maximum(i - 1, 0)
    s_ref[...] = sel_logits(0)

    def far(j, c):
        s = s_ref[...]
        s_ref[...] = sel_logits(j + 1)
        _flash_update(s, tile(vs_ref, j), m_ref, al_ref)
        return c

    def with_bias(s, add):
        return (s.reshape(R, t, t) + add).reshape(R * t, t)

    lax.fori_loop(0, jn, far, 0)
    off = jnp.where(i == 0, NEG, 0.0)
    s_near = s_ref[...]
    s_diag = sel_logits(i)
    _flash_update(with_bias(s_near, d_ref[:, 1] + off), tile(vs_ref, jn), m_ref, al_ref)
    _flash_update(with_bias(s_diag, d_ref[:, 0]), tile(vs_ref, i), m_ref, al_ref)
    write(osel_ref, al_ref[...])

    init()
    pre = WINDOW // t
    for o in range(pre, -1, -1):
        jw = i - o
        jc = jnp.maximum(jw, 0)
        off = jnp.where(jw < 0, NEG, 0.0)
        if o == 0:
            add = d_ref[:, 0] + off
        elif o == 1:
            add = d_ref[:, 1] + off
        elif o == pre:
            add = (w4_ref[...] + off)[None]
        else:
            add = off
        _flash_update(logits(kw_ref, jc, add), tile(vw_ref, jc), m_ref, al_ref)
    write(owin_ref, al_ref[...])


def _selwin(proj, sel, dtiles, w4, wf1, wf2, B, S):
    T = B * S
    t = SEL_T
    nq = S // t
    G = NSA_KV
    qcol = COL_NQ // (NSA_REP * NSA_DH)
    kv = lambda col: pl.BlockSpec((S, NSA_DH), lambda b, g, i, col=col: (b, col // NSA_DH + g))
    slab = lambda wf: pl.BlockSpec(_cast_slab(wf, B * G * nq), lambda b, g, i: ((b * G + g) * nq + i, 0))
    return pl.pallas_call(
        _selwin_kernel,
        grid=(B, G, nq),
        in_specs=[
            pl.BlockSpec((t, NSA_REP * NSA_DH), lambda b, g, i: (b * nq + i, qcol + g)),
            kv(COL_KS), kv(COL_VS), kv(COL_KW), kv(COL_VW),
            pl.BlockSpec((1, t, LANE), lambda b, g, i: (b * G + g, i, 0)),
            pl.BlockSpec((NSA_REP, 2, t, t), lambda b, g, i: (g, 0, 0, 0)),
            pl.BlockSpec((t, t), lambda b, g, i: (0, 0)),
            slab(wf1), slab(wf2),
        ],
        out_specs=[
            pl.BlockSpec((t, NSA_REP * NSA_DH), lambda b, g, i: (b * nq + i, g)),
            pl.BlockSpec((t, NSA_REP * NSA_DH), lambda b, g, i: (b * nq + i, g)),
            slab(wf1), slab(wf2),
        ],
        out_shape=[
            jax.ShapeDtypeStruct((T, NSA_WIDTH), BF16),
            jax.ShapeDtypeStruct((T, NSA_WIDTH), BF16),
            jax.ShapeDtypeStruct(wf1.shape, BF16),
            jax.ShapeDtypeStruct(wf2.shape, BF16),
        ],
        scratch_shapes=[
            pltpu.VMEM((NSA_REP * t, LANE), F32),
            pltpu.VMEM((NSA_REP * t, 2 * NSA_DH), F32),
            pltpu.VMEM((NSA_REP * t, t), F32),
        ],
        compiler_params=_params(("parallel", "parallel", "parallel")),
    )(proj, proj, proj, proj, proj, sel, dtiles, w4, wf1, wf2)


def _outproj_kernel(x_ref, og_ref, oc_ref, os_ref, ow_ref, nar_ref, ng_ref, wo_ref, fg_ref,
                    rwh_ref, rwl_ref, rb_ref, h1_ref, hn_ref, ti_ref, tw_ref):
    gates = _sigmoid(nar_ref[...])
    parts = []
    for h in range(NSA_HEADS):
        cs = slice(h * NSA_DH, (h + 1) * NSA_DH)
        g0 = gates[:, GATE_COL0 + h:GATE_COL0 + h + 1]
        g1 = gates[:, GATE_COL0 + NSA_HEADS + h:GATE_COL0 + NSA_HEADS + h + 1]
        g2 = gates[:, GATE_COL0 + 2 * NSA_HEADS + h:GATE_COL0 + 2 * NSA_HEADS + h + 1]
        parts.append(g0 * oc_ref[:, cs].astype(F32) + g1 * os_ref[:, cs].astype(F32)
                     + g2 * ow_ref[:, cs].astype(F32))
    on = _rms(jnp.concatenate(parts, axis=1), ng_ref[...]).astype(BF16)
    h1 = x_ref[...] + _dot(og_ref[...], wo_ref[:GLA_WIDTH, :]) + _dot(on, wo_ref[GLA_WIDTH:, :])
    h1_ref[...] = h1
    hn = _rms(h1, fg_ref[...])
    half = hn.shape[1] // 2
    _store_rows_tiled(hn_ref, 0, hn.shape[0], _pack_pairs(hn[:, :half], hn[:, half:]))

    hh = hn.astype(BF16)
    hl = (hn - hh.astype(F32)).astype(BF16)
    both = _dot(hh, jnp.concatenate([rwh_ref[...], rwl_ref[...]], axis=1))
    lg = both[:, :LANE] + both[:, LANE:] + _dot(hl, rwh_ref[...]) + rb_ref[...]
    lane = lax.broadcasted_iota(I32, lg.shape, 1)
    vals, idxs = [], []
    for _ in range(TOP_K):
        mx = jnp.max(lg, axis=-1, keepdims=True)
        ix = jnp.min(jnp.where(lg == mx, lane, LANE), axis=-1, keepdims=True)
        vals.append(mx)
        idxs.append(ix)
        lg = jnp.where(lane == ix, -3e38, lg)
    es = [jnp.exp(v - vals[0]) for v in vals]
    den = es[0] + es[1] + es[2] + es[3]
    ti = jnp.zeros(lg.shape, I32)
    tw = jnp.zeros(lg.shape, F32)
    for kk in range(TOP_K):
        ti = jnp.where(lane == kk, idxs[kk], ti)
        tw = jnp.where(lane == kk, es[kk] / den, tw)
    ti_ref[...] = ti
    tw_ref[...] = tw


def _outproj(x2, og, oc, osel, ow, nar, ng, wo, fg, rwh, rwl, rb, tm):
    T, D = x2.shape
    row = lambda w: pl.BlockSpec((tm, w), lambda i: (i, 0))
    full = lambda a: pl.BlockSpec(a.shape, lambda i: (0,) * a.ndim)
    return pl.pallas_call(
        _outproj_kernel,
        grid=(T // tm,),
        in_specs=[row(D), row(GLA_WIDTH), row(NSA_WIDTH), row(NSA_WIDTH), row(NSA_WIDTH), row(LANE),
                  full(ng), full(wo), full(fg), full(rwh), full(rwl), full(rb)],
        out_specs=[row(D), pl.BlockSpec((tm * SUBLANE, LANE), lambda i: (i, 0)), row(LANE), row(LANE)],
        out_shape=[
            jax.ShapeDtypeStruct((T, D), F32),
            jax.ShapeDtypeStruct((T * SUBLANE, LANE), U32),
            jax.ShapeDtypeStruct((T, LANE), I32),
            jax.ShapeDtypeStruct((T, LANE), F32),
        ],
        compiler_params=_params(("parallel",)),
    )(x2, og, oc, osel, ow, nar, ng, wo, fg, rwh, rwl, rb)


def _moe_kernel(te_ref, tv_ref, tok_hbm, dst_hbm, hn_hbm, wgt_ref, w1g_ref, w1l_ref, b1g_ref, b1l_ref,
                w2_ref, b2_ref, out_hbm, tok_smem, dst_smem, xbuf, xb_ref, act_ref, ybuf,
                sem_tok, sem_dst, sem_in, sem_out, *, tm, nf, n_tiles):
    i = pl.program_id(0)
    f = pl.program_id(1)
    slot = i % 2
    valid = tv_ref[i]
    sub = MOE_SUB
    nsub = tm // sub
    half = SUBLANE * LANE
    sl8 = SUBLANE

    def tile_of(r):
        return pl.ds(pl.multiple_of(r * sl8, sl8), sl8)

    def tok_copy(t, sl):
        return pltpu.make_async_copy(tok_hbm.at[t], tok_smem.at[sl], sem_tok.at[sl])

    def dst_copy():
        return pltpu.make_async_copy(dst_hbm.at[i], dst_smem, sem_dst)

    def issue_gather(t, sl):
        for s in range(nsub):
            @pl.when(s * sub < tv_ref[t])
            def _():
                def body(p, c):
                    for pr in range(2):
                        r = 2 * p + pr
                        pltpu.make_async_copy(hn_hbm.at[tile_of(tok_smem[sl, r]), :],
                                              xbuf.at[sl, tile_of(r), :], sem_in.at[sl, s]).start(priority=pr)
                    return c

                lax.fori_loop(s * sub // 2, (s + 1) * sub // 2, body, 0, unroll=DMA_UNROLL // 2)

    def wait_gather(sl, s):
        pltpu.make_async_copy(hn_hbm.at[pl.ds(0, sub * sl8), :], xbuf.at[sl, pl.ds(s * sub * sl8, sub * sl8), :],
                              sem_in.at[sl, s]).wait()

    def wait_scatter(count):
        for s in range(nsub):
            @pl.when(s * sub < count)
            def _():
                pltpu.make_async_copy(ybuf.at[pl.ds(s * sub * sl8, sub * sl8), :],
                                      out_hbm.at[pl.ds(0, sub * sl8), :], sem_out).wait()

    @pl.when(f == 0)
    def _():
        @pl.when(i == 0)
        def _():
            cp = tok_copy(0, 0)
            cp.start()
            cp.wait()
            issue_gather(0, 0)

        @pl.when(i + 1 < n_tiles)
        def _():
            tok_copy(i + 1, 1 - slot).start()

        dst_copy().start()
        for s in range(nsub):
            @pl.when(s * sub < valid)
            def _():
                wait_gather(slot, s)
                for j, w in enumerate(_load_rows_tiled(xbuf, s * sub, sub, lead=(slot,))):
                    lo, hi = _unpack_pairs(w)
                    xb_ref[s * sub:(s + 1) * sub, j * LANE:(j + 1) * LANE] = lo.astype(BF16)
                    xb_ref[s * sub:(s + 1) * sub, half + j * LANE:half + (j + 1) * LANE] = hi.astype(BF16)

        @pl.when(i + 1 < n_tiles)
        def _():
            tok_copy(i + 1, 1 - slot).wait()
            issue_gather(i + 1, 1 - slot)

    for s in range(nsub):
        @pl.when(s * sub < valid)
        def _():
            rows = slice(s * sub, (s + 1) * sub)
            xb = xb_ref[rows, :]
            glu = jnp.minimum(_dot(xb, w1g_ref[0]) + b1g_ref[0], SWIGLU_LIMIT)
            lin = jnp.clip(_dot(xb, w1l_ref[0]) + b1l_ref[0], -SWIGLU_LIMIT, SWIGLU_LIMIT)
            act_ref[f, rows, :] = (glu * _sigmoid(SWIGLU_ALPHA * glu) * (lin + 1.0)).astype(BF16)

    @pl.when(f == nf - 1)
    def _():
        @pl.when(i > 0)
        def _():
            wait_scatter(tv_ref[jnp.maximum(i - 1, 0)])

        dst_copy().wait()
        for s in range(nsub):
            @pl.when(s * sub < valid)
            def _():
                rows = slice(s * sub, (s + 1) * sub)
                fc = act_ref.shape[2]
                y = _dot(act_ref[0, rows, :], w2_ref[0, :fc, :])
                for c in range(1, nf):
                    y = y + _dot(act_ref[c, rows, :], w2_ref[0, c * fc:(c + 1) * fc, :])
                y = (y + b2_ref[0]) * wgt_ref[0, rows, :]
                _store_rows_tiled(ybuf, s * sub, sub, _pack_pairs(y[:, :half], y[:, half:]))

                def body(p, c):
                    for pr in range(2):
                        r = 2 * p + pr
                        pltpu.make_async_copy(ybuf.at[tile_of(r), :], out_hbm.at[tile_of(dst_smem[r]), :],
                                              sem_out).start(priority=pr)
                    return c

                lax.fori_loop(s * sub // 2, (s + 1) * sub // 2, body, 0, unroll=DMA_UNROLL // 2)

        @pl.when(i == n_tiles - 1)
        def _():
            wait_scatter(valid)
            ybuf[...] = jnp.zeros_like(ybuf)
            dump = pltpu.make_async_copy(ybuf, out_hbm.at[pl.ds(out_hbm.shape[0] - tm * sl8, tm * sl8), :], sem_out)
            dump.start()
            dump.wait()


def _moe(tile_exp, tile_valid, tok, dst, hn, wgt, w1, b1, w2, b2, n_out_rows, tm, fc):
    n_tiles = tok.shape[0]
    D = w1.shape[1]
    assert D == 2 * SUBLANE * LANE
    dff = w2.shape[1]
    nf = dff // fc
    grid_spec = pltpu.PrefetchScalarGridSpec(
        num_scalar_prefetch=2,
        grid=(n_tiles, nf),
        in_specs=[
            pl.BlockSpec(memory_space=pl.ANY),
            pl.BlockSpec(memory_space=pl.ANY),
            pl.BlockSpec(memory_space=pl.ANY),
            pl.BlockSpec((1, tm, 1), lambda i, f, te, tv: (i, 0, 0)),
            pl.BlockSpec((1, D, fc), lambda i, f, te, tv: (te[i], 0, f)),
            pl.BlockSpec((1, D, fc), lambda i, f, te, tv: (te[i], 0, nf + f)),
            pl.BlockSpec((1, 1, fc), lambda i, f, te, tv: (te[i], 0, f)),
            pl.BlockSpec((1, 1, fc), lambda i, f, te, tv: (te[i], 0, nf + f)),
            pl.BlockSpec((1, dff, D), lambda i, f, te, tv: (te[i], 0, 0)),
            pl.BlockSpec((1, 1, D), lambda i, f, te, tv: (te[i], 0, 0)),
        ],
        out_specs=pl.BlockSpec(memory_space=pl.ANY),
        scratch_shapes=[
            pltpu.SMEM((2, tm), I32),
            pltpu.SMEM((tm,), I32),
            pltpu.VMEM((2, tm * SUBLANE, LANE), U32),
            pltpu.VMEM((tm, D), BF16),
            pltpu.VMEM((nf, tm, fc), BF16),
            pltpu.VMEM((tm * SUBLANE, LANE), U32),
            pltpu.SemaphoreType.DMA((2,)),
            pltpu.SemaphoreType.DMA(()),
            pltpu.SemaphoreType.DMA((2, tm // MOE_SUB)),
            pltpu.SemaphoreType.DMA(()),
        ],
    )
    return pl.pallas_call(
        functools.partial(_moe_kernel, tm=tm, nf=nf, n_tiles=n_tiles),
        grid_spec=grid_spec,
        out_shape=jax.ShapeDtypeStruct((n_out_rows * SUBLANE, LANE), U32),
        compiler_params=_params(("arbitrary", "arbitrary")),
    )(tile_exp, tile_valid, tok, dst, hn, wgt, w1, w1, b1, b1, w2, b2)


def _combine_kernel(h1_ref, y0_ref, y1_ref, y2_ref, y3_ref, g_ref, o_ref):
    h = h1_ref[...]
    tm = h.shape[0]
    for y_ref in (y0_ref, y1_ref, y2_ref, y3_ref):
        pairs = [_unpack_pairs(w) for w in _load_rows_tiled(y_ref, 0, tm)]
        h = h + jnp.concatenate([p[0] for p in pairs] + [p[1] for p in pairs], axis=1)
    o_ref[...] = _rms(h, g_ref[...])


def _combine(ys, h1, g, tm):
    T, D = h1.shape
    nb = T // tm
    slab = lambda k: pl.BlockSpec((tm * SUBLANE, LANE), lambda i, k=k: (k * nb + i, 0))
    return pl.pallas_call(
        _combine_kernel,
        grid=(nb,),
        in_specs=[pl.BlockSpec((tm, D), lambda i: (i, 0))] + [slab(k) for k in range(TOP_K)]
        + [pl.BlockSpec((1, D), lambda i: (0, 0))],
        out_specs=pl.BlockSpec((tm, D), lambda i: (i, 0)),
        out_shape=jax.ShapeDtypeStruct((T, D), F32),
        compiler_params=_params(("parallel",)),
    )(h1, ys, ys, ys, ys, g)


def _t5_bucket(dist):
    n = jnp.maximum(dist, 0)
    exact = N_BUCKETS // 2
    scaled = jnp.log(jnp.maximum(n, 1).astype(F32) / exact) / math.log(MAX_DISTANCE / exact)
    large = exact + (scaled * (N_BUCKETS - exact)).astype(I32)
    return jnp.where(n < exact, n, jnp.minimum(large, N_BUCKETS - 1)).astype(I32)


def _bias_tables(rel_bias, S):
    shifted = ((rel_bias - rel_bias[N_BUCKETS - 1][None, :]) * LOG2E).astype(F32)

    def lookup(dist):
        bucket = _t5_bucket(dist)
        out = jnp.zeros((NSA_HEADS,) + dist.shape, F32)
        for kb in range(N_BUCKETS - 1):
            out = jnp.where((bucket == kb)[None], shifted[kb].reshape((NSA_HEADS,) + (1,) * dist.ndim), out)
        return jnp.where((dist >= 0)[None], out, NEG)

    t = SEL_T
    r = jnp.arange(t)[:, None]
    c = jnp.arange(t)[None, :]
    dtiles = jnp.stack([lookup(r - c), lookup(t + r - c)], axis=1)
    w4 = jnp.where(r < c, 0.0, NEG).astype(F32)
    ncp = S // CMP_STRIDE
    tq = ATT_T
    nq = S // tq
    shift = tq // CMP_STRIDE
    lead = shift * (nq - 1)
    cols = jnp.arange(ncp + lead)[None, :] - lead
    pattern = lookup(jnp.arange(tq)[:, None] - CMP_STRIDE * cols - (CMP_LEN - 1))
    bias_c = jnp.concatenate([pattern[:, :, lead - shift * i:lead - shift * i + ncp] for i in range(nq)], axis=1)
    return dtiles, w4, bias_c


def _overlap_t(S):
    ncp = S // CMP_STRIDE
    ns = S // SEL_BLOCK
    cs = jnp.arange(ncp)[None, :] * CMP_STRIDE
    ss = jnp.arange(ns)[:, None] * SEL_BLOCK
    ov = (cs < ss + SEL_BLOCK) & (cs + CMP_LEN > ss) & (jnp.arange(ncp)[None, :] < ncp - 1)
    return ov.astype(BF16)


def _route(top_idx, top_w, n_exp, tm):
    T = top_idx.shape[0]
    A = T * TOP_K
    e_flat = top_idx.reshape(A)
    w_flat = top_w.reshape(A)
    _, order = lax.sort_key_val(e_flat, jnp.arange(A, dtype=I32), is_stable=True)
    counts = jnp.sum((e_flat[:, None] == jnp.arange(n_exp, dtype=I32)[None, :]).astype(I32), axis=0)
    padded = (counts + tm - 1) // tm * tm
    start = jnp.cumsum(counts) - counts
    pend = jnp.cumsum(padded)
    pstart = pend - padded
    n_tiles = A // tm + n_exp
    tile_row0 = jnp.arange(n_tiles, dtype=I32) * tm
    tile_exp = jnp.minimum(jnp.sum((pend[None, :] <= tile_row0[:, None]).astype(I32), axis=1), n_exp - 1)
    tile_valid = jnp.clip(pstart[tile_exp] + counts[tile_exp] - tile_row0, 0, tm).astype(I32)
    off = tile_row0[:, None] - pstart[tile_exp][:, None] + jnp.arange(tm, dtype=I32)[None, :]
    live = jnp.arange(tm, dtype=I32)[None, :] < tile_valid[:, None]
    src = order[jnp.clip(start[tile_exp][:, None] + off, 0, A - 1)]
    buf_tok = jnp.where(live, src // TOP_K, 0)
    buf_w = jnp.where(live, w_flat[src], 0.0)
    buf_dst = jnp.where(live, (src % TOP_K) * T + src // TOP_K, TOP_K * T + jnp.arange(tm, dtype=I32)[None, :])
    return buf_tok, buf_dst, buf_w[:, :, None], tile_exp, tile_valid


def kernel(x, norm_mix_g, w_in, gla_wa2, gla_ba, gla_norm_g, cmp_pe_k, cmp_wk1, cmp_wk2, cmp_pe_v,
           cmp_wv1, cmp_wv2, nsa_norm_g, rel_bias, w_out, norm_ffn_g, router_w, router_b, exp_w1,
           exp_b1, exp_w2, exp_b2, norm_final_g):
    B, S, D = x.shape
    T = B * S
    G = NSA_KV
    n_exp = router_w.shape[-1]
    assert w_in.shape[0] == 1 and S % 1024 == 0 and S // SEL_BLOCK <= LANE and n_exp <= LANE
    x2 = x.reshape(T, D)

    w = w_in[0]
    a0 = COL_NQ
    n0 = a0 + GLA_RANK
    g0 = n0 + (N_MAIN - COL_NQ)
    w_main = jnp.concatenate([
        w[:, :COL_GK] * (GLA_DK ** -0.5), w[:, COL_GK:a0],
        w[:, n0:n0 + NSA_WIDTH] * (NSA_DH ** -0.5 * LOG2E), w[:, n0 + NSA_WIDTH:g0]], axis=1).astype(BF16)
    w_nar = jnp.concatenate([w[:, a0:n0], w[:, g0:], jnp.zeros((D, LANE - GLA_RANK - 3 * NSA_HEADS), F32)],
                            axis=1).astype(BF16)
    wa2p = jnp.concatenate([gla_wa2[0], jnp.zeros((LANE - GLA_RANK, GLA_HEADS * GLA_DK), F32)], axis=0).astype(BF16)

    tm1 = min(1024, T)
    proj, log_a, nar = _inproj(x2, norm_mix_g, w_main, w_nar, wa2p, gla_ba, tm1, N_MAIN // 4)

    e_w1, e_w2 = exp_w1[0], exp_w2[0]
    o_gla = _gla(proj, log_a, gla_norm_g, B, S, min(256, S))

    ncp = S // CMP_STRIDE
    half = CMP_STRIDE * NSA_DH

    def stride_groups(col):
        t4 = proj[:, col:col + KV_W].reshape(B, ncp, CMP_STRIDE, G, NSA_DH)
        return t4.transpose(0, 3, 1, 2, 4).reshape(B * G, ncp, half)

    hb = jnp.stack([stride_groups(COL_KC), stride_groups(COL_VC)], axis=0)
    pe = jnp.stack([cmp_pe_k[0].reshape(2, half), cmp_pe_v[0].reshape(2, half)], axis=0)
    w1c = jnp.stack([cmp_wk1[0], cmp_wv1[0]], axis=0).astype(BF16)
    w2c = jnp.stack([cmp_wk2[0], cmp_wv2[0]], axis=0).astype(BF16)
    cmp_kv = _compress(hb, pe, w1c, w2c)

    dtiles, w4, bias_c = _bias_tables(rel_bias, S)
    o_cmp, sel = _cmpattn(proj, cmp_kv, bias_c, _overlap_t(S), B, S)
    o_sel, o_win, w1b, w2b = _selwin(proj, sel, dtiles, w4, e_w1.reshape(-1, e_w1.shape[-1]),
                                     e_w2.reshape(-1, e_w2.shape[-1]), B, S)
    w1b, w2b = w1b.reshape(e_w1.shape), w2b.reshape(e_w2.shape)

    rw = jnp.concatenate([router_w[0], jnp.zeros((D, LANE - n_exp), F32)], axis=1)
    rwh = rw.astype(BF16)
    rwl = (rw - rwh.astype(F32)).astype(BF16)
    rb = jnp.concatenate([router_b[0], jnp.full((LANE - n_exp,), NEG, F32)])[None, :]
    h1, hn, top_i, top_w = _outproj(x2, o_gla, o_cmp, o_sel, o_win, nar, nsa_norm_g, w_out[0].astype(BF16),
                                    norm_ffn_g, rwh, rwl, rb, min(512, T))

    tm6 = min(1024, T)
    tok, dst, wgt, tile_exp, tile_valid = _route(top_i[:, :TOP_K], top_w[:, :TOP_K], n_exp, tm6)
    ys = _moe(tile_exp, tile_valid, tok, dst, hn, wgt, w1b, exp_b1[0][:, None, :],
              w2b, exp_b2[0][:, None, :], TOP_K * T + tm6, tm6, 512)
    out = _combine(ys, h1, norm_final_g[None, :], min(512, T))
    return out.reshape(B, S, D)
```
